```python
import math
import jax, jax.numpy as jnp
from jax import lax
import numpy as np

D_MODEL = 2048
BATCH = 1
SEQ = 8192
DEPTH = 1

ATTN_HEADS = 16
HEAD_DIM = 64
ATTN_WIDTH = ATTN_HEADS * HEAD_DIM
SSM_GROUP = 16
SSM_GROUPS = (D_MODEL - ATTN_WIDTH) // SSM_GROUP
SSM_WIDTH = SSM_GROUPS * SSM_GROUP
MIX_WIDTH = ATTN_WIDTH + SSM_WIDTH
IN_WIDTH = 3 * ATTN_WIDTH + SSM_WIDTH
SSM_STATE = 64
DILATED_PATTERNS = ((128, 1), (512, 4), (2048, 16))
ATTN_BLOCK = 128
ROPE_THETA = 10000.0
NEG_INF = -1e30
PEER_HEADS = 8
PEER_KEY_DIM = 256
PEER_SUB_DIM = PEER_KEY_DIM // 2
PEER_N_KEYS = 128
PEER_N_EXPERTS = PEER_N_KEYS * PEER_N_KEYS
PEER_TOPK = 16
PEER_CHUNK = 128
LN_EPS = 1e-5
DEEPNORM_ALPHA = (2.0 * DEPTH) ** 0.25
DEEPNORM_BETA = (8.0 * DEPTH) ** -0.25
ADA_SCALE = 0.3

kernel_name = "hymba_dilated_s5_peer_deepnorm_block"


def _layernorm(x, g=None, b=None):
    xf = x.astype(jnp.float32)
    mu = jnp.mean(xf, axis=-1, keepdims=True)
    var = jnp.mean(jnp.square(xf - mu), axis=-1, keepdims=True)
    y = (xf - mu) * lax.rsqrt(var + LN_EPS)
    if g is not None:
        y = y * g.astype(jnp.float32) + b.astype(jnp.float32)
    return y.astype(x.dtype)


def _rmsnorm(x, g):
    xf = x.astype(jnp.float32)
    y = xf * lax.rsqrt(jnp.mean(jnp.square(xf), axis=-1, keepdims=True) + LN_EPS)
    return (y * g.astype(jnp.float32)).astype(x.dtype)


def _rope(x, positions):
    half = HEAD_DIM // 2
    inv_freq = ROPE_THETA ** (-jnp.arange(half, dtype=jnp.float32) / half)
    ang = positions.astype(jnp.float32)[..., None] * inv_freq
    cos = jnp.cos(ang)[:, :, None, :]
    sin = jnp.sin(ang)[:, :, None, :]
    xf = x.astype(jnp.float32)
    x1, x2 = xf[..., :half], xf[..., half:]
    return jnp.concatenate([x1 * cos - x2 * sin, x2 * cos + x1 * sin], axis=-1).astype(x.dtype)


def _dilated_window_attention(q, k, v, window, dilation):
    B, S, H, E = q.shape
    L = S // dilation
    W = window // dilation
    nb = -(-L // ATTN_BLOCK)
    Lp = nb * ATTN_BLOCK

    def to_blocks(t):
        t = t.reshape(B, L, dilation, H, E).transpose(0, 2, 1, 3, 4)
        t = jnp.pad(t, ((0, 0), (0, 0), (0, Lp - L), (0, 0), (0, 0)))
        return t.reshape(B, dilation, nb, ATTN_BLOCK, H, E)

    def with_prev(t):
        prev = jnp.pad(t, ((0, 0), (0, 0), (1, 0), (0, 0), (0, 0), (0, 0)))[:, :, :-1]
        return jnp.concatenate([prev, t], axis=3)

    qb, kb, vb = to_blocks(q), to_blocks(k), to_blocks(v)
    kw, vw = with_prev(kb), with_prev(vb)
    s = jnp.einsum('bdnqhe,bdnkhe->bdnhqk', qb, kw,
                   preferred_element_type=jnp.float32) * (E ** -0.5)
    qi = jnp.arange(ATTN_BLOCK)[:, None] + ATTN_BLOCK
    kj = jnp.arange(2 * ATTN_BLOCK)[None, :]
    rel = qi - kj
    band = (rel >= 0) & (rel <= W)
    has_prev = (jnp.arange(nb)[:, None, None] > 0) | (kj >= ATTN_BLOCK)[None]
    mask = band[None] & has_prev
    s = jnp.where(mask[None, None, :, None], s, NEG_INF)
    m = jnp.max(s, axis=-1, keepdims=True)
    p = jnp.exp(s - m)
    l = jnp.sum(p, axis=-1, keepdims=True)
    o = jnp.einsum('bdnhqk,bdnkhe->bdnqhe', p / l, vw.astype(jnp.float32))
    lse = (m + jnp.log(l))[..., 0]
    o = o.reshape(B, dilation, Lp, H, E)[:, :, :L].transpose(0, 2, 1, 3, 4).reshape(B, S, H, E)
    lse = lse.transpose(0, 1, 2, 4, 3).reshape(B, dilation, Lp, H)[:, :, :L]
    lse = lse.transpose(0, 2, 1, 3).reshape(B, S, H)
    return o, lse


def _complex_affine_combine(e1, e2):
    a1r, a1i, b1r, b1i = e1
    a2r, a2i, b2r, b2i = e2
    ar = a2r * a1r - a2i * a1i
    ai = a2r * a1i + a2i * a1r
    br = a2r * b1r - a2i * b1i + b2r
    bi = a2r * b1i + a2i * b1r + b2i
    return (ar, ai, br, bi)


def _s5(u, lam_re, lam_im, log_dt, b_re, b_im, c_re, c_im, d_skip):
    uf = u.astype(jnp.float32)
    dt = jnp.exp(log_dt.astype(jnp.float32))[:, None]
    lr = lam_re.astype(jnp.float32)
    li = lam_im.astype(jnp.float32)
    mag = jnp.exp(lr * dt)
    ar = mag * jnp.cos(li * dt)
    ai = mag * jnp.sin(li * dt)
    den = lr * lr + li * li
    cr = ((ar - 1.0) * lr + ai * li) / den
    ci = (ai * lr - (ar - 1.0) * li) / den
    br = b_re.astype(jnp.float32)
    bi = b_im.astype(jnp.float32)
    bbar_re = cr[..., None] * br - ci[..., None] * bi
    bbar_im = cr[..., None] * bi + ci[..., None] * br
    bu_re = jnp.einsum('bsgp,gnp->bsgn', uf, bbar_re)
    bu_im = jnp.einsum('bsgp,gnp->bsgn', uf, bbar_im)
    a_re = jnp.broadcast_to(ar, bu_re.shape)
    a_im = jnp.broadcast_to(ai, bu_im.shape)
    _, _, h_re, h_im = lax.associative_scan(_complex_affine_combine,
                                            (a_re, a_im, bu_re, bu_im), axis=1)
    y = (jnp.einsum('bsgn,gpn->bsgp', h_re, c_re.astype(jnp.float32))
         - jnp.einsum('bsgn,gpn->bsgp', h_im, c_im.astype(jnp.float32))
         + d_skip.astype(jnp.float32) * uf)
    return y


def _peer(h, w_qp, sub_keys, expert_u, expert_v):
    B, S, D = h.shape
    T = B * S
    t = h.reshape(T, D)
    q = (t @ w_qp).reshape(T, PEER_HEADS, 2, PEER_SUB_DIM)
    s = jnp.einsum('thic,ikc->thik', q, sub_keys,
                   preferred_element_type=jnp.float32)
    top_s, top_i = lax.top_k(s, PEER_TOPK)
    cand = top_s[:, :, 0, :, None] + top_s[:, :, 1, None, :]
    best_s, best = lax.top_k(cand.reshape(T, PEER_HEADS, PEER_TOPK * PEER_TOPK), PEER_TOPK)
    i0 = jnp.take_along_axis(top_i[:, :, 0], best // PEER_TOPK, axis=-1)
    i1 = jnp.take_along_axis(top_i[:, :, 1], best % PEER_TOPK, axis=-1)
    experts = i0 * PEER_N_KEYS + i1
    gates = jax.nn.softmax(best_s, axis=-1)
    nc = T // PEER_CHUNK

    def chunk(args):
        tc, ec, gc = args
        u = expert_u[ec]
        v = expert_v[ec]
        act = jax.nn.gelu(jnp.einsum('td,thkd->thk', tc, u,
                                     preferred_element_type=jnp.float32), approximate=False)
        return jnp.einsum('thk,thkd->td', gc * act, v.astype(jnp.float32))

    y = lax.map(chunk, (t.reshape(nc, PEER_CHUNK, D),
                        experts.reshape(nc, PEER_CHUNK, PEER_HEADS, PEER_TOPK),
                        gates.reshape(nc, PEER_CHUNK, PEER_HEADS, PEER_TOPK)))
    return y.reshape(B, S, D).astype(h.dtype)


def setup_inputs(seed: int = 0) -> dict:
    key = jax.random.key(seed)
    ks = jax.random.split(key, 32)
    f32 = jnp.float32
    nrm = lambda k, shape, scale: jax.random.normal(k, shape, f32) * scale
    L, D = DEPTH, D_MODEL
    x = nrm(ks[0], (BATCH, SEQ, D), 1.0)
    c = nrm(ks[1], (BATCH, D), 1.0)
    positions = jnp.broadcast_to(jnp.arange(SEQ, dtype=jnp.int32), (BATCH, SEQ))
    w_ada = nrm(ks[2], (L, D, 6 * D), ADA_SCALE * D ** -0.5)
    b_ada = nrm(ks[3], (L, 6 * D), 0.02)
    col_scale = jnp.concatenate([jnp.ones((2 * ATTN_WIDTH,), f32),
                                 jnp.full((ATTN_WIDTH,), DEEPNORM_BETA, f32),
                                 jnp.ones((SSM_WIDTH,), f32)])
    w_in = nrm(ks[4], (L, D, IN_WIDTH), D ** -0.5) * col_scale
    lam_re = -0.5 + nrm(ks[5], (L, SSM_GROUPS, SSM_STATE), 0.01)
    lam_im = math.pi * jnp.arange(SSM_STATE, dtype=f32) + nrm(ks[6], (L, SSM_GROUPS, SSM_STATE), 0.01)
    log_dt = jax.random.uniform(ks[7], (L, SSM_GROUPS), f32, math.log(1e-3), math.log(1e-1))
    ssm_b_re = nrm(ks[8], (L, SSM_GROUPS, SSM_STATE, SSM_GROUP), (2 * SSM_GROUP) ** -0.5)
    ssm_b_im = nrm(ks[9], (L, SSM_GROUPS, SSM_STATE, SSM_GROUP), (2 * SSM_GROUP) ** -0.5)
    ssm_c_re = nrm(ks[10], (L, SSM_GROUPS, SSM_GROUP, SSM_STATE), (2 * SSM_STATE) ** -0.5)
    ssm_c_im = nrm(ks[11], (L, SSM_GROUPS, SSM_GROUP, SSM_STATE), (2 * SSM_STATE) ** -0.5)
    ssm_d = nrm(ks[12], (L, SSM_GROUPS, SSM_GROUP), 1.0)
    w_glu = nrm(ks[13], (L, SSM_WIDTH, SSM_WIDTH), SSM_WIDTH ** -0.5)
    b_glu = nrm(ks[14], (L, SSM_WIDTH), 0.02)
    g_attn = 1.0 + nrm(ks[15], (L, ATTN_WIDTH), 0.02)
    g_ssm = 1.0 + nrm(ks[16], (L, SSM_WIDTH), 0.02)
    w_out = nrm(ks[17], (L, MIX_WIDTH, D), DEEPNORM_BETA * MIX_WIDTH ** -0.5)
    ln1_g = 1.0 + nrm(ks[18], (L, D), 0.02)
    ln1_b = nrm(ks[19], (L, D), 0.02)
    w_qp = nrm(ks[20], (L, D, PEER_HEADS * PEER_KEY_DIM), D ** -0.5)
    sub_keys = nrm(ks[21], (L, 2, PEER_N_KEYS, PEER_SUB_DIM), PEER_SUB_DIM ** -0.5)
    expert_u = nrm(ks[22], (L, PEER_N_EXPERTS, D), D ** -0.5)
    expert_v = nrm(ks[23], (L, PEER_N_EXPERTS, D), DEEPNORM_BETA)
    ln2_g = 1.0 + nrm(ks[24], (L, D), 0.02)
    ln2_b = nrm(ks[25], (L, D), 0.02)
    return {"x": x, "c": c, "positions": positions, "w_ada": w_ada, "b_ada": b_ada,
            "w_in": w_in, "lam_re": lam_re, "lam_im": lam_im, "log_dt": log_dt,
            "ssm_b_re": ssm_b_re, "ssm_b_im": ssm_b_im, "ssm_c_re": ssm_c_re,
            "ssm_c_im": ssm_c_im, "ssm_d": ssm_d, "w_glu": w_glu, "b_glu": b_glu,
            "g_attn": g_attn, "g_ssm": g_ssm, "w_out": w_out, "ln1_g": ln1_g,
            "ln1_b": ln1_b, "w_qp": w_qp, "sub_keys": sub_keys, "expert_u": expert_u,
            "expert_v": expert_v, "ln2_g": ln2_g, "ln2_b": ln2_b}


def reference(x, c, positions, w_ada, b_ada, w_in, lam_re, lam_im, log_dt,
              ssm_b_re, ssm_b_im, ssm_c_re, ssm_c_im, ssm_d, w_glu, b_glu,
              g_attn, g_ssm, w_out, ln1_g, ln1_b, w_qp, sub_keys, expert_u,
              expert_v, ln2_g, ln2_b):
    B, S, D = x.shape
    for l in range(DEPTH):
        mod = jax.nn.silu(c) @ w_ada[l] + b_ada[l]
        sh1, sc1, gt1, sh2, sc2, gt2 = [m[:, None, :] for m in jnp.split(mod, 6, axis=-1)]

        h = _layernorm(x) * (1.0 + sc1) + sh1
        z = h @ w_in[l]
        q, k, v, u = jnp.split(z, [ATTN_WIDTH, 2 * ATTN_WIDTH, 3 * ATTN_WIDTH], axis=-1)
        q = _rope(q.reshape(B, S, ATTN_HEADS, HEAD_DIM), positions)
        k = _rope(k.reshape(B, S, ATTN_HEADS, HEAD_DIM), positions)
        v = v.reshape(B, S, ATTN_HEADS, HEAD_DIM)
        outs, lses = [], []
        for window, dilation in DILATED_PATTERNS:
            o_p, lse_p = _dilated_window_attention(q, k, v, window, dilation)
            outs.append(o_p)
            lses.append(lse_p)
        wts = jax.nn.softmax(jnp.stack(lses, axis=0), axis=0)
        attn = jnp.sum(wts[..., None] * jnp.stack(outs, axis=0), axis=0)
        attn = attn.reshape(B, S, ATTN_WIDTH).astype(x.dtype)

        y = _s5(u.reshape(B, S, SSM_GROUPS, SSM_GROUP), lam_re[l], lam_im[l], log_dt[l],
                ssm_b_re[l], ssm_b_im[l], ssm_c_re[l], ssm_c_im[l], ssm_d[l])
        y = jax.nn.gelu(y.reshape(B, S, SSM_WIDTH), approximate=False)
        y = (y * jax.nn.sigmoid(y @ w_glu[l] + b_glu[l])).astype(x.dtype)

        mixed = jnp.concatenate([_rmsnorm(attn, g_attn[l]), _rmsnorm(y, g_ssm[l])], axis=-1)
        mix_out = mixed @ w_out[l]
        x = _layernorm(DEEPNORM_ALPHA * x + (1.0 + gt1) * mix_out, ln1_g[l], ln1_b[l])

        h2 = _layernorm(x) * (1.0 + sc2) + sh2
        ffn = _peer(h2, w_qp[l], sub_keys[l], expert_u[l], expert_v[l])
        x = _layernorm(DEEPNORM_ALPHA * x + (1.0 + gt2) * ffn, ln2_g[l], ln2_b[l])
    return x
```

```python
import functools
import math

import jax
import jax.numpy as jnp
from jax import lax
from jax.experimental import pallas as pl
from jax.experimental.pallas import tpu as pltpu

ATTN_HEADS = 16
HEAD_DIM = 64
ATTN_WIDTH = ATTN_HEADS * HEAD_DIM
SSM_GROUP = 16
SSM_STATE = 64
DILATED_PATTERNS = ((128, 1), (512, 4), (2048, 16))
ATTN_BLOCK = 128
ROPE_THETA = 10000.0
NEG_INF = -1e30
PEER_HEADS = 8
PEER_SUB_DIM = 128
PEER_N_KEYS = 128
PEER_TOPK = 16
LN_EPS = 1e-5
ADA_CHUNKS = 6

LANES = 128
SUBLANES = 8
VMEM_LIMIT_BYTES = 56 * 1024 * 1024

S5_CHUNK = 16
GROUPS_PER_BLOCK = LANES // SSM_GROUP
STATE_PER_BLOCK = GROUPS_PER_BLOCK * SSM_STATE
BIG_NEG = -3.0e38


def _params(sem):
    return pltpu.CompilerParams(dimension_semantics=sem, vmem_limit_bytes=VMEM_LIMIT_BYTES)


def _ln_rows(t):
    mu = jnp.mean(t, axis=-1, keepdims=True)
    d = t - mu
    var = jnp.mean(d * d, axis=-1, keepdims=True)
    return d * lax.rsqrt(var + LN_EPS)


def _ada_kernel(c_ref, w_ref, b_ref, o_ref):
    c = c_ref[...]
    sc = c * jax.nn.sigmoid(c)
    o_ref[...] = jnp.dot(sc, w_ref[...], preferred_element_type=jnp.float32,
                         precision=lax.Precision.HIGHEST) + b_ref[...]


def _ada(c, w_ada, b_ada):
    d = c.shape[-1]
    n = w_ada.shape[-1]
    tn = n // 8
    c8 = jnp.broadcast_to(c.reshape(1, d), (SUBLANES, d))
    out = pl.pallas_call(
        _ada_kernel,
        grid=(n // tn,),
        in_specs=[pl.BlockSpec((SUBLANES, d), lambda j: (0, 0)),
                  pl.BlockSpec((d, tn), lambda j: (0, j)),
                  pl.BlockSpec((1, tn), lambda j: (0, j))],
        out_specs=pl.BlockSpec((SUBLANES, tn), lambda j: (0, j)),
        out_shape=jax.ShapeDtypeStruct((SUBLANES, n), jnp.float32),
        compiler_params=_params(("arbitrary",)),
        name="ada",
    )(c8, w_ada, b_ada.reshape(1, n))
    return out[0:1]


def _inproj_kernel(x_ref, pos_ref, sc_ref, sh_ref, w_ref, z_ref, h_scr, cos_scr, sin_scr, *, n_rope):
    j = pl.program_id(1)
    tn = z_ref.shape[1]

    @pl.when(j == 0)
    def _():
        h = _ln_rows(x_ref[...]) * (1.0 + sc_ref[...]) + sh_ref[...]
        h_scr[...] = h.astype(jnp.bfloat16)
        lane = lax.broadcasted_iota(jnp.int32, (1, LANES), 1)
        fi = ((lane % HEAD_DIM) % (HEAD_DIM // 2)).astype(jnp.float32)
        inv_freq = jnp.exp(fi * (-math.log(ROPE_THETA) / (HEAD_DIM // 2)))
        ang = pos_ref[...].astype(jnp.float32) * inv_freq
        first_half = (lane % HEAD_DIM) < (HEAD_DIM // 2)
        cos_scr[...] = jnp.cos(ang)
        sin_scr[...] = jnp.where(first_half, -jnp.sin(ang), jnp.sin(ang))

    z = jnp.dot(h_scr[...], w_ref[...], preferred_element_type=jnp.float32)

    @pl.when(j < n_rope)
    def _():
        lane = lax.broadcasted_iota(jnp.int32, (1, LANES), 1)
        first_half = (lane % HEAD_DIM) < (HEAD_DIM // 2)
        cos = cos_scr[...]
        sin = sin_scr[...]
        for cc in range(tn // LANES):
            zc = z[:, cc * LANES:(cc + 1) * LANES]
            partner = jnp.where(first_half, pltpu.roll(zc, LANES - HEAD_DIM // 2, 1),
                                pltpu.roll(zc, HEAD_DIM // 2, 1))
            z_ref[:, cc * LANES:(cc + 1) * LANES] = (zc * cos + partner * sin).astype(z_ref.dtype)

    @pl.when(j >= n_rope)
    def _():
        z_ref[...] = z.astype(z_ref.dtype)


def _inproj(x2, pos, sc1, sh1, w_in_bf):
    s, d = x2.shape
    n = w_in_bf.shape[1]
    tm = min(1024, s)
    tn = 512
    n_rope = (2 * ATTN_WIDTH) // tn
    return pl.pallas_call(
        functools.partial(_inproj_kernel, n_rope=n_rope),
        grid=(s // tm, n // tn),
        in_specs=[pl.BlockSpec((tm, d), lambda i, j: (i, 0)),
                  pl.BlockSpec((tm, 1), lambda i, j: (i, 0)),
                  pl.BlockSpec((1, d), lambda i, j: (0, 0)),
                  pl.BlockSpec((1, d), lambda i, j: (0, 0)),
                  pl.BlockSpec((d, tn), lambda i, j: (0, j))],
        out_specs=pl.BlockSpec((tm, tn), lambda i, j: (i, j)),
        out_shape=jax.ShapeDtypeStruct((s, n), jnp.bfloat16),
        scratch_shapes=[pltpu.VMEM((tm, d), jnp.bfloat16),
                        pltpu.VMEM((tm, LANES), jnp.float32),
                        pltpu.VMEM((tm, LANES), jnp.float32)],
        compiler_params=_params(("arbitrary", "arbitrary")),
        name="inproj",
    )(x2, pos, sc1, sh1, w_in_bf)


def _attn_kernel(*refs, nblk, first, last):
    if first:
        q_ref, kc_ref, vc_ref, kp_ref, vp_ref, acc_o, ml_o, kwin, vwin = refs
        acc_i = ml_i = None
    elif last:
        q_ref, kc_ref, vc_ref, kp_ref, vp_ref, acc_i, ml_i, out_o, kwin, vwin = refs
    else:
        q_ref, kc_ref, vc_ref, kp_ref, vp_ref, acc_i, ml_i, acc_o, ml_o, kwin, vwin = refs
    i = pl.program_id(1)
    blk = ATTN_BLOCK
    kwin[0:blk, :] = kp_ref[...]
    kwin[blk:, :] = kc_ref[...]
    vwin[0:blk, :] = vp_ref[...]
    vwin[blk:, :] = vc_ref[...]

    row = lax.broadcasted_iota(jnp.int32, (blk, blk), 0)
    col = lax.broadcasted_iota(jnp.int32, (blk, blk), 1)
    band_prev = col >= row
    band_cur = col <= row
    lane = col
    nt = (((1,), (1,)), ((), ()))
    scale = HEAD_DIM ** -0.5

    def body(j, carry):
        r0 = pl.multiple_of(j * blk, blk)
        has_prev = (i * nblk + j) > 0
        mask_a = jnp.logical_and(band_prev, has_prev)
        if not first:
            mlt = ml_i[pl.ds(r0, blk), :]
        ml_new = jnp.zeros((blk, LANES), jnp.float32)
        for hp in range(ATTN_HEADS // 2):
            cs = slice(hp * LANES, (hp + 1) * LANES)
            q2 = q_ref[pl.ds(r0, blk), cs]
            ka = kwin[pl.ds(r0, blk), cs]
            kb = kwin[pl.ds(r0 + blk, blk), cs]
            va = vwin[pl.ds(r0, blk), cs]
            vb = vwin[pl.ds(r0 + blk, blk), cs]
            if not first:
                acc_prev = acc_i[pl.ds(r0, blk), cs]
            halves = []
            for hd in range(2):
                hidx = 2 * hp + hd
                in_head = (lane < HEAD_DIM) if hd == 0 else (lane >= HEAD_DIM)
                qm = jnp.where(in_head, q2, jnp.zeros_like(q2)) * jnp.asarray(scale, q2.dtype)
                sa = lax.dot_general(qm, ka, nt, preferred_element_type=jnp.float32)
                sb = lax.dot_general(qm, kb, nt, preferred_element_type=jnp.float32)
                sa = jnp.where(mask_a, sa, NEG_INF)
                sb = jnp.where(band_cur, sb, NEG_INF)
                m_blk = jnp.maximum(jnp.max(sa, axis=1, keepdims=True), jnp.max(sb, axis=1, keepdims=True))
                if first:
                    m_new = m_blk
                else:
                    m_prev = mlt[:, hidx:hidx + 1]
                    l_prev = mlt[:, ATTN_HEADS + hidx:ATTN_HEADS + hidx + 1]
                    m_new = jnp.maximum(m_prev, m_blk)
                pa = jnp.exp(sa - m_new)
                pb = jnp.exp(sb - m_new)
                l_new = jnp.sum(pa, axis=1, keepdims=True) + jnp.sum(pb, axis=1, keepdims=True)
                o = (jnp.dot(pa.astype(va.dtype), va, preferred_element_type=jnp.float32)
                     + jnp.dot(pb.astype(vb.dtype), vb, preferred_element_type=jnp.float32))
                if not first:
                    alpha = jnp.exp(m_prev - m_new)
                    l_new = l_new + alpha * l_prev
                    o = o + alpha * acc_prev
                if last:
                    o = o / l_new
                else:
                    ml_new = jnp.where(lane == hidx, m_new, ml_new)
                    ml_new = jnp.where(lane == ATTN_HEADS + hidx, l_new, ml_new)
                halves.append(o)
            o2 = jnp.where(lane < HEAD_DIM, halves[0], halves[1])
            if last:
                out_o[pl.ds(r0, blk), cs] = o2.astype(out_o.dtype)
            else:
                acc_o[pl.ds(r0, blk), cs] = o2
        if not last:
            ml_o[pl.ds(r0, blk), :] = ml_new
        return carry

    lax.fori_loop(0, nblk, body, 0)


def _attn_pattern(z, state, dilation, first, last):
    s = z.shape[0]
    zw = z.shape[1]
    length = s // dilation
    rows = min(1024, length)
    nblk = rows // ATTN_BLOCK
    aw = ATTN_WIDTH
    zc = zw // aw
    zv = z.reshape(length, dilation * zw)
    cur = lambda off: pl.BlockSpec((rows, aw), lambda r, i: (i, r * zc + off))
    prev = lambda off: pl.BlockSpec(
        (ATTN_BLOCK, aw), lambda r, i: (jnp.maximum(i * nblk - 1, 0), r * zc + off))
    in_specs = [cur(0), cur(1), cur(2), prev(1), prev(2)]
    args = [zv, zv, zv, zv, zv]
    acc_spec = pl.BlockSpec((rows, aw), lambda r, i: (i, r))
    ml_spec = pl.BlockSpec((rows, LANES), lambda r, i: (i, r))
    if not first:
        acc, ml = state
        in_specs += [acc_spec, ml_spec]
        args += [acc.reshape(length, dilation * aw), ml.reshape(length, dilation * LANES)]
    if last:
        out_specs = acc_spec
        out_shape = jax.ShapeDtypeStruct((length, dilation * aw), jnp.bfloat16)
    else:
        out_specs = [acc_spec, ml_spec]
        out_shape = [jax.ShapeDtypeStruct((length, dilation * aw), jnp.float32),
                     jax.ShapeDtypeStruct((length, dilation * LANES), jnp.float32)]
    out = pl.pallas_call(
        functools.partial(_attn_kernel, nblk=nblk, first=first, last=last),
        grid=(dilation, length // rows),
        in_specs=in_specs,
        out_specs=out_specs,
        out_shape=out_shape,
        scratch_shapes=[pltpu.VMEM((rows + ATTN_BLOCK, aw), jnp.bfloat16),
                        pltpu.VMEM((rows + ATTN_BLOCK, aw), jnp.bfloat16)],
        compiler_params=_params(("arbitrary", "arbitrary")),
        name=f"attn_d{dilation}",
    )(*args)
    if last:
        return out.reshape(s, aw)
    return out[0].reshape(s, aw), out[1].reshape(s, LANES)


def _attention(z):
    state = None
    n = len(DILATED_PATTERNS)
    for p, (window, dilation) in enumerate(DILATED_PATTERNS):
        assert window // dilation == ATTN_BLOCK
        state = _attn_pattern(z, state, dilation, p == 0, p == n - 1)
    return state


def _s5_param_kernel(lr_r, li_r, ld_r, lr_c, li_c, ld_c, bre_ref, bim_ref, cre_ref, cim_ref,
                     krev_ref, pblk_ref, qblk_ref, at_ref):
    t_chunk = S5_CHUNK
    nst = STATE_PER_BLOCK

    def powers(lr, li, ld, tau):
        dt = jnp.exp(ld)
        mag = jnp.exp(lr * dt * tau)
        return mag * jnp.cos(li * dt * tau), mag * jnp.sin(li * dt * tau)

    lr, li, ld = lr_r[...], li_r[...], ld_r[...]
    ar, ai = powers(lr, li, ld, 1.0)
    den = lr * lr + li * li
    cr = ((ar - 1.0) * lr + ai * li) / den
    ci = (ai * lr - (ar - 1.0) * li) / den
    bre, bim = bre_ref[...], bim_ref[...]
    bbr = cr * bre - ci * bim
    bbi = cr * bim + ci * bre
    cre, cim = cre_ref[...], cim_ref[...]
    hi = lax.Precision.HIGHEST
    for tau in range(t_chunk):
        er, ei = powers(lr, li, ld, float(tau))
        pr = er * bbr - ei * bbi
        pi = er * bbi + ei * bbr
        blk = t_chunk - 1 - tau
        rows = slice(blk * LANES, (blk + 1) * LANES)
        pblk_ref[rows, 0:nst] = pr.astype(pblk_ref.dtype)
        pblk_ref[rows, nst:2 * nst] = pi.astype(pblk_ref.dtype)
        k_tau = (jnp.dot(pr, cre, preferred_element_type=jnp.float32, precision=hi)
                 - jnp.dot(pi, cim, preferred_element_type=jnp.float32, precision=hi))
        krev_ref[rows, :] = k_tau.astype(krev_ref.dtype)
    lrc, lic, ldc = lr_c[...], li_c[...], ld_c[...]
    for t in range(t_chunk):
        er, ei = powers(lrc, lic, ldc, float(t + 1))
        qblk_ref[t, 0:nst, :] = (cre * er - cim * ei).astype(qblk_ref.dtype)
        qblk_ref[t, nst:2 * nst, :] = (-(cre * ei + cim * er)).astype(qblk_ref.dtype)
    er, ei = powers(lr, li, ld, float(t_chunk))
    at_ref[:, 0:nst] = er
    at_ref[:, nst:2 * nst] = ei


def _s5_params(lam_re, lam_im, log_dt, b_re, b_im, c_re, c_im):
    g, n = lam_re.shape
    p = b_re.shape[-1]
    gb = GROUPS_PER_BLOCK
    nlb = g // gb
    nst = gb * n
    f32 = jnp.float32
    eye = jnp.eye(gb, dtype=f32)

    def row(a):
        return a.astype(f32).reshape(nlb, 1, nst)

    def colv(a):
        return a.astype(f32).reshape(nlb, nst, 1)

    ldt = jnp.broadcast_to(log_dt.astype(f32)[:, None], (g, n))

    def b_blockdiag(b):
        b4 = b.astype(f32).reshape(nlb, gb, n, p)
        return jnp.einsum("lhnq,gh->lgqhn", b4, eye).reshape(nlb, gb * p, nst)

    def c_blockdiag(c):
        c4 = c.astype(f32).reshape(nlb, gb, p, n)
        return jnp.einsum("lhpn,gh->lhngp", c4, eye).reshape(nlb, nst, gb * p)

    tc = S5_CHUNK
    vec_r = pl.BlockSpec((None, 1, nst), lambda l: (l, 0, 0))
    vec_c = pl.BlockSpec((None, nst, 1), lambda l: (l, 0, 0))
    return pl.pallas_call(
        _s5_param_kernel,
        grid=(nlb,),
        in_specs=[vec_r, vec_r, vec_r, vec_c, vec_c, vec_c,
                  pl.BlockSpec((None, LANES, nst), lambda l: (l, 0, 0)),
                  pl.BlockSpec((None, LANES, nst), lambda l: (l, 0, 0)),
                  pl.BlockSpec((None, nst, LANES), lambda l: (l, 0, 0)),
                  pl.BlockSpec((None, nst, LANES), lambda l: (l, 0, 0))],
        out_specs=[pl.BlockSpec((None, tc * LANES, LANES), lambda l: (l, 0, 0)),
                   pl.BlockSpec((None, tc * LANES, 2 * nst), lambda l: (l, 0, 0)),
                   pl.BlockSpec((None, tc, 2 * nst, LANES), lambda l: (l, 0, 0, 0)),
                   pl.BlockSpec((None, 1, 2 * nst), lambda l: (l, 0, 0))],
        out_shape=[jax.ShapeDtypeStruct((nlb, tc * LANES, LANES), jnp.bfloat16),
                   jax.ShapeDtypeStruct((nlb, tc * LANES, 2 * nst), jnp.bfloat16),
                   jax.ShapeDtypeStruct((nlb, tc, 2 * nst, LANES), jnp.bfloat16),
                   jax.ShapeDtypeStruct((nlb, 1, 2 * nst), f32)],
        compiler_params=_params(("arbitrary",)),
        name="s5_params",
    )(row(lam_re), row(lam_im), row(ldt), colv(lam_re), colv(lam_im), colv(ldt),
      b_blockdiag(b_re), b_blockdiag(b_im), c_blockdiag(c_re), c_blockdiag(c_im))


def _load_ucat(u_ref, uf_scr, ucat_scr, nc):
    uf_scr[...] = u_ref[...].astype(jnp.float32)
    for s in range(S5_CHUNK):
        ucat_scr[:, s * LANES:(s + 1) * LANES] = (
            uf_scr[pl.ds(s, nc, stride=S5_CHUNK), :].astype(ucat_scr.dtype))


def _s5_state_in_kernel(u_ref, pblk_ref, b_ref, uf_scr, ucat_scr, *, nc):
    _load_ucat(u_ref, uf_scr, ucat_scr, nc)
    b_ref[...] = jnp.dot(ucat_scr[...], pblk_ref[...], preferred_element_type=jnp.float32)


def _s5_scan_kernel(b_ref, at_ref, h_ref, *, nc, nlb):
    nst = STATE_PER_BLOCK
    a_re = at_ref[:, 0:nst]
    a_im = at_ref[:, nst:2 * nst]

    def body(c, carry):
        hr, hi = carry
        h_ref[c, :, 0:nst] = hr
        h_ref[c, :, nst:2 * nst] = hi
        bc = b_ref[c]
        return (a_re * hr - a_im * hi + bc[:, 0:nst], a_re * hi + a_im * hr + bc[:, nst:2 * nst])

    zero = jnp.zeros((nlb, nst), jnp.float32)
    lax.fori_loop(0, nc, body, (zero, zero))


def _s5_out_kernel(u_ref, h_ref, krev_ref, qblk_ref, dsk_ref, y_ref, uf_scr, ucat_scr, y_scr, *, nc):
    _load_ucat(u_ref, uf_scr, ucat_scr, nc)
    hb = h_ref[...].astype(jnp.bfloat16)
    dsk = dsk_ref[...]
    tc = S5_CHUNK
    for t in range(tc):
        kd = (t + 1) * LANES
        y = jnp.dot(ucat_scr[:, 0:kd], krev_ref[(tc - 1 - t) * LANES:, :],
                    preferred_element_type=jnp.float32)
        y = y + jnp.dot(hb, qblk_ref[t], preferred_element_type=jnp.float32)
        y = y + dsk * uf_scr[pl.ds(t, nc, stride=tc), :]
        y = 0.5 * y * (1.0 + lax.erf(y * (2.0 ** -0.5)))
        y_scr[pl.ds(t, nc, stride=tc), :] = y
    y_ref[...] = y_scr[...].astype(y_ref.dtype)


def _s5(z, lam_re, lam_im, log_dt, b_re, b_im, c_re, c_im, d_skip):
    s = z.shape[0]
    nlb = lam_re.shape[0] // GROUPS_PER_BLOCK
    nc = s // S5_CHUNK
    nst2 = 2 * STATE_PER_BLOCK
    u_col0 = (3 * ATTN_WIDTH) // LANES
    krev, pblk, qblk, at = _s5_params(lam_re, lam_im, log_dt, b_re, b_im, c_re, c_im)
    u_spec = pl.BlockSpec((s, LANES), lambda l: (0, u_col0 + l))
    b2 = pl.pallas_call(
        functools.partial(_s5_state_in_kernel, nc=nc),
        grid=(nlb,),
        in_specs=[u_spec, pl.BlockSpec((None, S5_CHUNK * LANES, nst2), lambda l: (l, 0, 0))],
        out_specs=pl.BlockSpec((nc, nst2), lambda l: (0, l)),
        out_shape=jax.ShapeDtypeStruct((nc, nlb * nst2), jnp.float32),
        scratch_shapes=[pltpu.VMEM((s, LANES), jnp.float32),
                        pltpu.VMEM((nc, S5_CHUNK * LANES), jnp.bfloat16)],
        compiler_params=_params(("arbitrary",)),
        name="s5_state_in",
    )(z, pblk)
    h3 = pl.pallas_call(
        functools.partial(_s5_scan_kernel, nc=nc, nlb=nlb),
        out_shape=jax.ShapeDtypeStruct((nc, nlb, nst2), jnp.float32),
        compiler_params=pltpu.CompilerParams(vmem_limit_bytes=VMEM_LIMIT_BYTES),
        name="s5_scan",
    )(b2.reshape(nc, nlb, nst2), at.reshape(nlb, nst2))
    return pl.pallas_call(
        functools.partial(_s5_out_kernel, nc=nc),
        grid=(nlb,),
        in_specs=[u_spec,
                  pl.BlockSpec((nc, nst2), lambda l: (0, l)),
                  pl.BlockSpec((None, S5_CHUNK * LANES, LANES), lambda l: (l, 0, 0)),
                  pl.BlockSpec((None, S5_CHUNK, nst2, LANES), lambda l: (l, 0, 0, 0)),
                  pl.BlockSpec((1, LANES), lambda l: (0, l))],
        out_specs=pl.BlockSpec((s, LANES), lambda l: (0, l)),
        out_shape=jax.ShapeDtypeStruct((s, nlb * LANES), jnp.bfloat16),
        scratch_shapes=[pltpu.VMEM((s, LANES), jnp.float32),
                        pltpu.VMEM((nc, S5_CHUNK * LANES), jnp.bfloat16),
                        pltpu.VMEM((s, LANES), jnp.float32)],
        compiler_params=_params(("arbitrary",)),
        name="s5_out",
    )(z, h3.reshape(nc, nlb * nst2), krev, qblk, d_skip.astype(jnp.float32).reshape(1, -1))


def _mixout_kernel(attn_ref, yg_ref, x_ref, wglu_ref, bglu_ref, ga_ref, gs_ref, wout_ref,
                   gt_ref, lg_ref, lb_ref, sc_ref, sh_ref, x1_ref, h2_ref, *, alpha):
    aw = attn_ref.shape[1]
    a = attn_ref[...].astype(jnp.float32)
    ra = a * lax.rsqrt(jnp.mean(a * a, axis=-1, keepdims=True) + LN_EPS) * ga_ref[...]
    yg = yg_ref[...]
    y = yg.astype(jnp.float32)
    gate = jax.nn.sigmoid(jnp.dot(yg, wglu_ref[...], preferred_element_type=jnp.float32) + bglu_ref[...])
    y = y * gate
    ry = y * lax.rsqrt(jnp.mean(y * y, axis=-1, keepdims=True) + LN_EPS) * gs_ref[...]
    mix = (jnp.dot(ra.astype(jnp.bfloat16), wout_ref[0:aw, :], preferred_element_type=jnp.float32)
           + jnp.dot(ry.astype(jnp.bfloat16), wout_ref[aw:, :], preferred_element_type=jnp.float32))
    t = alpha * x_ref[...] + (1.0 + gt_ref[...]) * mix
    x1 = _ln_rows(t) * lg_ref[...] + lb_ref[...]
    x1_ref[...] = x1
    h2_ref[...] = (_ln_rows(x1) * (1.0 + sc_ref[...]) + sh_ref[...]).astype(h2_ref.dtype)


def _mixout(attn, yg, x2, w_glu_bf, b_glu, g_attn, g_ssm, w_out_bf, gt1, ln_g, ln_b, sc2, sh2, alpha):
    s, d = x2.shape
    aw = attn.shape[1]
    sw = yg.shape[1]
    tm = min(512, s)
    row = lambda i: (i, 0)
    fix = lambda i: (0, 0)
    vec = lambda n: pl.BlockSpec((1, n), fix)
    return pl.pallas_call(
        functools.partial(_mixout_kernel, alpha=alpha),
        grid=(s // tm,),
        in_specs=[pl.BlockSpec((tm, aw), row), pl.BlockSpec((tm, sw), row), pl.BlockSpec((tm, d), row),
                  pl.BlockSpec((sw, sw), fix), vec(sw), vec(aw), vec(sw),
                  pl.BlockSpec((aw + sw, d), fix), vec(d), vec(d), vec(d), vec(d), vec(d)],
        out_specs=[pl.BlockSpec((tm, d), row), pl.BlockSpec((tm, d), row)],
        out_shape=[jax.ShapeDtypeStruct((s, d), jnp.float32), jax.ShapeDtypeStruct((s, d), jnp.bfloat16)],
        compiler_params=_params(("arbitrary",)),
        name="mixout",
    )(attn, yg, x2, w_glu_bf, b_glu.reshape(1, -1), g_attn.reshape(1, -1), g_ssm.reshape(1, -1),
      w_out_bf, gt1, ln_g.reshape(1, -1), ln_b.reshape(1, -1), sc2, sh2)


def _top_values(sc, k):
    vals = []
    for _ in range(k):
        m = jnp.max(sc, axis=0, keepdims=True)
        vals.append(m)
        sc = jnp.where(sc == m, BIG_NEG, sc)
    return vals


def _peerq_kernel(h2_ref, wqp_ref, keys_ref, a0_ref, n1_ref, e0_ref, e1_ref):
    k = PEER_TOPK
    qp = jnp.dot(h2_ref[...], wqp_ref[...], preferred_element_type=jnp.float32).astype(jnp.bfloat16)
    nt = (((1,), (1,)), ((), ()))
    for h in range(PEER_HEADS):
        st = []
        for i in range(2):
            c0 = (2 * h + i) * PEER_SUB_DIM
            st.append(lax.dot_general(keys_ref[i], qp[:, c0:c0 + PEER_SUB_DIM], nt,
                                      preferred_element_type=jnp.float32))
        s0, s1 = st
        top0 = _top_values(s0, k + 1)
        top1 = _top_values(s1, k + 1)
        cands = [top0[i] + top1[j] for i in range(k + 1) for j in range(k + 1) if (i + 1) * (j + 1) <= k + 1]
        pad = (-len(cands)) % SUBLANES
        cmat = jnp.concatenate(cands + [jnp.full_like(cands[0], BIG_NEG)] * pad, axis=0)
        best = _top_values(cmat, k + 1)
        thr = 0.5 * (best[k - 1] + best[k])
        m0, m1 = top0[0], top1[0]
        zsum = jnp.sum(jnp.where(cmat >= thr, jnp.exp(cmat - (m0 + m1)), 0.0), axis=0, keepdims=True)
        a0_ref[h] = s0 - thr
        n1_ref[h] = -s1
        e0_ref[h] = jnp.exp(s0 - m0) / zsum
        e1_ref[h] = jnp.exp(s1 - m1)


def _peerq(h2, w_qp_bf, keys_bf):
    s, d = h2.shape
    tm = min(256, s)
    nk = keys_bf.shape[1]
    stat = pl.BlockSpec((PEER_HEADS, nk, tm), lambda i: (0, 0, i))
    shp = jax.ShapeDtypeStruct((PEER_HEADS, nk, s), jnp.float32)
    return pl.pallas_call(
        _peerq_kernel,
        grid=(s // tm,),
        in_specs=[pl.BlockSpec((tm, d), lambda i: (i, 0)),
                  pl.BlockSpec(w_qp_bf.shape, lambda i: (0, 0)),
                  pl.BlockSpec(keys_bf.shape, lambda i: (0, 0, 0))],
        out_specs=[stat, stat, stat, stat],
        out_shape=[shp, shp, shp, shp],
        compiler_params=_params(("arbitrary",)),
        name="peerq",
    )(h2, w_qp_bf, keys_bf)


def _peer_kernel(u_ref, vt_ref, h2t_ref, a0_ref, e0_ref, n1_ref, e1_ref, o_ref, a_scr, w_scr, *, lane_chunk):
    j = pl.program_id(1)
    nk = PEER_N_KEYS
    te, tm = a_scr.shape

    @pl.when(j == 0)
    def _():
        o_ref[...] = jnp.zeros_like(o_ref)

    a_scr[...] = jnp.dot(u_ref[...], h2t_ref[...], preferred_element_type=jnp.float32)

    def body(r, carry):
        r0 = pl.multiple_of(r * nk, nk)
        for lc in range(tm // lane_chunk):
            ls = slice(lc * lane_chunk, (lc + 1) * lane_chunk)
            a = a_scr[pl.ds(r0, nk), ls]
            act = 0.5 * a * (1.0 + lax.erf(a * (2.0 ** -0.5)))
            gate = jnp.zeros_like(a)
            for h in range(PEER_HEADS):
                a0 = a0_ref[h, pl.ds(r, 1), ls]
                e0 = e0_ref[h, pl.ds(r, 1), ls]
                gate = gate + jnp.where(a0 >= n1_ref[h, :, ls], e0 * e1_ref[h, :, ls], 0.0)
            w_scr[pl.ds(r0, nk), ls] = (gate * act).astype(w_scr.dtype)
        return carry

    lax.fori_loop(0, te // nk, body, 0)
    o_ref[...] += jnp.dot(vt_ref[...], w_scr[...], preferred_element_type=jnp.float32)


def _peer(h2t, u_bf, vt_bf, a0, n1, e0, e1):
    d, s = h2t.shape
    ne = u_bf.shape[0]
    nk = PEER_N_KEYS
    tm = min(512, s)
    rows_i0 = SUBLANES
    te = rows_i0 * nk
    row_stat = pl.BlockSpec((PEER_HEADS, rows_i0, tm), lambda i, j: (0, j, i))
    full_stat = pl.BlockSpec((PEER_HEADS, nk, tm), lambda i, j: (0, 0, i))
    return pl.pallas_call(
        functools.partial(_peer_kernel, lane_chunk=min(256, tm)),
        grid=(s // tm, ne // te),
        in_specs=[pl.BlockSpec((te, d), lambda i, j: (j, 0)),
                  pl.BlockSpec((d, te), lambda i, j: (0, j)),
                  pl.BlockSpec((d, tm), lambda i, j: (0, i)),
                  row_stat, row_stat, full_stat, full_stat],
        out_specs=pl.BlockSpec((d, tm), lambda i, j: (0, i)),
        out_shape=jax.ShapeDtypeStruct((d, s), jnp.float32),
        scratch_shapes=[pltpu.VMEM((te, tm), jnp.float32), pltpu.VMEM((te, tm), jnp.bfloat16)],
        compiler_params=_params(("arbitrary", "arbitrary")),
        name="peer",
    )(u_bf, vt_bf, h2t, a0, e0, n1, e1)


def _final_kernel(ft_ref, x1_ref, gt_ref, lg_ref, lb_ref, o_ref, *, alpha):
    ffn = ft_ref[...].T
    t = alpha * x1_ref[...] + (1.0 + gt_ref[...]) * ffn
    o_ref[...] = _ln_rows(t) * lg_ref[...] + lb_ref[...]


def _final(ffn_t, x1, gt2, ln_g, ln_b, alpha):
    s, d = x1.shape
    tm = min(512, s)
    vec = pl.BlockSpec((1, d), lambda i: (0, 0))
    return pl.pallas_call(
        functools.partial(_final_kernel, alpha=alpha),
        grid=(s // tm,),
        in_specs=[pl.BlockSpec((d, tm), lambda i: (0, i)), pl.BlockSpec((tm, d), lambda i: (i, 0)),
                  vec, vec, vec],
        out_specs=pl.BlockSpec((tm, d), lambda i: (i, 0)),
        out_shape=jax.ShapeDtypeStruct((s, d), jnp.float32),
        compiler_params=_params(("arbitrary",)),
        name="final",
    )(ffn_t, x1, gt2, ln_g.reshape(1, -1), ln_b.reshape(1, -1))


def kernel(x, c, positions, w_ada, b_ada, w_in, lam_re, lam_im, log_dt, ssm_b_re, ssm_b_im, ssm_c_re,
           ssm_c_im, ssm_d, w_glu, b_glu, g_attn, g_ssm, w_out, ln1_g, ln1_b, w_qp, sub_keys, expert_u,
           expert_v, ln2_g, ln2_b):
    b, s, d = x.shape
    assert b == 1, "one sequence per call"
    depth = w_ada.shape[0]
    alpha = (2.0 * depth) ** 0.25
    bf = jnp.bfloat16
    x2 = x.reshape(s, d)
    pos = positions.reshape(s, 1)
    for l in range(depth):
        mod = _ada(c, w_ada[l], b_ada[l])
        sh1, sc1, gt1, sh2, sc2, gt2 = [mod[:, k * d:(k + 1) * d] for k in range(ADA_CHUNKS)]
        z = _inproj(x2, pos, sc1, sh1, w_in[l].astype(bf))
        attn = _attention(z)
        yg = _s5(z, lam_re[l], lam_im[l], log_dt[l], ssm_b_re[l], ssm_b_im[l],
                 ssm_c_re[l], ssm_c_im[l], ssm_d[l])
        x1, h2 = _mixout(attn, yg, x2, w_glu[l].astype(bf), b_glu[l], g_attn[l], g_ssm[l],
                         w_out[l].astype(bf), gt1, ln1_g[l], ln1_b[l], sc2, sh2, alpha)
        a0, n1, e0, e1 = _peerq(h2, w_qp[l].astype(bf), sub_keys[l].astype(bf))
        ffn_t = _peer(h2.T, expert_u[l].astype(bf), expert_v[l].T.astype(bf), a0, n1, e0, e1)
        x2 = _final(ffn_t, x1, gt2, ln2_g[l], ln2_b[l], alpha)
    return x2.reshape(b, s, d)
```

```python
import functools
import math

import jax
import jax.numpy as jnp
from jax import lax
from jax.experimental import pallas as pl
from jax.experimental.pallas import tpu as pltpu

ATTN_HEADS = 16
HEAD_DIM = 64
ATTN_WIDTH = ATTN_HEADS * HEAD_DIM
SSM_GROUP = 16
SSM_STATE = 64
DILATED_PATTERNS = ((128, 1), (512, 4), (2048, 16))
ATTN_BLOCK = 128
ROPE_THETA = 10000.0
NEG_INF = -1e30
PEER_HEADS = 8
PEER_SUB_DIM = 128
PEER_N_KEYS = 128
PEER_TOPK = 16
LN_EPS = 1e-5
ADA_CHUNKS = 6

LANES = 128
SUBLANES = 8
VMEM_LIMIT_BYTES = 56 * 1024 * 1024

S5_CHUNK = 16
GROUPS_PER_BLOCK = LANES // SSM_GROUP
STATE_PER_BLOCK = GROUPS_PER_BLOCK * SSM_STATE
BIG_NEG = -3.0e38


def _params(sem):
    return pltpu.CompilerParams(dimension_semantics=sem, vmem_limit_bytes=VMEM_LIMIT_BYTES)


def _ln_rows(t):
    mu = jnp.mean(t, axis=-1, keepdims=True)
    d = t - mu
    var = jnp.mean(d * d, axis=-1, keepdims=True)
    return d * lax.rsqrt(var + LN_EPS)


def _ada_kernel(c_ref, w_ref, b_ref, o_ref):
    c = c_ref[...]
    sc = c * jax.nn.sigmoid(c)
    o_ref[...] = jnp.dot(sc, w_ref[...], preferred_element_type=jnp.float32,
                         precision=lax.Precision.HIGHEST) + b_ref[...]


def _ada(c, w_ada, b_ada):
    d = c.shape[-1]
    n = w_ada.shape[-1]
    tn = n // 8
    c8 = jnp.broadcast_to(c.reshape(1, d), (SUBLANES, d))
    out = pl.pallas_call(
        _ada_kernel,
        grid=(n // tn,),
        in_specs=[pl.BlockSpec((SUBLANES, d), lambda j: (0, 0)),
                  pl.BlockSpec((d, tn), lambda j: (0, j)),
                  pl.BlockSpec((1, tn), lambda j: (0, j))],
        out_specs=pl.BlockSpec((SUBLANES, tn), lambda j: (0, j)),
        out_shape=jax.ShapeDtypeStruct((SUBLANES, n), jnp.float32),
        compiler_params=_params(("arbitrary",)),
        name="ada",
    )(c8, w_ada, b_ada.reshape(1, n))
    return out[0:1]


def _inproj_kernel(x_ref, pos_ref, sc_ref, sh_ref, w_ref, z_ref, *rest, n_rope, dilations):
    dil_refs = rest[:len(dilations)]
    h_scr, cos_scr, sin_scr, z_scr = rest[len(dilations):]
    j = pl.program_id(1)
    tm, tn = z_ref.shape

    @pl.when(j == 0)
    def _():
        h = _ln_rows(x_ref[...]) * (1.0 + sc_ref[...]) + sh_ref[...]
        h_scr[...] = h.astype(jnp.bfloat16)
        lane = lax.broadcasted_iota(jnp.int32, (1, LANES), 1)
        fi = ((lane % HEAD_DIM) % (HEAD_DIM // 2)).astype(jnp.float32)
        inv_freq = jnp.exp(fi * (-math.log(ROPE_THETA) / (HEAD_DIM // 2)))
        ang = pos_ref[...].astype(jnp.float32) * inv_freq
        first_half = (lane % HEAD_DIM) < (HEAD_DIM // 2)
        cos_scr[...] = jnp.cos(ang)
        sin_scr[...] = jnp.where(first_half, -jnp.sin(ang), jnp.sin(ang))

    z = jnp.dot(h_scr[...], w_ref[...], preferred_element_type=jnp.float32)

    @pl.when(j < n_rope)
    def _():
        lane = lax.broadcasted_iota(jnp.int32, (1, LANES), 1)
        first_half = (lane % HEAD_DIM) < (HEAD_DIM // 2)
        cos = cos_scr[...]
        sin = sin_scr[...]
        for cc in range(tn // LANES):
            zc = z[:, cc * LANES:(cc + 1) * LANES]
            partner = jnp.where(first_half, pltpu.roll(zc, LANES - HEAD_DIM // 2, 1),
                                pltpu.roll(zc, HEAD_DIM // 2, 1))
            z_scr[cc] = zc * cos + partner * sin

    @pl.when(j >= n_rope)
    def _():
        for cc in range(tn // LANES):
            z_scr[cc] = z[:, cc * LANES:(cc + 1) * LANES]

    for cc in range(tn // LANES):
        z_ref[:, cc * LANES:(cc + 1) * LANES] = z_scr[cc].astype(z_ref.dtype)

    @pl.when(j < (3 * ATTN_WIDTH) // tn)
    def _():
        for dil, ref in zip(dilations, dil_refs):
            for r in range(dil):
                for cc in range(tn // LANES):
                    ref[r, :, cc * LANES:(cc + 1) * LANES] = (
                        z_scr[cc, pl.ds(r, tm // dil, stride=dil), :].astype(ref.dtype))


def _inproj(x2, pos, sc1, sh1, w_in_bf, dilations):
    s, d = x2.shape
    n = w_in_bf.shape[1]
    tm = min(1024, s)
    tn = 512
    n_rope = (2 * ATTN_WIDTH) // tn
    n_qkv = (3 * ATTN_WIDTH) // tn
    bf = jnp.bfloat16
    dil_specs = [pl.BlockSpec((dil, tm // dil, tn), lambda i, j: (0, i, jnp.minimum(j, n_qkv - 1)))
                 for dil in dilations]
    dil_shapes = [jax.ShapeDtypeStruct((dil, s // dil, 3 * ATTN_WIDTH), bf) for dil in dilations]
    return pl.pallas_call(
        functools.partial(_inproj_kernel, n_rope=n_rope, dilations=tuple(dilations)),
        grid=(s // tm, n // tn),
        in_specs=[pl.BlockSpec((tm, d), lambda i, j: (i, 0)),
                  pl.BlockSpec((tm, 1), lambda i, j: (i, 0)),
                  pl.BlockSpec((1, d), lambda i, j: (0, 0)),
                  pl.BlockSpec((1, d), lambda i, j: (0, 0)),
                  pl.BlockSpec((d, tn), lambda i, j: (0, j))],
        out_specs=[pl.BlockSpec((tm, tn), lambda i, j: (i, j))] + dil_specs,
        out_shape=[jax.ShapeDtypeStruct((s, n), bf)] + dil_shapes,
        scratch_shapes=[pltpu.VMEM((tm, d), bf),
                        pltpu.VMEM((tm, LANES), jnp.float32),
                        pltpu.VMEM((tm, LANES), jnp.float32),
                        pltpu.VMEM((tn // LANES, tm, LANES), jnp.float32)],
        compiler_params=_params(("arbitrary", "arbitrary")),
        name="inproj",
    )(x2, pos, sc1, sh1, w_in_bf)


def _attn_kernel(q_ref, kc_ref, vc_ref, kp_ref, vp_ref, o_ref, lse_ref, kwin, vt_win, *, nblk):
    i = pl.program_id(1)
    blk = ATTN_BLOCK
    kwin[0:blk, :] = kp_ref[...]
    kwin[blk:, :] = kc_ref[...]
    vt_win[:, 0:blk] = vp_ref[...].T
    vt_win[:, blk:] = vc_ref[...].T

    key = lax.broadcasted_iota(jnp.int32, (2 * blk, blk), 0)
    qry = lax.broadcasted_iota(jnp.int32, (2 * blk, blk), 1)
    lane = lax.broadcasted_iota(jnp.int32, (blk, LANES), 1)
    row = lax.broadcasted_iota(jnp.int32, (blk, LANES), 0)
    nt = (((1,), (1,)), ((), ()))
    scale = HEAD_DIM ** -0.5

    def body(j, carry):
        r0 = pl.multiple_of(j * blk, blk)
        first_key = jnp.where((i * nblk + j) > 0, 0, blk)
        mask = jnp.logical_and(key >= jnp.maximum(qry, first_key), key <= qry + blk)
        lse_rows = []
        for hp in range(ATTN_HEADS // 2):
            cs = slice(hp * LANES, (hp + 1) * LANES)
            q2 = q_ref[pl.ds(r0, blk), cs]
            k2 = kwin[pl.ds(r0, 2 * blk), cs]
            vt2 = vt_win[cs, pl.ds(r0, 2 * blk)]
            halves = []
            for hd in range(2):
                in_head = (lane < HEAD_DIM) if hd == 0 else (lane >= HEAD_DIM)
                qm = jnp.where(in_head, q2, jnp.zeros_like(q2)) * jnp.asarray(scale, q2.dtype)
                st = lax.dot_general(k2, qm, nt, preferred_element_type=jnp.float32)
                st = jnp.where(mask, st, NEG_INF)
                m = jnp.max(st, axis=0, keepdims=True)
                p = jnp.exp(st - m)
                l = jnp.sum(p, axis=0, keepdims=True)
                ot = jnp.dot(vt2, p.astype(vt2.dtype), preferred_element_type=jnp.float32)
                halves.append(ot / l)
                lse_rows.append(m + jnp.log(l))
            ot2 = jnp.where(row < HEAD_DIM, halves[0], halves[1])
            o_ref[pl.ds(r0, blk), cs] = ot2.T.astype(o_ref.dtype)
        lse_mat = jnp.zeros((LANES, blk), jnp.float32)
        for hidx, lse_h in enumerate(lse_rows):
            lse_mat = jnp.where(row == hidx, lse_h, lse_mat)
        lse_ref[pl.ds(r0, blk), :] = lse_mat.T
        return carry

    lax.fori_loop(0, nblk, body, 0)


def _attn_pattern(zd, dilation):
    _, length, zw = zd.shape
    aw = ATTN_WIDTH
    rows = min(1024, length)
    nblk = rows // ATTN_BLOCK
    cur = lambda off: pl.BlockSpec((None, rows, aw), lambda r, i: (r, i, off))
    prev = lambda off: pl.BlockSpec((None, ATTN_BLOCK, aw), lambda r, i: (r, jnp.maximum(i * nblk - 1, 0), off))
    return pl.pallas_call(
        functools.partial(_attn_kernel, nblk=nblk),
        grid=(dilation, length // rows),
        in_specs=[cur(0), cur(1), cur(2), prev(1), prev(2)],
        out_specs=[pl.BlockSpec((None, rows, aw), lambda r, i: (r, i, 0)),
                   pl.BlockSpec((None, rows, LANES), lambda r, i: (r, i, 0))],
        out_shape=[jax.ShapeDtypeStruct((dilation, length, aw), jnp.bfloat16),
                   jax.ShapeDtypeStruct((dilation, length, LANES), jnp.float32)],
        scratch_shapes=[pltpu.VMEM((rows + ATTN_BLOCK, aw), jnp.bfloat16),
                        pltpu.VMEM((aw, rows + ATTN_BLOCK), jnp.bfloat16)],
        compiler_params=_params(("arbitrary", "arbitrary")),
        name=f"attn_d{dilation}",
    )(zd, zd, zd, zd, zd)


def _attention(z, z_dil):
    outs = []
    for (window, dilation), zd in zip(DILATED_PATTERNS, z_dil):
        assert window // dilation == ATTN_BLOCK
        outs.append(_attn_pattern(zd, dilation))
    return outs


def _s5_param_kernel(lr_r, li_r, ld_r, lr_c, li_c, ld_c, bre_ref, bim_ref, cre_ref, cim_ref,
                     krev_ref, pblk_ref, qblk_ref, at_ref):
    t_chunk = S5_CHUNK
    nst = STATE_PER_BLOCK

    def powers(lr, li, ld, tau):
        dt = jnp.exp(ld)
        mag = jnp.exp(lr * dt * tau)
        return mag * jnp.cos(li * dt * tau), mag * jnp.sin(li * dt * tau)

    lr, li, ld = lr_r[...], li_r[...], ld_r[...]
    ar, ai = powers(lr, li, ld, 1.0)
    den = lr * lr + li * li
    cr = ((ar - 1.0) * lr + ai * li) / den
    ci = (ai * lr - (ar - 1.0) * li) / den
    bre, bim = bre_ref[...], bim_ref[...]
    bbr = cr * bre - ci * bim
    bbi = cr * bim + ci * bre
    cre, cim = cre_ref[...], cim_ref[...]
    hi = lax.Precision.HIGHEST
    for tau in range(t_chunk):
        er, ei = powers(lr, li, ld, float(tau))
        pr = er * bbr - ei * bbi
        pi = er * bbi + ei * bbr
        blk = t_chunk - 1 - tau
        rows = slice(blk * LANES, (blk + 1) * LANES)
        pblk_ref[rows, 0:nst] = pr.astype(pblk_ref.dtype)
        pblk_ref[rows, nst:2 * nst] = pi.astype(pblk_ref.dtype)
        k_tau = (jnp.dot(pr, cre, preferred_element_type=jnp.float32, precision=hi)
                 - jnp.dot(pi, cim, preferred_element_type=jnp.float32, precision=hi))
        krev_ref[rows, :] = k_tau.astype(krev_ref.dtype)
    lrc, lic, ldc = lr_c[...], li_c[...], ld_c[...]
    for t in range(t_chunk):
        er, ei = powers(lrc, lic, ldc, float(t + 1))
        qblk_ref[t, 0:nst, :] = (cre * er - cim * ei).astype(qblk_ref.dtype)
        qblk_ref[t, nst:2 * nst, :] = (-(cre * ei + cim * er)).astype(qblk_ref.dtype)
    er, ei = powers(lr, li, ld, float(t_chunk))
    at_ref[:, 0:nst] = er
    at_ref[:, nst:2 * nst] = ei


def _s5_params(lam_re, lam_im, log_dt, b_re, b_im, c_re, c_im):
    g, n = lam_re.shape
    p = b_re.shape[-1]
    gb = GROUPS_PER_BLOCK
    nlb = g // gb
    nst = gb * n
    f32 = jnp.float32
    eye = jnp.eye(gb, dtype=f32)

    def row(a):
        return a.astype(f32).reshape(nlb, 1, nst)

    def colv(a):
        return a.astype(f32).reshape(nlb, nst, 1)

    ldt = jnp.broadcast_to(log_dt.astype(f32)[:, None], (g, n))

    def b_blockdiag(b):
        b4 = b.astype(f32).reshape(nlb, gb, n, p)
        return jnp.einsum("lhnq,gh->lgqhn", b4, eye).reshape(nlb, gb * p, nst)

    def c_blockdiag(c):
        c4 = c.astype(f32).reshape(nlb, gb, p, n)
        return jnp.einsum("lhpn,gh->lhngp", c4, eye).reshape(nlb, nst, gb * p)

    tc = S5_CHUNK
    vec_r = pl.BlockSpec((None, 1, nst), lambda l: (l, 0, 0))
    vec_c = pl.BlockSpec((None, nst, 1), lambda l: (l, 0, 0))
    return pl.pallas_call(
        _s5_param_kernel,
        grid=(nlb,),
        in_specs=[vec_r, vec_r, vec_r, vec_c, vec_c, vec_c,
                  pl.BlockSpec((None, LANES, nst), lambda l: (l, 0, 0)),
                  pl.BlockSpec((None, LANES, nst), lambda l: (l, 0, 0)),
                  pl.BlockSpec((None, nst, LANES), lambda l: (l, 0, 0)),
                  pl.BlockSpec((None, nst, LANES), lambda l: (l, 0, 0))],
        out_specs=[pl.BlockSpec((None, tc * LANES, LANES), lambda l: (l, 0, 0)),
                   pl.BlockSpec((None, tc * LANES, 2 * nst), lambda l: (l, 0, 0)),
                   pl.BlockSpec((None, tc, 2 * nst, LANES), lambda l: (l, 0, 0, 0)),
                   pl.BlockSpec((None, 1, 2 * nst), lambda l: (l, 0, 0))],
        out_shape=[jax.ShapeDtypeStruct((nlb, tc * LANES, LANES), jnp.bfloat16),
                   jax.ShapeDtypeStruct((nlb, tc * LANES, 2 * nst), jnp.bfloat16),
                   jax.ShapeDtypeStruct((nlb, tc, 2 * nst, LANES), jnp.bfloat16),
                   jax.ShapeDtypeStruct((nlb, 1, 2 * nst), f32)],
        compiler_params=_params(("arbitrary",)),
        name="s5_params",
    )(row(lam_re), row(lam_im), row(ldt), colv(lam_re), colv(lam_im), colv(ldt),
      b_blockdiag(b_re), b_blockdiag(b_im), c_blockdiag(c_re), c_blockdiag(c_im))


def _load_ucat(u_ref, uf_scr, ucat_scr, nc):
    uf_scr[...] = u_ref[...].astype(jnp.float32)
    for s in range(S5_CHUNK):
        ucat_scr[:, s * LANES:(s + 1) * LANES] = (
            uf_scr[pl.ds(s, nc, stride=S5_CHUNK), :].astype(ucat_scr.dtype))


def _s5_state_in_kernel(u_ref, pblk_ref, b_ref, uf_scr, ucat_scr, *, nc):
    _load_ucat(u_ref, uf_scr, ucat_scr, nc)
    b_ref[...] = jnp.dot(ucat_scr[...], pblk_ref[...], preferred_element_type=jnp.float32)


def _s5_scan_kernel(b_ref, at_ref, h_ref, *, nc, nlb):
    nst = STATE_PER_BLOCK
    a_re = at_ref[:, 0:nst]
    a_im = at_ref[:, nst:2 * nst]

    def body(c, carry):
        hr, hi = carry
        h_ref[c, :, 0:nst] = hr
        h_ref[c, :, nst:2 * nst] = hi
        bc = b_ref[c]
        return (a_re * hr - a_im * hi + bc[:, 0:nst], a_re * hi + a_im * hr + bc[:, nst:2 * nst])

    zero = jnp.zeros((nlb, nst), jnp.float32)
    lax.fori_loop(0, nc, body, (zero, zero))


def _s5_out_kernel(u_ref, h_ref, krev_ref, qblk_ref, dsk_ref, y_ref, uf_scr, ucat_scr, y_scr, *, nc):
    _load_ucat(u_ref, uf_scr, ucat_scr, nc)
    hb = h_ref[...].astype(jnp.bfloat16)
    dsk = dsk_ref[...]
    tc = S5_CHUNK
    for t in range(tc):
        kd = (t + 1) * LANES
        y = jnp.dot(ucat_scr[:, 0:kd], krev_ref[(tc - 1 - t) * LANES:, :],
                    preferred_element_type=jnp.float32)
        y = y + jnp.dot(hb, qblk_ref[t], preferred_element_type=jnp.float32)
        y = y + dsk * uf_scr[pl.ds(t, nc, stride=tc), :]
        y = 0.5 * y * (1.0 + lax.erf(y * (2.0 ** -0.5)))
        y_scr[pl.ds(t, nc, stride=tc), :] = y
    y_ref[...] = y_scr[...].astype(y_ref.dtype)


def _s5(z, lam_re, lam_im, log_dt, b_re, b_im, c_re, c_im, d_skip):
    s = z.shape[0]
    nlb = lam_re.shape[0] // GROUPS_PER_BLOCK
    nc = s // S5_CHUNK
    nst2 = 2 * STATE_PER_BLOCK
    u_col0 = (3 * ATTN_WIDTH) // LANES
    krev, pblk, qblk, at = _s5_params(lam_re, lam_im, log_dt, b_re, b_im, c_re, c_im)
    u_spec = pl.BlockSpec((s, LANES), lambda l: (0, u_col0 + l))
    b2 = pl.pallas_call(
        functools.partial(_s5_state_in_kernel, nc=nc),
        grid=(nlb,),
        in_specs=[u_spec, pl.BlockSpec((None, S5_CHUNK * LANES, nst2), lambda l: (l, 0, 0))],
        out_specs=pl.BlockSpec((nc, nst2), lambda l: (0, l)),
        out_shape=jax.ShapeDtypeStruct((nc, nlb * nst2), jnp.float32),
        scratch_shapes=[pltpu.VMEM((s, LANES), jnp.float32),
                        pltpu.VMEM((nc, S5_CHUNK * LANES), jnp.bfloat16)],
        compiler_params=_params(("arbitrary",)),
        name="s5_state_in",
    )(z, pblk)
    h3 = pl.pallas_call(
        functools.partial(_s5_scan_kernel, nc=nc, nlb=nlb),
        out_shape=jax.ShapeDtypeStruct((nc, nlb, nst2), jnp.float32),
        compiler_params=pltpu.CompilerParams(vmem_limit_bytes=VMEM_LIMIT_BYTES),
        name="s5_scan",
    )(b2.reshape(nc, nlb, nst2), at.reshape(nlb, nst2))
    return pl.pallas_call(
        functools.partial(_s5_out_kernel, nc=nc),
        grid=(nlb,),
        in_specs=[u_spec,
                  pl.BlockSpec((nc, nst2), lambda l: (0, l)),
                  pl.BlockSpec((None, S5_CHUNK * LANES, LANES), lambda l: (l, 0, 0)),
                  pl.BlockSpec((None, S5_CHUNK, nst2, LANES), lambda l: (l, 0, 0, 0)),
                  pl.BlockSpec((1, LANES), lambda l: (0, l))],
        out_specs=pl.BlockSpec((s, LANES), lambda l: (0, l)),
        out_shape=jax.ShapeDtypeStruct((s, nlb * LANES), jnp.bfloat16),
        scratch_shapes=[pltpu.VMEM((s, LANES), jnp.float32),
                        pltpu.VMEM((nc, S5_CHUNK * LANES), jnp.bfloat16),
                        pltpu.VMEM((s, LANES), jnp.float32)],
        compiler_params=_params(("arbitrary",)),
        name="s5_out",
    )(z, h3.reshape(nc, nlb * nst2), krev, qblk, d_skip.astype(jnp.float32).reshape(1, -1))


def _mixout_kernel(*refs, alpha, n_pat):
    o_refs = refs[:n_pat]
    lse_refs = refs[n_pat:2 * n_pat]
    (yg_ref, x_ref, wglu_ref, bglu_ref, ga_ref, gs_ref, wout_ref,
     gt_ref, lg_ref, lb_ref, sc_ref, sh_ref, x1_ref, h2_ref, o_scr, lse_scr) = refs[2 * n_pat:]
    planes, tm, _ = o_scr.shape
    aw = planes * LANES

    def token_major(ref, scr):
        dil = ref.shape[0]
        if dil == 1:
            return ref[0].astype(jnp.float32)
        n_planes = ref.shape[2] // LANES
        for r in range(dil):
            for cc in range(n_planes):
                scr[cc, pl.ds(r, tm // dil, stride=dil), :] = (
                    ref[r, :, cc * LANES:(cc + 1) * LANES].astype(jnp.float32))
        return jnp.concatenate([scr[cc] for cc in range(n_planes)], axis=1)

    lses = [token_major(ref, lse_scr) for ref in lse_refs]
    top = functools.reduce(jnp.maximum, lses)
    ws = [jnp.exp(v - top) for v in lses]
    inv = 1.0 / functools.reduce(lambda u, v: u + v, ws)
    head_of_lane = lax.broadcasted_iota(jnp.int32, (LANES, aw), 1) // HEAD_DIM
    spread = (lax.broadcasted_iota(jnp.int32, (LANES, aw), 0) == head_of_lane).astype(jnp.bfloat16)
    a = jnp.zeros((tm, aw), jnp.float32)
    for w, o_ref in zip(ws, o_refs):
        w = w * inv
        w_hi = w.astype(jnp.bfloat16)
        w_lo = (w - w_hi.astype(jnp.float32)).astype(jnp.bfloat16)
        wide = (jnp.dot(w_hi, spread, preferred_element_type=jnp.float32)
                + jnp.dot(w_lo, spread, preferred_element_type=jnp.float32))
        a = a + wide * token_major(o_ref, o_scr)
    ra = a * lax.rsqrt(jnp.mean(a * a, axis=-1, keepdims=True) + LN_EPS) * ga_ref[...]
    yg = yg_ref[...]
    y = yg.astype(jnp.float32)
    gate = jax.nn.sigmoid(jnp.dot(yg, wglu_ref[...], preferred_element_type=jnp.float32) + bglu_ref[...])
    y = y * gate
    ry = y * lax.rsqrt(jnp.mean(y * y, axis=-1, keepdims=True) + LN_EPS) * gs_ref[...]
    mix = (jnp.dot(ra.astype(jnp.bfloat16), wout_ref[0:aw, :], preferred_element_type=jnp.float32)
           + jnp.dot(ry.astype(jnp.bfloat16), wout_ref[aw:, :], preferred_element_type=jnp.float32))
    t = alpha * x_ref[...] + (1.0 + gt_ref[...]) * mix
    x1 = _ln_rows(t) * lg_ref[...] + lb_ref[...]
    x1_ref[...] = x1
    h2_ref[...] = (_ln_rows(x1) * (1.0 + sc_ref[...]) + sh_ref[...]).astype(h2_ref.dtype)


def _mixout(attn_parts, yg, x2, w_glu_bf, b_glu, g_attn, g_ssm, w_out_bf, gt1, ln_g, ln_b, sc2, sh2, alpha):
    s, d = x2.shape
    aw = attn_parts[0][0].shape[-1]
    sw = yg.shape[1]
    n_pat = len(attn_parts)
    tm = min(512, s)
    row = lambda i: (i, 0)
    fix = lambda i: (0, 0)
    vec = lambda n: pl.BlockSpec((1, n), fix)
    res = lambda a: pl.BlockSpec((a.shape[0], tm // a.shape[0], a.shape[2]), lambda i: (0, i, 0))
    return pl.pallas_call(
        functools.partial(_mixout_kernel, alpha=alpha, n_pat=n_pat),
        grid=(s // tm,),
        in_specs=([res(o) for o, _ in attn_parts] + [res(v) for _, v in attn_parts]
                  + [pl.BlockSpec((tm, sw), row), pl.BlockSpec((tm, d), row),
                     pl.BlockSpec((sw, sw), fix), vec(sw), vec(aw), vec(sw),
                     pl.BlockSpec((aw + sw, d), fix), vec(d), vec(d), vec(d), vec(d), vec(d)]),
        out_specs=[pl.BlockSpec((tm, d), row), pl.BlockSpec((tm, d), row)],
        out_shape=[jax.ShapeDtypeStruct((s, d), jnp.float32), jax.ShapeDtypeStruct((s, d), jnp.bfloat16)],
        scratch_shapes=[pltpu.VMEM((aw // LANES, tm, LANES), jnp.float32),
                        pltpu.VMEM((1, tm, LANES), jnp.float32)],
        compiler_params=_params(("arbitrary",)),
        name="mixout",
    )(*[o for o, _ in attn_parts], *[l for _, l in attn_parts], yg, x2, w_glu_bf, b_glu.reshape(1, -1),
      g_attn.reshape(1, -1), g_ssm.reshape(1, -1), w_out_bf, gt1, ln_g.reshape(1, -1), ln_b.reshape(1, -1),
      sc2, sh2)


def _top_values(sc, k):
    vals = []
    for _ in range(k):
        m = jnp.max(sc, axis=0, keepdims=True)
        vals.append(m)
        sc = jnp.where(sc == m, BIG_NEG, sc)
    return vals


def _peerq_kernel(h2_ref, wqp_ref, keys_ref, a0_ref, n1_ref, e0_ref, e1_ref):
    k = PEER_TOPK
    qp = jnp.dot(h2_ref[...], wqp_ref[...], preferred_element_type=jnp.float32).astype(jnp.bfloat16)
    nt = (((1,), (1,)), ((), ()))
    for h in range(PEER_HEADS):
        st = []
        for i in range(2):
            c0 = (2 * h + i) * PEER_SUB_DIM
            st.append(lax.dot_general(keys_ref[i], qp[:, c0:c0 + PEER_SUB_DIM], nt,
                                      preferred_element_type=jnp.float32))
        s0, s1 = st
        top0 = _top_values(s0, k + 1)
        top1 = _top_values(s1, k + 1)
        cands = [top0[i] + top1[j] for i in range(k + 1) for j in range(k + 1) if (i + 1) * (j + 1) <= k + 1]
        pad = (-len(cands)) % SUBLANES
        cmat = jnp.concatenate(cands + [jnp.full_like(cands[0], BIG_NEG)] * pad, axis=0)
        best = _top_values(cmat, k + 1)
        thr = 0.5 * (best[k - 1] + best[k])
        m0, m1 = top0[0], top1[0]
        zsum = jnp.sum(jnp.where(cmat >= thr, jnp.exp(cmat - (m0 + m1)), 0.0), axis=0, keepdims=True)
        a0_ref[h] = s0 - thr
        n1_ref[h] = -s1
        e0_ref[h] = jnp.exp(s0 - m0) / zsum
        e1_ref[h] = jnp.exp(s1 - m1)


def _peerq(h2, w_qp_bf, keys_bf):
    s, d = h2.shape
    tm = min(256, s)
    nk = keys_bf.shape[1]
    stat = pl.BlockSpec((PEER_HEADS, nk, tm), lambda i: (0, 0, i))
    shp = jax.ShapeDtypeStruct((PEER_HEADS, nk, s), jnp.float32)
    return pl.pallas_call(
        _peerq_kernel,
        grid=(s // tm,),
        in_specs=[pl.BlockSpec((tm, d), lambda i: (i, 0)),
                  pl.BlockSpec(w_qp_bf.shape, lambda i: (0, 0)),
                  pl.BlockSpec(keys_bf.shape, lambda i: (0, 0, 0))],
        out_specs=[stat, stat, stat, stat],
        out_shape=[shp, shp, shp, shp],
        compiler_params=_params(("arbitrary",)),
        name="peerq",
    )(h2, w_qp_bf, keys_bf)


def _peer_kernel(u_ref, vt_ref, h2t_ref, a0_ref, e0_ref, n1_ref, e1_ref, o_ref, a_scr, w_scr, *, lane_chunk):
    j = pl.program_id(1)
    nk = PEER_N_KEYS
    te, tm = a_scr.shape

    @pl.when(j == 0)
    def _():
        o_ref[...] = jnp.zeros_like(o_ref)

    a_scr[...] = jnp.dot(u_ref[...], h2t_ref[...], preferred_element_type=jnp.float32)

    def body(r, carry):
        r0 = pl.multiple_of(r * nk, nk)
        for lc in range(tm // lane_chunk):
            ls = slice(lc * lane_chunk, (lc + 1) * lane_chunk)
            a = a_scr[pl.ds(r0, nk), ls]
            act = 0.5 * a * (1.0 + lax.erf(a * (2.0 ** -0.5)))
            gate = jnp.zeros_like(a)
            for h in range(PEER_HEADS):
                a0 = a0_ref[h, pl.ds(r, 1), ls]
                e0 = e0_ref[h, pl.ds(r, 1), ls]
                gate = gate + jnp.where(a0 >= n1_ref[h, :, ls], e0 * e1_ref[h, :, ls], 0.0)
            w_scr[pl.ds(r0, nk), ls] = (gate * act).astype(w_scr.dtype)
        return carry

    lax.fori_loop(0, te // nk, body, 0)
    o_ref[...] += jnp.dot(vt_ref[...], w_scr[...], preferred_element_type=jnp.float32)


def _peer(h2t, u_bf, vt_bf, a0, n1, e0, e1):
    d, s = h2t.shape
    ne = u_bf.shape[0]
    nk = PEER_N_KEYS
    tm = min(512, s)
    rows_i0 = SUBLANES
    te = rows_i0 * nk
    row_stat = pl.BlockSpec((PEER_HEADS, rows_i0, tm), lambda i, j: (0, j, i))
    full_stat = pl.BlockSpec((PEER_HEADS, nk, tm), lambda i, j: (0, 0, i))
    return pl.pallas_call(
        functools.partial(_peer_kernel, lane_chunk=min(256, tm)),
        grid=(s // tm, ne // te),
        in_specs=[pl.BlockSpec((te, d), lambda i, j: (j, 0)),
                  pl.BlockSpec((d, te), lambda i, j: (0, j)),
                  pl.BlockSpec((d, tm), lambda i, j: (0, i)),
                  row_stat, row_stat, full_stat, full_stat],
        out_specs=pl.BlockSpec((d, tm), lambda i, j: (0, i)),
        out_shape=jax.ShapeDtypeStruct((d, s), jnp.float32),
        scratch_shapes=[pltpu.VMEM((te, tm), jnp.float32), pltpu.VMEM((te, tm), jnp.bfloat16)],
        compiler_params=_params(("arbitrary", "arbitrary")),
        name="peer",
    )(u_bf, vt_bf, h2t, a0, e0, n1, e1)


def _final_kernel(ft_ref, x1_ref, gt_ref, lg_ref, lb_ref, o_ref, *, alpha):
    ffn = ft_ref[...].T
    t = alpha * x1_ref[...] + (1.0 + gt_ref[...]) * ffn
    o_ref[...] = _ln_rows(t) * lg_ref[...] + lb_ref[...]


def _final(ffn_t, x1, gt2, ln_g, ln_b, alpha):
    s, d = x1.shape
    tm = min(512, s)
    vec = pl.BlockSpec((1, d), lambda i: (0, 0))
    return pl.pallas_call(
        functools.partial(_final_kernel, alpha=alpha),
        grid=(s // tm,),
        in_specs=[pl.BlockSpec((d, tm), lambda i: (0, i)), pl.BlockSpec((tm, d), lambda i: (i, 0)),
                  vec, vec, vec],
        out_specs=pl.BlockSpec((tm, d), lambda i: (i, 0)),
        out_shape=jax.ShapeDtypeStruct((s, d), jnp.float32),
        compiler_params=_params(("arbitrary",)),
        name="final",
    )(ffn_t, x1, gt2, ln_g.reshape(1, -1), ln_b.reshape(1, -1))


def kernel(x, c, positions, w_ada, b_ada, w_in, lam_re, lam_im, log_dt, ssm_b_re, ssm_b_im, ssm_c_re,
           ssm_c_im, ssm_d, w_glu, b_glu, g_attn, g_ssm, w_out, ln1_g, ln1_b, w_qp, sub_keys, expert_u,
           expert_v, ln2_g, ln2_b):
    b, s, d = x.shape
    assert b == 1, "one sequence per call"
    depth = w_ada.shape[0]
    alpha = (2.0 * depth) ** 0.25
    bf = jnp.bfloat16
    x2 = x.reshape(s, d)
    pos = positions.reshape(s, 1)
    dilations = [dil for _, dil in DILATED_PATTERNS if dil > 1]
    for l in range(depth):
        mod = _ada(c, w_ada[l], b_ada[l])
        sh1, sc1, gt1, sh2, sc2, gt2 = [mod[:, k * d:(k + 1) * d] for k in range(ADA_CHUNKS)]
        z, *z_res = _inproj(x2, pos, sc1, sh1, w_in[l].astype(bf), dilations)
        z_dil = [z.reshape(1, s, -1) if dil == 1 else z_res[dilations.index(dil)]
                 for _, dil in DILATED_PATTERNS]
        attn = _attention(z, z_dil)
        yg = _s5(z, lam_re[l], lam_im[l], log_dt[l], ssm_b_re[l], ssm_b_im[l],
                 ssm_c_re[l], ssm_c_im[l], ssm_d[l])
        x1, h2 = _mixout(attn, yg, x2, w_glu[l].astype(bf), b_glu[l], g_attn[l], g_ssm[l],
                         w_out[l].astype(bf), gt1, ln1_g[l], ln1_b[l], sc2, sh2, alpha)
        a0, n1, e0, e1 = _peerq(h2, w_qp[l].astype(bf), sub_keys[l].astype(bf))
        ffn_t = _peer(h2.T, expert_u[l].astype(bf), expert_v[l].T.astype(bf), a0, n1, e0, e1)
        x2 = _final(ffn_t, x1, gt2, ln2_g[l], ln2_b[l], alpha)
    return x2.reshape(b, s, d)
```

```python
import functools
import math

import jax
import jax.numpy as jnp
from jax import lax
from jax.experimental import pallas as pl
from jax.experimental.pallas import tpu as pltpu

ATTN_HEADS = 16
HEAD_DIM = 64
ATTN_WIDTH = ATTN_HEADS * HEAD_DIM
SSM_GROUP = 16
SSM_STATE = 64
DILATED_PATTERNS = ((128, 1), (512, 4), (2048, 16))
ATTN_BLOCK = 128
ROPE_THETA = 10000.0
NEG_INF = -1e30
PEER_HEADS = 8
PEER_SUB_DIM = 128
PEER_N_KEYS = 128
PEER_TOPK = 16
LN_EPS = 1e-5
ADA_CHUNKS = 6

LANES = 128
SUBLANES = 8
VMEM_LIMIT_BYTES = 56 * 1024 * 1024

S5_CHUNK = 16
GROUPS_PER_BLOCK = LANES // SSM_GROUP
STATE_PER_BLOCK = GROUPS_PER_BLOCK * SSM_STATE
BIG_NEG = -3.0e38


def _params(sem):
    return pltpu.CompilerParams(dimension_semantics=sem, vmem_limit_bytes=VMEM_LIMIT_BYTES)


def _ln_rows(t):
    mu = jnp.mean(t, axis=-1, keepdims=True)
    d = t - mu
    var = jnp.mean(d * d, axis=-1, keepdims=True)
    return d * lax.rsqrt(var + LN_EPS)


def _ada_kernel(c_ref, w_ref, b_ref, o_ref):
    c = c_ref[...]
    sc = c * jax.nn.sigmoid(c)
    o_ref[...] = jnp.dot(sc, w_ref[...], preferred_element_type=jnp.float32,
                         precision=lax.Precision.HIGHEST) + b_ref[...]


def _ada(c, w_ada, b_ada):
    d = c.shape[-1]
    n = w_ada.shape[-1]
    tn = n // 8
    c8 = jnp.broadcast_to(c.reshape(1, d), (SUBLANES, d))
    out = pl.pallas_call(
        _ada_kernel,
        grid=(n // tn,),
        in_specs=[pl.BlockSpec((SUBLANES, d), lambda j: (0, 0)),
                  pl.BlockSpec((d, tn), lambda j: (0, j)),
                  pl.BlockSpec((1, tn), lambda j: (0, j))],
        out_specs=pl.BlockSpec((SUBLANES, tn), lambda j: (0, j)),
        out_shape=jax.ShapeDtypeStruct((SUBLANES, n), jnp.float32),
        compiler_params=_params(("arbitrary",)),
        name="ada",
    )(c8, w_ada, b_ada.reshape(1, n))
    return out[0:1]


def _inproj_kernel(x_ref, pos_ref, sc_ref, sh_ref, w_ref, z_ref, *rest, n_rope, dilations):
    dil_refs = rest[:len(dilations)]
    h_scr, cos_scr, sin_scr, z_scr = rest[len(dilations):]
    j = pl.program_id(1)
    tm, tn = z_ref.shape

    @pl.when(j == 0)
    def _():
        h = _ln_rows(x_ref[...]) * (1.0 + sc_ref[...]) + sh_ref[...]
        h_scr[...] = h.astype(jnp.bfloat16)
        lane = lax.broadcasted_iota(jnp.int32, (1, LANES), 1)
        fi = ((lane % HEAD_DIM) % (HEAD_DIM // 2)).astype(jnp.float32)
        inv_freq = jnp.exp(fi * (-math.log(ROPE_THETA) / (HEAD_DIM // 2)))
        ang = pos_ref[...].astype(jnp.float32) * inv_freq
        first_half = (lane % HEAD_DIM) < (HEAD_DIM // 2)
        cos_scr[...] = jnp.cos(ang)
        sin_scr[...] = jnp.where(first_half, -jnp.sin(ang), jnp.sin(ang))

    z = jnp.dot(h_scr[...], w_ref[...], preferred_element_type=jnp.float32)

    @pl.when(j < n_rope)
    def _():
        lane = lax.broadcasted_iota(jnp.int32, (1, LANES), 1)
        first_half = (lane % HEAD_DIM) < (HEAD_DIM // 2)
        cos = cos_scr[...]
        sin = sin_scr[...]
        for cc in range(tn // LANES):
            zc = z[:, cc * LANES:(cc + 1) * LANES]
            partner = jnp.where(first_half, pltpu.roll(zc, LANES - HEAD_DIM // 2, 1),
                                pltpu.roll(zc, HEAD_DIM // 2, 1))
            z_scr[cc] = zc * cos + partner * sin

    @pl.when(j >= n_rope)
    def _():
        for cc in range(tn // LANES):
            z_scr[cc] = z[:, cc * LANES:(cc + 1) * LANES]

    for cc in range(tn // LANES):
        z_ref[:, cc * LANES:(cc + 1) * LANES] = z_scr[cc].astype(z_ref.dtype)

    @pl.when(j < (3 * ATTN_WIDTH) // tn)
    def _():
        for dil, ref in zip(dilations, dil_refs):
            for r in range(dil):
                for cc in range(tn // LANES):
                    ref[r, :, cc * LANES:(cc + 1) * LANES] = (
                        z_scr[cc, pl.ds(r, tm // dil, stride=dil), :].astype(ref.dtype))


def _inproj(x2, pos, sc1, sh1, w_in_bf, dilations):
    s, d = x2.shape
    n = w_in_bf.shape[1]
    tm = min(1024, s)
    tn = 512
    n_rope = (2 * ATTN_WIDTH) // tn
    n_qkv = (3 * ATTN_WIDTH) // tn
    bf = jnp.bfloat16
    dil_specs = [pl.BlockSpec((dil, tm // dil, tn), lambda i, j: (0, i, jnp.minimum(j, n_qkv - 1)))
                 for dil in dilations]
    dil_shapes = [jax.ShapeDtypeStruct((dil, s // dil, 3 * ATTN_WIDTH), bf) for dil in dilations]
    return pl.pallas_call(
        functools.partial(_inproj_kernel, n_rope=n_rope, dilations=tuple(dilations)),
        grid=(s // tm, n // tn),
        in_specs=[pl.BlockSpec((tm, d), lambda i, j: (i, 0)),
                  pl.BlockSpec((tm, 1), lambda i, j: (i, 0)),
                  pl.BlockSpec((1, d), lambda i, j: (0, 0)),
                  pl.BlockSpec((1, d), lambda i, j: (0, 0)),
                  pl.BlockSpec((d, tn), lambda i, j: (0, j))],
        out_specs=[pl.BlockSpec((tm, tn), lambda i, j: (i, j))] + dil_specs,
        out_shape=[jax.ShapeDtypeStruct((s, n), bf)] + dil_shapes,
        scratch_shapes=[pltpu.VMEM((tm, d), bf),
                        pltpu.VMEM((tm, LANES), jnp.float32),
                        pltpu.VMEM((tm, LANES), jnp.float32),
                        pltpu.VMEM((tn // LANES, tm, LANES), jnp.float32)],
        compiler_params=_params(("arbitrary", "arbitrary")),
        name="inproj",
    )(x2, pos, sc1, sh1, w_in_bf)


def _attn_kernel(q_ref, kc_ref, vc_ref, kp_ref, vp_ref, o_ref, lse_ref, kwin, vt_win, *, nblk):
    i = pl.program_id(1)
    blk = ATTN_BLOCK
    kwin[0:blk, :] = kp_ref[...]
    kwin[blk:, :] = kc_ref[...]
    vt_win[:, 0:blk] = vp_ref[...].T
    vt_win[:, blk:] = vc_ref[...].T

    key = lax.broadcasted_iota(jnp.int32, (2 * blk, blk), 0)
    qry = lax.broadcasted_iota(jnp.int32, (2 * blk, blk), 1)
    lane = lax.broadcasted_iota(jnp.int32, (blk, LANES), 1)
    row = lax.broadcasted_iota(jnp.int32, (blk, LANES), 0)
    nt = (((1,), (1,)), ((), ()))
    scale = HEAD_DIM ** -0.5

    def body(j, carry):
        r0 = pl.multiple_of(j * blk, blk)
        first_key = jnp.where((i * nblk + j) > 0, 0, blk)
        mask = jnp.logical_and(key >= jnp.maximum(qry, first_key), key <= qry + blk)
        lse_rows = []
        for hp in range(ATTN_HEADS // 2):
            cs = slice(hp * LANES, (hp + 1) * LANES)
            q2 = q_ref[pl.ds(r0, blk), cs]
            k2 = kwin[pl.ds(r0, 2 * blk), cs]
            vt2 = vt_win[cs, pl.ds(r0, 2 * blk)]
            halves = []
            for hd in range(2):
                in_head = (lane < HEAD_DIM) if hd == 0 else (lane >= HEAD_DIM)
                qm = jnp.where(in_head, q2, jnp.zeros_like(q2)) * jnp.asarray(scale, q2.dtype)
                st = lax.dot_general(k2, qm, nt, preferred_element_type=jnp.float32)
                st = jnp.where(mask, st, NEG_INF)
                m = jnp.max(st, axis=0, keepdims=True)
                p = jnp.exp(st - m)
                l = jnp.sum(p, axis=0, keepdims=True)
                ot = jnp.dot(vt2, p.astype(vt2.dtype), preferred_element_type=jnp.float32)
                halves.append(ot / l)
                lse_rows.append(m + jnp.log(l))
            ot2 = jnp.where(row < HEAD_DIM, halves[0], halves[1])
            o_ref[pl.ds(r0, blk), cs] = ot2.T.astype(o_ref.dtype)
        lse_mat = jnp.zeros((LANES, blk), jnp.float32)
        for hidx, lse_h in enumerate(lse_rows):
            lse_mat = jnp.where(row == hidx, lse_h, lse_mat)
        lse_ref[pl.ds(r0, blk), :] = lse_mat.T
        return carry

    lax.fori_loop(0, nblk, body, 0)


def _attn_pattern(zd, dilation):
    _, length, zw = zd.shape
    aw = ATTN_WIDTH
    rows = min(1024, length)
    nblk = rows // ATTN_BLOCK
    cur = lambda off: pl.BlockSpec((None, rows, aw), lambda r, i: (r, i, off))
    prev = lambda off: pl.BlockSpec((None, ATTN_BLOCK, aw), lambda r, i: (r, jnp.maximum(i * nblk - 1, 0), off))
    return pl.pallas_call(
        functools.partial(_attn_kernel, nblk=nblk),
        grid=(dilation, length // rows),
        in_specs=[cur(0), cur(1), cur(2), prev(1), prev(2)],
        out_specs=[pl.BlockSpec((None, rows, aw), lambda r, i: (r, i, 0)),
                   pl.BlockSpec((None, rows, LANES), lambda r, i: (r, i, 0))],
        out_shape=[jax.ShapeDtypeStruct((dilation, length, aw), jnp.bfloat16),
                   jax.ShapeDtypeStruct((dilation, length, LANES), jnp.float32)],
        scratch_shapes=[pltpu.VMEM((rows + ATTN_BLOCK, aw), jnp.bfloat16),
                        pltpu.VMEM((aw, rows + ATTN_BLOCK), jnp.bfloat16)],
        compiler_params=_params(("arbitrary", "arbitrary")),
        name=f"attn_d{dilation}",
    )(zd, zd, zd, zd, zd)


def _attention(z, z_dil):
    outs = []
    for (window, dilation), zd in zip(DILATED_PATTERNS, z_dil):
        assert window // dilation == ATTN_BLOCK
        outs.append(_attn_pattern(zd, dilation))
    return outs


def _s5_param_kernel(lr_r, li_r, ld_r, bre_ref, bim_ref, cre_ref, cim_ref,
                     krev_ref, pblk_ref, qblk_ref, at_ref):
    t_chunk = S5_CHUNK
    nst = STATE_PER_BLOCK
    lr, li, ld = lr_r[...], li_r[...], ld_r[...]
    dt = jnp.exp(ld)

    def powers(tau):
        mag = jnp.exp(lr * dt * tau)
        return mag * jnp.cos(li * dt * tau), mag * jnp.sin(li * dt * tau)

    ar, ai = powers(1.0)
    den = lr * lr + li * li
    cr = ((ar - 1.0) * lr + ai * li) / den
    ci = (ai * lr - (ar - 1.0) * li) / den
    bre, bim = bre_ref[...], bim_ref[...]
    bbr = cr * bre - ci * bim
    bbi = cr * bim + ci * bre
    cre, cim = cre_ref[...], cim_ref[...]
    hi = lax.Precision.HIGHEST
    nt = (((1,), (1,)), ((), ()))
    for tau in range(t_chunk):
        er, ei = powers(float(tau))
        pr = er * bbr - ei * bbi
        pi = er * bbi + ei * bbr
        blk = t_chunk - 1 - tau
        rows = slice(blk * LANES, (blk + 1) * LANES)
        pblk_ref[rows, 0:nst] = pr.astype(pblk_ref.dtype)
        pblk_ref[rows, nst:2 * nst] = pi.astype(pblk_ref.dtype)
        k_tau = (lax.dot_general(pr, cre, nt, preferred_element_type=jnp.float32, precision=hi)
                 - lax.dot_general(pi, cim, nt, preferred_element_type=jnp.float32, precision=hi))
        krev_ref[rows, :] = k_tau.astype(krev_ref.dtype)
        er1, ei1 = powers(float(tau + 1))
        qblk_ref[tau, :, 0:nst] = (cre * er1 - cim * ei1).astype(qblk_ref.dtype)
        qblk_ref[tau, :, nst:2 * nst] = (-(cre * ei1 + cim * er1)).astype(qblk_ref.dtype)
    er, ei = powers(float(t_chunk))
    at_ref[:, 0:nst] = er
    at_ref[:, nst:2 * nst] = ei


def _s5_params(lam_re, lam_im, log_dt, b_re, b_im, c_re, c_im):
    g, n = lam_re.shape
    p = b_re.shape[-1]
    gb = GROUPS_PER_BLOCK
    nlb = g // gb
    nst = gb * n
    f32 = jnp.float32
    eye = jnp.eye(gb, dtype=f32)

    def row(a):
        return a.astype(f32).reshape(nlb, 1, nst)

    ldt = jnp.broadcast_to(log_dt.astype(f32)[:, None], (g, n))

    def b_blockdiag(b):
        b4 = b.astype(f32).reshape(nlb, gb, n, p)
        return jnp.einsum("lhnq,gh->lgqhn", b4, eye).reshape(nlb, gb * p, nst)

    def c_blockdiag(c):
        c4 = c.astype(f32).reshape(nlb, gb, p, n)
        return jnp.einsum("lhpn,gh->lgphn", c4, eye).reshape(nlb, gb * p, nst)

    tc = S5_CHUNK
    vec_r = pl.BlockSpec((None, 1, nst), lambda l: (l, 0, 0))
    mat = pl.BlockSpec((None, LANES, nst), lambda l: (l, 0, 0))
    return pl.pallas_call(
        _s5_param_kernel,
        grid=(nlb,),
        in_specs=[vec_r, vec_r, vec_r, mat, mat, mat, mat],
        out_specs=[pl.BlockSpec((None, tc * LANES, LANES), lambda l: (l, 0, 0)),
                   pl.BlockSpec((None, tc * LANES, 2 * nst), lambda l: (l, 0, 0)),
                   pl.BlockSpec((None, tc, LANES, 2 * nst), lambda l: (l, 0, 0, 0)),
                   pl.BlockSpec((None, 1, 2 * nst), lambda l: (l, 0, 0))],
        out_shape=[jax.ShapeDtypeStruct((nlb, tc * LANES, LANES), jnp.bfloat16),
                   jax.ShapeDtypeStruct((nlb, tc * LANES, 2 * nst), jnp.bfloat16),
                   jax.ShapeDtypeStruct((nlb, tc, LANES, 2 * nst), jnp.bfloat16),
                   jax.ShapeDtypeStruct((nlb, 1, 2 * nst), f32)],
        compiler_params=_params(("arbitrary",)),
        name="s5_params",
    )(row(lam_re), row(lam_im), row(ldt),
      b_blockdiag(b_re), b_blockdiag(b_im), c_blockdiag(c_re), c_blockdiag(c_im))


def _load_ucat(u_ref, uf_scr, ucat_scr, nc):
    uf_scr[...] = u_ref[...].astype(jnp.float32)
    for s in range(S5_CHUNK):
        ucat_scr[:, s * LANES:(s + 1) * LANES] = (
            uf_scr[pl.ds(s, nc, stride=S5_CHUNK), :].astype(ucat_scr.dtype))


def _s5_state_in_kernel(u_ref, pblk_ref, b_ref, uf_scr, ucat_scr, *, nc):
    _load_ucat(u_ref, uf_scr, ucat_scr, nc)
    b_ref[...] = jnp.dot(ucat_scr[...], pblk_ref[...], preferred_element_type=jnp.float32)


def _s5_scan_kernel(b_ref, at_ref, h_ref, *, nc, nlb):
    nst = STATE_PER_BLOCK
    a_re = at_ref[:, 0:nst]
    a_im = at_ref[:, nst:2 * nst]

    def body(c, carry):
        hr, hi = carry
        h_ref[c, :, 0:nst] = hr
        h_ref[c, :, nst:2 * nst] = hi
        bc = b_ref[c]
        return (a_re * hr - a_im * hi + bc[:, 0:nst], a_re * hi + a_im * hr + bc[:, nst:2 * nst])

    zero = jnp.zeros((nlb, nst), jnp.float32)
    lax.fori_loop(0, nc, body, (zero, zero))


def _s5_out_kernel(u_ref, h_ref, krev_ref, qblk_ref, dsk_ref, y_ref, uf_scr, ucat_scr, y_scr, *, nc):
    _load_ucat(u_ref, uf_scr, ucat_scr, nc)
    hb = h_ref[...].astype(jnp.bfloat16)
    dsk = dsk_ref[...]
    tc = S5_CHUNK
    for t in range(tc):
        kd = (t + 1) * LANES
        y = jnp.dot(ucat_scr[:, 0:kd], krev_ref[(tc - 1 - t) * LANES:, :],
                    preferred_element_type=jnp.float32)
        y = y + lax.dot_general(hb, qblk_ref[t], (((1,), (1,)), ((), ())), preferred_element_type=jnp.float32)
        y = y + dsk * uf_scr[pl.ds(t, nc, stride=tc), :]
        y = 0.5 * y * (1.0 + lax.erf(y * (2.0 ** -0.5)))
        y_scr[pl.ds(t, nc, stride=tc), :] = y
    y_ref[...] = y_scr[...].astype(y_ref.dtype)


def _s5(z, lam_re, lam_im, log_dt, b_re, b_im, c_re, c_im, d_skip):
    s = z.shape[0]
    nlb = lam_re.shape[0] // GROUPS_PER_BLOCK
    nc = s // S5_CHUNK
    nst2 = 2 * STATE_PER_BLOCK
    u_col0 = (3 * ATTN_WIDTH) // LANES
    krev, pblk, qblk, at = _s5_params(lam_re, lam_im, log_dt, b_re, b_im, c_re, c_im)
    u_spec = pl.BlockSpec((s, LANES), lambda l: (0, u_col0 + l))
    b2 = pl.pallas_call(
        functools.partial(_s5_state_in_kernel, nc=nc),
        grid=(nlb,),
        in_specs=[u_spec, pl.BlockSpec((None, S5_CHUNK * LANES, nst2), lambda l: (l, 0, 0))],
        out_specs=pl.BlockSpec((nc, nst2), lambda l: (0, l)),
        out_shape=jax.ShapeDtypeStruct((nc, nlb * nst2), jnp.float32),
        scratch_shapes=[pltpu.VMEM((s, LANES), jnp.float32),
                        pltpu.VMEM((nc, S5_CHUNK * LANES), jnp.bfloat16)],
        compiler_params=_params(("arbitrary",)),
        name="s5_state_in",
    )(z, pblk)
    h3 = pl.pallas_call(
        functools.partial(_s5_scan_kernel, nc=nc, nlb=nlb),
        out_shape=jax.ShapeDtypeStruct((nc, nlb, nst2), jnp.float32),
        compiler_params=pltpu.CompilerParams(vmem_limit_bytes=VMEM_LIMIT_BYTES),
        name="s5_scan",
    )(b2.reshape(nc, nlb, nst2), at.reshape(nlb, nst2))
    return pl.pallas_call(
        functools.partial(_s5_out_kernel, nc=nc),
        grid=(nlb,),
        in_specs=[u_spec,
                  pl.BlockSpec((nc, nst2), lambda l: (0, l)),
                  pl.BlockSpec((None, S5_CHUNK * LANES, LANES), lambda l: (l, 0, 0)),
                  pl.BlockSpec((None, S5_CHUNK, LANES, nst2), lambda l: (l, 0, 0, 0)),
                  pl.BlockSpec((1, LANES), lambda l: (0, l))],
        out_specs=pl.BlockSpec((s, LANES), lambda l: (0, l)),
        out_shape=jax.ShapeDtypeStruct((s, nlb * LANES), jnp.bfloat16),
        scratch_shapes=[pltpu.VMEM((s, LANES), jnp.float32),
                        pltpu.VMEM((nc, S5_CHUNK * LANES), jnp.bfloat16),
                        pltpu.VMEM((s, LANES), jnp.float32)],
        compiler_params=_params(("arbitrary",)),
        name="s5_out",
    )(z, h3.reshape(nc, nlb * nst2), krev, qblk, d_skip.astype(jnp.float32).reshape(1, -1))


def _mixout_kernel(*refs, alpha, n_pat):
    o_refs = refs[:n_pat]
    lse_refs = refs[n_pat:2 * n_pat]
    (yg_ref, x_ref, wglu_ref, bglu_ref, ga_ref, gs_ref, wout_ref,
     gt_ref, lg_ref, lb_ref, sc_ref, sh_ref, x1_ref, h2_ref, o_scr, lse_scr) = refs[2 * n_pat:]
    planes, tm, _ = o_scr.shape
    aw = planes * LANES

    def token_major(ref, scr):
        dil = ref.shape[0]
        if dil == 1:
            return ref[0].astype(jnp.float32)
        n_planes = ref.shape[2] // LANES
        for r in range(dil):
            for cc in range(n_planes):
                scr[cc, pl.ds(r, tm // dil, stride=dil), :] = (
                    ref[r, :, cc * LANES:(cc + 1) * LANES].astype(jnp.float32))
        return jnp.concatenate([scr[cc] for cc in range(n_planes)], axis=1)

    lses = [token_major(ref, lse_scr) for ref in lse_refs]
    top = functools.reduce(jnp.maximum, lses)
    ws = [jnp.exp(v - top) for v in lses]
    inv = 1.0 / functools.reduce(lambda u, v: u + v, ws)
    head_of_lane = lax.broadcasted_iota(jnp.int32, (LANES, aw), 1) // HEAD_DIM
    spread = (lax.broadcasted_iota(jnp.int32, (LANES, aw), 0) == head_of_lane).astype(jnp.bfloat16)
    a = jnp.zeros((tm, aw), jnp.float32)
    for w, o_ref in zip(ws, o_refs):
        w = w * inv
        w_hi = w.astype(jnp.bfloat16)
        w_lo = (w - w_hi.astype(jnp.float32)).astype(jnp.bfloat16)
        wide = (jnp.dot(w_hi, spread, preferred_element_type=jnp.float32)
                + jnp.dot(w_lo, spread, preferred_element_type=jnp.float32))
        a = a + wide * token_major(o_ref, o_scr)
    ra = a * lax.rsqrt(jnp.mean(a * a, axis=-1, keepdims=True) + LN_EPS) * ga_ref[...]
    yg = yg_ref[...]
    y = yg.astype(jnp.float32)
    gate = jax.nn.sigmoid(jnp.dot(yg, wglu_ref[...], preferred_element_type=jnp.float32) + bglu_ref[...])
    y = y * gate
    ry = y * lax.rsqrt(jnp.mean(y * y, axis=-1, keepdims=True) + LN_EPS) * gs_ref[...]
    mix = (jnp.dot(ra.astype(jnp.bfloat16), wout_ref[0:aw, :], preferred_element_type=jnp.float32)
           + jnp.dot(ry.astype(jnp.bfloat16), wout_ref[aw:, :], preferred_element_type=jnp.float32))
    t = alpha * x_ref[...] + (1.0 + gt_ref[...]) * mix
    x1 = _ln_rows(t) * lg_ref[...] + lb_ref[...]
    x1_ref[...] = x1
    h2_ref[...] = (_ln_rows(x1) * (1.0 + sc_ref[...]) + sh_ref[...]).astype(h2_ref.dtype)


def _mixout(attn_parts, yg, x2, w_glu_bf, b_glu, g_attn, g_ssm, w_out_bf, gt1, ln_g, ln_b, sc2, sh2, alpha):
    s, d = x2.shape
    aw = attn_parts[0][0].shape[-1]
    sw = yg.shape[1]
    n_pat = len(attn_parts)
    tm = min(512, s)
    row = lambda i: (i, 0)
    fix = lambda i: (0, 0)
    vec = lambda n: pl.BlockSpec((1, n), fix)
    res = lambda a: pl.BlockSpec((a.shape[0], tm // a.shape[0], a.shape[2]), lambda i: (0, i, 0))
    return pl.pallas_call(
        functools.partial(_mixout_kernel, alpha=alpha, n_pat=n_pat),
        grid=(s // tm,),
        in_specs=([res(o) for o, _ in attn_parts] + [res(v) for _, v in attn_parts]
                  + [pl.BlockSpec((tm, sw), row), pl.BlockSpec((tm, d), row),
                     pl.BlockSpec((sw, sw), fix), vec(sw), vec(aw), vec(sw),
                     pl.BlockSpec((aw + sw, d), fix), vec(d), vec(d), vec(d), vec(d), vec(d)]),
        out_specs=[pl.BlockSpec((tm, d), row), pl.BlockSpec((tm, d), row)],
        out_shape=[jax.ShapeDtypeStruct((s, d), jnp.float32), jax.ShapeDtypeStruct((s, d), jnp.bfloat16)],
        scratch_shapes=[pltpu.VMEM((aw // LANES, tm, LANES), jnp.float32),
                        pltpu.VMEM((1, tm, LANES), jnp.float32)],
        compiler_params=_params(("arbitrary",)),
        name="mixout",
    )(*[o for o, _ in attn_parts], *[l for _, l in attn_parts], yg, x2, w_glu_bf, b_glu.reshape(1, -1),
      g_attn.reshape(1, -1), g_ssm.reshape(1, -1), w_out_bf, gt1, ln_g.reshape(1, -1), ln_b.reshape(1, -1),
      sc2, sh2)


def _top_values(sc, k):
    vals = []
    for _ in range(k):
        m = jnp.max(sc, axis=0, keepdims=True)
        vals.append(m)
        sc = jnp.where(sc == m, BIG_NEG, sc)
    return vals


def _peerq_kernel(h2_ref, wqp_ref, keys_ref, a0_ref, n1_ref, e0_ref, e1_ref):
    k = PEER_TOPK
    qp = jnp.dot(h2_ref[...], wqp_ref[...], preferred_element_type=jnp.float32).astype(jnp.bfloat16)
    nt = (((1,), (1,)), ((), ()))
    for h in range(PEER_HEADS):
        st = []
        for i in range(2):
            c0 = (2 * h + i) * PEER_SUB_DIM
            st.append(lax.dot_general(keys_ref[i], qp[:, c0:c0 + PEER_SUB_DIM], nt,
                                      preferred_element_type=jnp.float32))
        s0, s1 = st
        top0 = _top_values(s0, k + 1)
        top1 = _top_values(s1, k + 1)
        cands = [top0[i] + top1[j] for i in range(k + 1) for j in range(k + 1) if (i + 1) * (j + 1) <= k + 1]
        pad = (-len(cands)) % SUBLANES
        cmat = jnp.concatenate(cands + [jnp.full_like(cands[0], BIG_NEG)] * pad, axis=0)
        best = _top_values(cmat, k + 1)
        thr = 0.5 * (best[k - 1] + best[k])
        m0, m1 = top0[0], top1[0]
        zsum = jnp.sum(jnp.where(cmat >= thr, jnp.exp(cmat - (m0 + m1)), 0.0), axis=0, keepdims=True)
        a0_ref[h] = s0 - thr
        n1_ref[h] = -s1
        e0_ref[h] = jnp.exp(s0 - m0) / zsum
        e1_ref[h] = jnp.exp(s1 - m1)


def _peerq(h2, w_qp_bf, keys_bf):
    s, d = h2.shape
    tm = min(256, s)
    nk = keys_bf.shape[1]
    stat = pl.BlockSpec((PEER_HEADS, nk, tm), lambda i: (0, 0, i))
    shp = jax.ShapeDtypeStruct((PEER_HEADS, nk, s), jnp.float32)
    return pl.pallas_call(
        _peerq_kernel,
        grid=(s // tm,),
        in_specs=[pl.BlockSpec((tm, d), lambda i: (i, 0)),
                  pl.BlockSpec(w_qp_bf.shape, lambda i: (0, 0)),
                  pl.BlockSpec(keys_bf.shape, lambda i: (0, 0, 0))],
        out_specs=[stat, stat, stat, stat],
        out_shape=[shp, shp, shp, shp],
        compiler_params=_params(("arbitrary",)),
        name="peerq",
    )(h2, w_qp_bf, keys_bf)


def _peer_kernel(ua_ref, ub_ref, vt_ref, h2t_ref, a0_ref, e0_ref, n1_ref, e1_ref, o_ref,
                 a_s0, a_s1, w_s0, w_s1, *, n_pairs, lane_chunk):
    m = pl.program_id(1)
    nk = PEER_N_KEYS
    te, tm = a_s0.shape
    rows_per_tile = te // nk

    def stage_a(u_ref, dst):
        dst[...] = jnp.dot(u_ref[...], h2t_ref[...], preferred_element_type=jnp.float32)

    def stage_b(src, dst, row_off, ls):
        for rr in range(rows_per_tile):
            rows = slice(rr * nk, (rr + 1) * nk)
            i0 = row_off + rr
            a = src[rows, ls]
            act = 0.5 * a * (1.0 + lax.erf(a * (2.0 ** -0.5)))
            gate = jnp.zeros_like(a)
            for h in range(PEER_HEADS):
                a0 = a0_ref[h, i0:i0 + 1, ls]
                e0 = e0_ref[h, i0:i0 + 1, ls]
                gate = gate + jnp.where(a0 >= n1_ref[h, :, ls], e0 * e1_ref[h, :, ls], 0.0)
            dst[rows, ls] = (gate * act).astype(dst.dtype)

    def tile(src, dst, half, row_off):
        for c in range(tm // lane_chunk):
            ls = slice(c * lane_chunk, (c + 1) * lane_chunk)
            stage_b(src, dst, row_off, ls)
            o_ref[:, ls] += jnp.dot(vt_ref[:, half * te:(half + 1) * te], dst[:, ls],
                                    preferred_element_type=jnp.float32)

    @pl.when(m == 0)
    def _():
        o_ref[...] = jnp.zeros_like(o_ref)
        stage_a(ub_ref, a_s0)

    @pl.when(jnp.logical_and(m > 0, m < n_pairs))
    def _():
        stage_a(ua_ref, a_s1)
        tile(a_s0, w_s0, 0, 0)
        stage_a(ub_ref, a_s0)
        tile(a_s1, w_s1, 1, rows_per_tile)

    @pl.when(m == n_pairs)
    def _():
        stage_a(ua_ref, a_s1)
        tile(a_s0, w_s0, 0, 0)
        tile(a_s1, w_s1, 1, rows_per_tile)


def _peer(h2t, u_bf, vt_bf, a0, n1, e0, e1):
    d, s = h2t.shape
    ne = u_bf.shape[0]
    nk = PEER_N_KEYS
    tm = min(512, s)
    te = (SUBLANES // 2) * nk
    n_tiles = ne // te
    n_pairs = n_tiles // 2
    rows = pl.BlockSpec((PEER_HEADS, SUBLANES, tm), lambda i, m: (0, jnp.maximum(m - 1, 0), i))
    full_stat = pl.BlockSpec((PEER_HEADS, nk, tm), lambda i, m: (0, 0, i))
    return pl.pallas_call(
        functools.partial(_peer_kernel, n_pairs=n_pairs, lane_chunk=min(256, tm)),
        grid=(s // tm, n_pairs + 1),
        in_specs=[pl.BlockSpec((te, d), lambda i, m: (jnp.maximum(2 * m - 1, 0), 0)),
                  pl.BlockSpec((te, d), lambda i, m: (jnp.minimum(2 * m, n_tiles - 1), 0)),
                  pl.BlockSpec((d, 2 * te), lambda i, m: (0, jnp.maximum(m - 1, 0))),
                  pl.BlockSpec((d, tm), lambda i, m: (0, i)),
                  rows, rows, full_stat, full_stat],
        out_specs=pl.BlockSpec((d, tm), lambda i, m: (0, i)),
        out_shape=jax.ShapeDtypeStruct((d, s), jnp.float32),
        scratch_shapes=[pltpu.VMEM((te, tm), jnp.float32), pltpu.VMEM((te, tm), jnp.float32),
                        pltpu.VMEM((te, tm), jnp.bfloat16), pltpu.VMEM((te, tm), jnp.bfloat16)],
        compiler_params=_params(("arbitrary", "arbitrary")),
        name="peer",
    )(u_bf, u_bf, vt_bf, h2t, a0, e0, n1, e1)


def _final_kernel(ft_ref, x1_ref, gt_ref, lg_ref, lb_ref, o_ref, *, alpha):
    ffn = ft_ref[...].T
    t = alpha * x1_ref[...] + (1.0 + gt_ref[...]) * ffn
    o_ref[...] = _ln_rows(t) * lg_ref[...] + lb_ref[...]


def _final(ffn_t, x1, gt2, ln_g, ln_b, alpha):
    s, d = x1.shape
    tm = min(512, s)
    vec = pl.BlockSpec((1, d), lambda i: (0, 0))
    return pl.pallas_call(
        functools.partial(_final_kernel, alpha=alpha),
        grid=(s // tm,),
        in_specs=[pl.BlockSpec((d, tm), lambda i: (0, i)), pl.BlockSpec((tm, d), lambda i: (i, 0)),
                  vec, vec, vec],
        out_specs=pl.BlockSpec((tm, d), lambda i: (i, 0)),
        out_shape=jax.ShapeDtypeStruct((s, d), jnp.float32),
        compiler_params=_params(("arbitrary",)),
        name="final",
    )(ffn_t, x1, gt2, ln_g.reshape(1, -1), ln_b.reshape(1, -1))


def kernel(x, c, positions, w_ada, b_ada, w_in, lam_re, lam_im, log_dt, ssm_b_re, ssm_b_im, ssm_c_re,
           ssm_c_im, ssm_d, w_glu, b_glu, g_attn, g_ssm, w_out, ln1_g, ln1_b, w_qp, sub_keys, expert_u,
           expert_v, ln2_g, ln2_b):
    b, s, d = x.shape
    assert b == 1, "one sequence per call"
    depth = w_ada.shape[0]
    alpha = (2.0 * depth) ** 0.25
    bf = jnp.bfloat16
    x2 = x.reshape(s, d)
    pos = positions.reshape(s, 1)
    dilations = [dil for _, dil in DILATED_PATTERNS if dil > 1]
    for l in range(depth):
        mod = _ada(c, w_ada[l], b_ada[l])
        sh1, sc1, gt1, sh2, sc2, gt2 = [mod[:, k * d:(k + 1) * d] for k in range(ADA_CHUNKS)]
        z, *z_res = _inproj(x2, pos, sc1, sh1, w_in[l].astype(bf), dilations)
        z_dil = [z.reshape(1, s, -1) if dil == 1 else z_res[dilations.index(dil)]
                 for _, dil in DILATED_PATTERNS]
        attn = _attention(z, z_dil)
        yg = _s5(z, lam_re[l], lam_im[l], log_dt[l], ssm_b_re[l], ssm_b_im[l],
                 ssm_c_re[l], ssm_c_im[l], ssm_d[l])
        x1, h2 = _mixout(attn, yg, x2, w_glu[l].astype(bf), b_glu[l], g_attn[l], g_ssm[l],
                         w_out[l].astype(bf), gt1, ln1_g[l], ln1_b[l], sc2, sh2, alpha)
        a0, n1, e0, e1 = _peerq(h2, w_qp[l].astype(bf), sub_keys[l].astype(bf))
        ffn_t = _peer(h2.T, expert_u[l].astype(bf), expert_v[l].T.astype(bf), a0, n1, e0, e1)
        x2 = _final(ffn_t, x1, gt2, ln2_g[l], ln2_b[l], alpha)
    return x2.reshape(b, s, d)
```

```python
import functools
import math

import jax
import jax.numpy as jnp
from jax import lax
from jax.experimental import pallas as pl
from jax.experimental.pallas import tpu as pltpu

ATTN_HEADS = 16
HEAD_DIM = 64
ATTN_WIDTH = ATTN_HEADS * HEAD_DIM
SSM_GROUP = 16
SSM_STATE = 64
DILATED_PATTERNS = ((128, 1), (512, 4), (2048, 16))
ATTN_BLOCK = 128
ROPE_THETA = 10000.0
NEG_INF = -1e30
PEER_HEADS = 8
PEER_SUB_DIM = 128
PEER_N_KEYS = 128
PEER_TOPK = 16
LN_EPS = 1e-5
ADA_CHUNKS = 6

LANES = 128
SUBLANES = 8
VMEM_LIMIT_BYTES = 56 * 1024 * 1024

S5_CHUNK = 16
GROUPS_PER_BLOCK = LANES // SSM_GROUP
STATE_PER_BLOCK = GROUPS_PER_BLOCK * SSM_STATE
BIG_NEG = -3.0e38


def _params(sem):
    return pltpu.CompilerParams(dimension_semantics=sem, vmem_limit_bytes=VMEM_LIMIT_BYTES)


def _ln_rows(t):
    mu = jnp.mean(t, axis=-1, keepdims=True)
    d = t - mu
    var = jnp.mean(d * d, axis=-1, keepdims=True)
    return d * lax.rsqrt(var + LN_EPS)


def _ada_kernel(c_ref, w_ref, b_ref, o_ref):
    c = c_ref[...]
    sc = c * jax.nn.sigmoid(c)
    o_ref[...] = jnp.dot(sc, w_ref[...], preferred_element_type=jnp.float32,
                         precision=lax.Precision.HIGHEST) + b_ref[...]


def _ada(c, w_ada, b_ada):
    d = c.shape[-1]
    n = w_ada.shape[-1]
    tn = n // 8
    c8 = jnp.broadcast_to(c.reshape(1, d), (SUBLANES, d))
    out = pl.pallas_call(
        _ada_kernel,
        grid=(n // tn,),
        in_specs=[pl.BlockSpec((SUBLANES, d), lambda j: (0, 0)),
                  pl.BlockSpec((d, tn), lambda j: (0, j)),
                  pl.BlockSpec((1, tn), lambda j: (0, j))],
        out_specs=pl.BlockSpec((SUBLANES, tn), lambda j: (0, j)),
        out_shape=jax.ShapeDtypeStruct((SUBLANES, n), jnp.float32),
        compiler_params=_params(("arbitrary",)),
        name="ada",
    )(c8, w_ada, b_ada.reshape(1, n))
    return out[0:1]


def _inproj_kernel(x_ref, pos_ref, sc_ref, sh_ref, w_ref, z_ref, *rest, n_rope, dilations):
    dil_refs = rest[:len(dilations)]
    h_scr, cos_scr, sin_scr, z_scr = rest[len(dilations):]
    j = pl.program_id(1)
    tm, tn = z_ref.shape

    @pl.when(j == 0)
    def _():
        h = _ln_rows(x_ref[...]) * (1.0 + sc_ref[...]) + sh_ref[...]
        h_scr[...] = h.astype(jnp.bfloat16)
        lane = lax.broadcasted_iota(jnp.int32, (1, LANES), 1)
        fi = ((lane % HEAD_DIM) % (HEAD_DIM // 2)).astype(jnp.float32)
        inv_freq = jnp.exp(fi * (-math.log(ROPE_THETA) / (HEAD_DIM // 2)))
        ang = pos_ref[...].astype(jnp.float32) * inv_freq
        first_half = (lane % HEAD_DIM) < (HEAD_DIM // 2)
        cos_scr[...] = jnp.cos(ang)
        sin_scr[...] = jnp.where(first_half, -jnp.sin(ang), jnp.sin(ang))

    z = jnp.dot(h_scr[...], w_ref[...], preferred_element_type=jnp.float32)

    @pl.when(j < n_rope)
    def _():
        lane = lax.broadcasted_iota(jnp.int32, (1, LANES), 1)
        first_half = (lane % HEAD_DIM) < (HEAD_DIM // 2)
        cos = cos_scr[...]
        sin = sin_scr[...]
        for cc in range(tn // LANES):
            zc = z[:, cc * LANES:(cc + 1) * LANES]
            partner = jnp.where(first_half, pltpu.roll(zc, LANES - HEAD_DIM // 2, 1),
                                pltpu.roll(zc, HEAD_DIM // 2, 1))
            z_scr[cc] = zc * cos + partner * sin

    @pl.when(j >= n_rope)
    def _():
        for cc in range(tn // LANES):
            z_scr[cc] = z[:, cc * LANES:(cc + 1) * LANES]

    for cc in range(tn // LANES):
        z_ref[:, cc * LANES:(cc + 1) * LANES] = z_scr[cc].astype(z_ref.dtype)

    @pl.when(j < (3 * ATTN_WIDTH) // tn)
    def _():
        for dil, ref in zip(dilations, dil_refs):
            for r in range(dil):
                for cc in range(tn // LANES):
                    ref[r, :, cc * LANES:(cc + 1) * LANES] = (
                        z_scr[cc, pl.ds(r, tm // dil, stride=dil), :].astype(ref.dtype))


def _inproj(x2, pos, sc1, sh1, w_in_bf, dilations):
    s, d = x2.shape
    n = w_in_bf.shape[1]
    tm = min(1024, s)
    tn = 512
    n_rope = (2 * ATTN_WIDTH) // tn
    n_qkv = (3 * ATTN_WIDTH) // tn
    bf = jnp.bfloat16
    dil_specs = [pl.BlockSpec((dil, tm // dil, tn), lambda i, j: (0, i, jnp.minimum(j, n_qkv - 1)))
                 for dil in dilations]
    dil_shapes = [jax.ShapeDtypeStruct((dil, s // dil, 3 * ATTN_WIDTH), bf) for dil in dilations]
    return pl.pallas_call(
        functools.partial(_inproj_kernel, n_rope=n_rope, dilations=tuple(dilations)),
        grid=(s // tm, n // tn),
        in_specs=[pl.BlockSpec((tm, d), lambda i, j: (i, 0)),
                  pl.BlockSpec((tm, 1), lambda i, j: (i, 0)),
                  pl.BlockSpec((1, d), lambda i, j: (0, 0)),
                  pl.BlockSpec((1, d), lambda i, j: (0, 0)),
                  pl.BlockSpec((d, tn), lambda i, j: (0, j))],
        out_specs=[pl.BlockSpec((tm, tn), lambda i, j: (i, j))] + dil_specs,
        out_shape=[jax.ShapeDtypeStruct((s, n), bf)] + dil_shapes,
        scratch_shapes=[pltpu.VMEM((tm, d), bf),
                        pltpu.VMEM((tm, LANES), jnp.float32),
                        pltpu.VMEM((tm, LANES), jnp.float32),
                        pltpu.VMEM((tn // LANES, tm, LANES), jnp.float32)],
        compiler_params=_params(("arbitrary", "arbitrary")),
        name="inproj",
    )(x2, pos, sc1, sh1, w_in_bf)


def _attn_kernel(q_ref, kc_ref, vc_ref, kp_ref, vp_ref, o_ref, lse_ref, kwin, vt_win, *, nblk):
    i = pl.program_id(1)
    blk = ATTN_BLOCK
    kwin[0:blk, :] = kp_ref[...]
    kwin[blk:, :] = kc_ref[...]
    vt_win[:, 0:blk] = vp_ref[...].T
    vt_win[:, blk:] = vc_ref[...].T

    key = lax.broadcasted_iota(jnp.int32, (2 * blk, blk), 0)
    qry = lax.broadcasted_iota(jnp.int32, (2 * blk, blk), 1)
    lane = lax.broadcasted_iota(jnp.int32, (blk, LANES), 1)
    row = lax.broadcasted_iota(jnp.int32, (blk, LANES), 0)
    nt = (((1,), (1,)), ((), ()))
    scale = HEAD_DIM ** -0.5

    def body(j, carry):
        r0 = pl.multiple_of(j * blk, blk)
        first_key = jnp.where((i * nblk + j) > 0, 0, blk)
        mask = jnp.logical_and(key >= jnp.maximum(qry, first_key), key <= qry + blk)
        lse_rows = []
        for hp in range(ATTN_HEADS // 2):
            cs = slice(hp * LANES, (hp + 1) * LANES)
            q2 = q_ref[pl.ds(r0, blk), cs]
            k2 = kwin[pl.ds(r0, 2 * blk), cs]
            vt2 = vt_win[cs, pl.ds(r0, 2 * blk)]
            halves = []
            for hd in range(2):
                in_head = (lane < HEAD_DIM) if hd == 0 else (lane >= HEAD_DIM)
                qm = jnp.where(in_head, q2, jnp.zeros_like(q2)) * jnp.asarray(scale, q2.dtype)
                st = lax.dot_general(k2, qm, nt, preferred_element_type=jnp.float32)
                st = jnp.where(mask, st, NEG_INF)
                m = jnp.max(st, axis=0, keepdims=True)
                p = jnp.exp(st - m)
                l = jnp.sum(p, axis=0, keepdims=True)
                ot = jnp.dot(vt2, p.astype(vt2.dtype), preferred_element_type=jnp.float32)
                halves.append(ot / l)
                lse_rows.append(m + jnp.log(l))
            ot2 = jnp.where(row < HEAD_DIM, halves[0], halves[1])
            o_ref[pl.ds(r0, blk), cs] = ot2.T.astype(o_ref.dtype)
        lse_mat = jnp.zeros((LANES, blk), jnp.float32)
        for hidx, lse_h in enumerate(lse_rows):
            lse_mat = jnp.where(row == hidx, lse_h, lse_mat)
        lse_ref[pl.ds(r0, blk), :] = lse_mat.T
        return carry

    lax.fori_loop(0, nblk, body, 0)


def _attn_pattern(zd, dilation):
    _, length, zw = zd.shape
    aw = ATTN_WIDTH
    rows = min(1024, length)
    nblk = rows // ATTN_BLOCK
    cur = lambda off: pl.BlockSpec((None, rows, aw), lambda r, i: (r, i, off))
    prev = lambda off: pl.BlockSpec((None, ATTN_BLOCK, aw), lambda r, i: (r, jnp.maximum(i * nblk - 1, 0), off))
    return pl.pallas_call(
        functools.partial(_attn_kernel, nblk=nblk),
        grid=(dilation, length // rows),
        in_specs=[cur(0), cur(1), cur(2), prev(1), prev(2)],
        out_specs=[pl.BlockSpec((None, rows, aw), lambda r, i: (r, i, 0)),
                   pl.BlockSpec((None, rows, LANES), lambda r, i: (r, i, 0))],
        out_shape=[jax.ShapeDtypeStruct((dilation, length, aw), jnp.bfloat16),
                   jax.ShapeDtypeStruct((dilation, length, LANES), jnp.float32)],
        scratch_shapes=[pltpu.VMEM((rows + ATTN_BLOCK, aw), jnp.bfloat16),
                        pltpu.VMEM((aw, rows + ATTN_BLOCK), jnp.bfloat16)],
        compiler_params=_params(("arbitrary", "arbitrary")),
        name=f"attn_d{dilation}",
    )(zd, zd, zd, zd, zd)


def _attention(z, z_dil):
    outs = []
    for (window, dilation), zd in zip(DILATED_PATTERNS, z_dil):
        assert window // dilation == ATTN_BLOCK
        outs.append(_attn_pattern(zd, dilation))
    return outs


def _s5_param_kernel(lr_r, li_r, ld_r, bre_ref, bim_ref, cre_ref, cim_ref,
                     krev_ref, pblk_ref, qblk_ref, at_ref):
    t_chunk = S5_CHUNK
    nst = STATE_PER_BLOCK
    lr, li, ld = lr_r[...], li_r[...], ld_r[...]
    dt = jnp.exp(ld)

    def powers(tau):
        mag = jnp.exp(lr * dt * tau)
        return mag * jnp.cos(li * dt * tau), mag * jnp.sin(li * dt * tau)

    ar, ai = powers(1.0)
    den = lr * lr + li * li
    cr = ((ar - 1.0) * lr + ai * li) / den
    ci = (ai * lr - (ar - 1.0) * li) / den
    bre, bim = bre_ref[...], bim_ref[...]
    bbr = cr * bre - ci * bim
    bbi = cr * bim + ci * bre
    cre, cim = cre_ref[...], cim_ref[...]
    hi = lax.Precision.HIGHEST
    nt = (((1,), (1,)), ((), ()))
    for tau in range(t_chunk):
        er, ei = powers(float(tau))
        pr = er * bbr - ei * bbi
        pi = er * bbi + ei * bbr
        blk = t_chunk - 1 - tau
        rows = slice(blk * LANES, (blk + 1) * LANES)
        pblk_ref[rows, 0:nst] = pr.astype(pblk_ref.dtype)
        pblk_ref[rows, nst:2 * nst] = pi.astype(pblk_ref.dtype)
        k_tau = (lax.dot_general(pr, cre, nt, preferred_element_type=jnp.float32, precision=hi)
                 - lax.dot_general(pi, cim, nt, preferred_element_type=jnp.float32, precision=hi))
        krev_ref[rows, :] = k_tau.astype(krev_ref.dtype)
        er1, ei1 = powers(float(tau + 1))
        qblk_ref[tau, :, 0:nst] = (cre * er1 - cim * ei1).astype(qblk_ref.dtype)
        qblk_ref[tau, :, nst:2 * nst] = (-(cre * ei1 + cim * er1)).astype(qblk_ref.dtype)
    er, ei = powers(float(t_chunk))
    at_ref[:, 0:nst] = er
    at_ref[:, nst:2 * nst] = ei


def _s5_params(lam_re, lam_im, log_dt, b_re, b_im, c_re, c_im):
    g, n = lam_re.shape
    p = b_re.shape[-1]
    gb = GROUPS_PER_BLOCK
    nlb = g // gb
    nst = gb * n
    f32 = jnp.float32
    eye = jnp.eye(gb, dtype=f32)

    def row(a):
        return a.astype(f32).reshape(nlb, 1, nst)

    ldt = jnp.broadcast_to(log_dt.astype(f32)[:, None], (g, n))

    def b_blockdiag(b):
        b4 = b.astype(f32).reshape(nlb, gb, n, p)
        return jnp.einsum("lhnq,gh->lgqhn", b4, eye).reshape(nlb, gb * p, nst)

    def c_blockdiag(c):
        c4 = c.astype(f32).reshape(nlb, gb, p, n)
        return jnp.einsum("lhpn,gh->lgphn", c4, eye).reshape(nlb, gb * p, nst)

    tc = S5_CHUNK
    vec_r = pl.BlockSpec((None, 1, nst), lambda l: (l, 0, 0))
    mat = pl.BlockSpec((None, LANES, nst), lambda l: (l, 0, 0))
    return pl.pallas_call(
        _s5_param_kernel,
        grid=(nlb,),
        in_specs=[vec_r, vec_r, vec_r, mat, mat, mat, mat],
        out_specs=[pl.BlockSpec((None, tc * LANES, LANES), lambda l: (l, 0, 0)),
                   pl.BlockSpec((None, tc * LANES, 2 * nst), lambda l: (l, 0, 0)),
                   pl.BlockSpec((None, tc, LANES, 2 * nst), lambda l: (l, 0, 0, 0)),
                   pl.BlockSpec((None, 1, 2 * nst), lambda l: (l, 0, 0))],
        out_shape=[jax.ShapeDtypeStruct((nlb, tc * LANES, LANES), jnp.bfloat16),
                   jax.ShapeDtypeStruct((nlb, tc * LANES, 2 * nst), jnp.bfloat16),
                   jax.ShapeDtypeStruct((nlb, tc, LANES, 2 * nst), jnp.bfloat16),
                   jax.ShapeDtypeStruct((nlb, 1, 2 * nst), f32)],
        compiler_params=_params(("arbitrary",)),
        name="s5_params",
    )(row(lam_re), row(lam_im), row(ldt),
      b_blockdiag(b_re), b_blockdiag(b_im), c_blockdiag(c_re), c_blockdiag(c_im))


def _load_ucat(u_ref, uf_scr, ucat_scr, nc):
    uf_scr[...] = u_ref[...].astype(jnp.float32)
    for s in range(S5_CHUNK):
        ucat_scr[:, s * LANES:(s + 1) * LANES] = (
            uf_scr[pl.ds(s, nc, stride=S5_CHUNK), :].astype(ucat_scr.dtype))


def _s5_state_in_kernel(u_ref, pblk_ref, b_ref, uf_scr, ucat_scr, *, nc):
    _load_ucat(u_ref, uf_scr, ucat_scr, nc)
    b_ref[...] = jnp.dot(ucat_scr[...], pblk_ref[...], preferred_element_type=jnp.float32)


def _s5_scan_kernel(b_ref, at_ref, h_ref, *, nc, nlb):
    nst = STATE_PER_BLOCK
    a_re = at_ref[:, 0:nst]
    a_im = at_ref[:, nst:2 * nst]

    def body(c, carry):
        hr, hi = carry
        h_ref[c, :, 0:nst] = hr
        h_ref[c, :, nst:2 * nst] = hi
        bc = b_ref[c]
        return (a_re * hr - a_im * hi + bc[:, 0:nst], a_re * hi + a_im * hr + bc[:, nst:2 * nst])

    zero = jnp.zeros((nlb, nst), jnp.float32)
    lax.fori_loop(0, nc, body, (zero, zero))


def _s5_out_kernel(u_ref, h_ref, krev_ref, qblk_ref, dsk_ref, y_ref, uf_scr, ucat_scr, y_scr, *, nc):
    _load_ucat(u_ref, uf_scr, ucat_scr, nc)
    hb = h_ref[...].astype(jnp.bfloat16)
    dsk = dsk_ref[...]
    tc = S5_CHUNK
    for t in range(tc):
        kd = (t + 1) * LANES
        y = jnp.dot(ucat_scr[:, 0:kd], krev_ref[(tc - 1 - t) * LANES:, :],
                    preferred_element_type=jnp.float32)
        y = y + lax.dot_general(hb, qblk_ref[t], (((1,), (1,)), ((), ())), preferred_element_type=jnp.float32)
        y = y + dsk * uf_scr[pl.ds(t, nc, stride=tc), :]
        y = 0.5 * y * (1.0 + lax.erf(y * (2.0 ** -0.5)))
        y_scr[pl.ds(t, nc, stride=tc), :] = y
    y_ref[...] = y_scr[...].astype(y_ref.dtype)


def _s5(z, lam_re, lam_im, log_dt, b_re, b_im, c_re, c_im, d_skip):
    s = z.shape[0]
    nlb = lam_re.shape[0] // GROUPS_PER_BLOCK
    nc = s // S5_CHUNK
    nst2 = 2 * STATE_PER_BLOCK
    u_col0 = (3 * ATTN_WIDTH) // LANES
    krev, pblk, qblk, at = _s5_params(lam_re, lam_im, log_dt, b_re, b_im, c_re, c_im)
    u_spec = pl.BlockSpec((s, LANES), lambda l: (0, u_col0 + l))
    b2 = pl.pallas_call(
        functools.partial(_s5_state_in_kernel, nc=nc),
        grid=(nlb,),
        in_specs=[u_spec, pl.BlockSpec((None, S5_CHUNK * LANES, nst2), lambda l: (l, 0, 0))],
        out_specs=pl.BlockSpec((nc, nst2), lambda l: (0, l)),
        out_shape=jax.ShapeDtypeStruct((nc, nlb * nst2), jnp.float32),
        scratch_shapes=[pltpu.VMEM((s, LANES), jnp.float32),
                        pltpu.VMEM((nc, S5_CHUNK * LANES), jnp.bfloat16)],
        compiler_params=_params(("arbitrary",)),
        name="s5_state_in",
    )(z, pblk)
    h3 = pl.pallas_call(
        functools.partial(_s5_scan_kernel, nc=nc, nlb=nlb),
        out_shape=jax.ShapeDtypeStruct((nc, nlb, nst2), jnp.float32),
        compiler_params=pltpu.CompilerParams(vmem_limit_bytes=VMEM_LIMIT_BYTES),
        name="s5_scan",
    )(b2.reshape(nc, nlb, nst2), at.reshape(nlb, nst2))
    return pl.pallas_call(
        functools.partial(_s5_out_kernel, nc=nc),
        grid=(nlb,),
        in_specs=[u_spec,
                  pl.BlockSpec((nc, nst2), lambda l: (0, l)),
                  pl.BlockSpec((None, S5_CHUNK * LANES, LANES), lambda l: (l, 0, 0)),
                  pl.BlockSpec((None, S5_CHUNK, LANES, nst2), lambda l: (l, 0, 0, 0)),
                  pl.BlockSpec((1, LANES), lambda l: (0, l))],
        out_specs=pl.BlockSpec((s, LANES), lambda l: (0, l)),
        out_shape=jax.ShapeDtypeStruct((s, nlb * LANES), jnp.bfloat16),
        scratch_shapes=[pltpu.VMEM((s, LANES), jnp.float32),
                        pltpu.VMEM((nc, S5_CHUNK * LANES), jnp.bfloat16),
                        pltpu.VMEM((s, LANES), jnp.float32)],
        compiler_params=_params(("arbitrary",)),
        name="s5_out",
    )(z, h3.reshape(nc, nlb * nst2), krev, qblk, d_skip.astype(jnp.float32).reshape(1, -1))


def _mixout_kernel(*refs, alpha, n_pat):
    o_refs = refs[:n_pat]
    lse_refs = refs[n_pat:2 * n_pat]
    (yg_ref, x_ref, wglu_ref, bglu_ref, ga_ref, gs_ref, wout_ref,
     gt_ref, lg_ref, lb_ref, sc_ref, sh_ref, x1_ref, h2_ref, o_scr, lse_scr) = refs[2 * n_pat:]
    planes, tm, _ = o_scr.shape
    aw = planes * LANES

    def token_major(ref, scr):
        dil = ref.shape[0]
        if dil == 1:
            return ref[0].astype(jnp.float32)
        n_planes = ref.shape[2] // LANES
        for r in range(dil):
            for cc in range(n_planes):
                scr[cc, pl.ds(r, tm // dil, stride=dil), :] = (
                    ref[r, :, cc * LANES:(cc + 1) * LANES].astype(jnp.float32))
        return jnp.concatenate([scr[cc] for cc in range(n_planes)], axis=1)

    lses = [token_major(ref, lse_scr) for ref in lse_refs]
    top = functools.reduce(jnp.maximum, lses)
    ws = [jnp.exp(v - top) for v in lses]
    inv = 1.0 / functools.reduce(lambda u, v: u + v, ws)
    head_of_lane = lax.broadcasted_iota(jnp.int32, (LANES, aw), 1) // HEAD_DIM
    spread = (lax.broadcasted_iota(jnp.int32, (LANES, aw), 0) == head_of_lane).astype(jnp.bfloat16)
    a = jnp.zeros((tm, aw), jnp.float32)
    for w, o_ref in zip(ws, o_refs):
        w = w * inv
        w_hi = w.astype(jnp.bfloat16)
        w_lo = (w - w_hi.astype(jnp.float32)).astype(jnp.bfloat16)
        wide = (jnp.dot(w_hi, spread, preferred_element_type=jnp.float32)
                + jnp.dot(w_lo, spread, preferred_element_type=jnp.float32))
        a = a + wide * token_major(o_ref, o_scr)
    ra = a * lax.rsqrt(jnp.mean(a * a, axis=-1, keepdims=True) + LN_EPS) * ga_ref[...]
    yg = yg_ref[...]
    y = yg.astype(jnp.float32)
    gate = jax.nn.sigmoid(jnp.dot(yg, wglu_ref[...], preferred_element_type=jnp.float32) + bglu_ref[...])
    y = y * gate
    ry = y * lax.rsqrt(jnp.mean(y * y, axis=-1, keepdims=True) + LN_EPS) * gs_ref[...]
    mix = (jnp.dot(ra.astype(jnp.bfloat16), wout_ref[0:aw, :], preferred_element_type=jnp.float32)
           + jnp.dot(ry.astype(jnp.bfloat16), wout_ref[aw:, :], preferred_element_type=jnp.float32))
    t = alpha * x_ref[...] + (1.0 + gt_ref[...]) * mix
    x1 = _ln_rows(t) * lg_ref[...] + lb_ref[...]
    x1_ref[...] = x1
    h2_ref[...] = (_ln_rows(x1) * (1.0 + sc_ref[...]) + sh_ref[...]).astype(h2_ref.dtype)


def _mixout(attn_parts, yg, x2, w_glu_bf, b_glu, g_attn, g_ssm, w_out_bf, gt1, ln_g, ln_b, sc2, sh2, alpha):
    s, d = x2.shape
    aw = attn_parts[0][0].shape[-1]
    sw = yg.shape[1]
    n_pat = len(attn_parts)
    tm = min(512, s)
    row = lambda i: (i, 0)
    fix = lambda i: (0, 0)
    vec = lambda n: pl.BlockSpec((1, n), fix)
    res = lambda a: pl.BlockSpec((a.shape[0], tm // a.shape[0], a.shape[2]), lambda i: (0, i, 0))
    return pl.pallas_call(
        functools.partial(_mixout_kernel, alpha=alpha, n_pat=n_pat),
        grid=(s // tm,),
        in_specs=([res(o) for o, _ in attn_parts] + [res(v) for _, v in attn_parts]
                  + [pl.BlockSpec((tm, sw), row), pl.BlockSpec((tm, d), row),
                     pl.BlockSpec((sw, sw), fix), vec(sw), vec(aw), vec(sw),
                     pl.BlockSpec((aw + sw, d), fix), vec(d), vec(d), vec(d), vec(d), vec(d)]),
        out_specs=[pl.BlockSpec((tm, d), row), pl.BlockSpec((tm, d), row)],
        out_shape=[jax.ShapeDtypeStruct((s, d), jnp.float32), jax.ShapeDtypeStruct((s, d), jnp.bfloat16)],
        scratch_shapes=[pltpu.VMEM((aw // LANES, tm, LANES), jnp.float32),
                        pltpu.VMEM((1, tm, LANES), jnp.float32)],
        compiler_params=_params(("arbitrary",)),
        name="mixout",
    )(*[o for o, _ in attn_parts], *[l for _, l in attn_parts], yg, x2, w_glu_bf, b_glu.reshape(1, -1),
      g_attn.reshape(1, -1), g_ssm.reshape(1, -1), w_out_bf, gt1, ln_g.reshape(1, -1), ln_b.reshape(1, -1),
      sc2, sh2)


def _top_values(sc, k, want_rank=False):
    vals = []
    rank = jnp.full(sc.shape, float(k), jnp.float32) if want_rank else None
    for j in range(k):
        m = jnp.max(sc, axis=0, keepdims=True)
        vals.append(m)
        hit = sc == m
        if want_rank:
            rank = jnp.where(hit, float(j), rank)
        sc = jnp.where(hit, BIG_NEG, sc)
    return vals, rank


def _peerq_kernel(h2_ref, wqp_ref, keys_ref, c0_ref, r1_ref, e0_ref, e1_ref):
    k = PEER_TOPK
    qp = jnp.dot(h2_ref[...], wqp_ref[...], preferred_element_type=jnp.float32).astype(jnp.bfloat16)
    nt = (((1,), (1,)), ((), ()))
    for h in range(PEER_HEADS):
        st = []
        for i in range(2):
            c0 = (2 * h + i) * PEER_SUB_DIM
            st.append(lax.dot_general(keys_ref[i], qp[:, c0:c0 + PEER_SUB_DIM], nt,
                                      preferred_element_type=jnp.float32))
        s0, s1 = st
        top0, _ = _top_values(s0, k + 1)
        top1, rank1 = _top_values(s1, k + 1, want_rank=True)
        cands = [top0[i] + top1[j] for i in range(k + 1) for j in range(k + 1) if (i + 1) * (j + 1) <= k + 1]
        pad = (-len(cands)) % SUBLANES
        cmat = jnp.concatenate(cands + [jnp.full_like(cands[0], BIG_NEG)] * pad, axis=0)
        best, _ = _top_values(cmat, k + 1)
        thr = 0.5 * (best[k - 1] + best[k])
        m0, m1 = top0[0], top1[0]
        zsum = jnp.sum(jnp.where(cmat >= thr, jnp.exp(cmat - (m0 + m1)), 0.0), axis=0, keepdims=True)
        count0 = jnp.zeros_like(s0)
        for j in range(k):
            count0 = count0 + jnp.where(s0 + top1[j] >= thr, 1.0, 0.0)
        c0_ref[h] = count0
        r1_ref[h] = rank1.astype(r1_ref.dtype)
        e0_ref[h] = jnp.exp(s0 - m0) / zsum
        e1_ref[h] = jnp.exp(s1 - m1).astype(e1_ref.dtype)


def _peerq(h2, w_qp_bf, keys_bf):
    s, d = h2.shape
    tm = min(256, s)
    nk = keys_bf.shape[1]
    stat = pl.BlockSpec((PEER_HEADS, nk, tm), lambda i: (0, 0, i))
    words = jax.ShapeDtypeStruct((PEER_HEADS, nk, s), jnp.float32)
    halfs = jax.ShapeDtypeStruct((PEER_HEADS, nk, s), jnp.bfloat16)
    return pl.pallas_call(
        _peerq_kernel,
        grid=(s // tm,),
        in_specs=[pl.BlockSpec((tm, d), lambda i: (i, 0)),
                  pl.BlockSpec(w_qp_bf.shape, lambda i: (0, 0)),
                  pl.BlockSpec(keys_bf.shape, lambda i: (0, 0, 0))],
        out_specs=[stat, stat, stat, stat],
        out_shape=[words, halfs, words, halfs],
        compiler_params=_params(("arbitrary",)),
        name="peerq",
    )(h2, w_qp_bf, keys_bf)


def _peer_kernel(ua_ref, ub_ref, vt_ref, h2t_ref, c0_ref, e0_ref, r1_ref, e1_ref, o_ref,
                 a_s0, a_s1, w_s0, w_s1, *, n_pairs, lane_chunk):
    m = pl.program_id(1)
    nk = PEER_N_KEYS
    te, tm = a_s0.shape
    rows_per_tile = te // nk
    packed = 2 * SUBLANES
    bf = jnp.bfloat16

    def stage_a(u_ref, dst):
        dst[...] = jnp.dot(u_ref[...], h2t_ref[...], preferred_element_type=jnp.float32)

    def row_tile(ref, h, i0, ls):
        return jnp.broadcast_to(ref[h, i0:i0 + 1, ls], (packed, lane_chunk)).astype(bf)

    def stage_b(src, dst, row_off, ls):
        for rr in range(rows_per_tile):
            rows = slice(rr * nk, (rr + 1) * nk)
            i0 = row_off + rr
            a = src[rows, ls]
            act = 0.5 * a * (1.0 + lax.erf(a * (2.0 ** -0.5)))
            gate = jnp.zeros((nk // packed, packed, lane_chunk), bf)
            for h in range(PEER_HEADS):
                c0 = row_tile(c0_ref, h, i0, ls)
                e0 = row_tile(e0_ref, h, i0, ls)
                r1 = r1_ref[h, :, ls].reshape(nk // packed, packed, lane_chunk)
                e1 = e1_ref[h, :, ls].reshape(nk // packed, packed, lane_chunk)
                gate = gate + jnp.where(r1 < c0, e0 * e1, jnp.zeros_like(e1))
            w = gate * act.astype(bf).reshape(nk // packed, packed, lane_chunk)
            dst[rows, ls] = w.reshape(nk, lane_chunk)

    def tile(src, dst, half, row_off):
        for c in range(tm // lane_chunk):
            ls = slice(c * lane_chunk, (c + 1) * lane_chunk)
            stage_b(src, dst, row_off, ls)
            o_ref[:, ls] += jnp.dot(vt_ref[:, half * te:(half + 1) * te], dst[:, ls],
                                    preferred_element_type=jnp.float32)

    @pl.when(m == 0)
    def _():
        o_ref[...] = jnp.zeros_like(o_ref)
        stage_a(ub_ref, a_s0)

    @pl.when(jnp.logical_and(m > 0, m < n_pairs))
    def _():
        stage_a(ua_ref, a_s1)
        tile(a_s0, w_s0, 0, 0)
        stage_a(ub_ref, a_s0)
        tile(a_s1, w_s1, 1, rows_per_tile)

    @pl.when(m == n_pairs)
    def _():
        stage_a(ua_ref, a_s1)
        tile(a_s0, w_s0, 0, 0)
        tile(a_s1, w_s1, 1, rows_per_tile)


def _peer(h2t, u_bf, vt_bf, a0, n1, e0, e1):
    d, s = h2t.shape
    ne = u_bf.shape[0]
    nk = PEER_N_KEYS
    tm = min(512, s)
    te = (SUBLANES // 2) * nk
    n_tiles = ne // te
    n_pairs = n_tiles // 2
    rows = pl.BlockSpec((PEER_HEADS, SUBLANES, tm), lambda i, m: (0, jnp.maximum(m - 1, 0), i))
    full_stat = pl.BlockSpec((PEER_HEADS, nk, tm), lambda i, m: (0, 0, i))
    return pl.pallas_call(
        functools.partial(_peer_kernel, n_pairs=n_pairs, lane_chunk=min(256, tm)),
        grid=(s // tm, n_pairs + 1),
        in_specs=[pl.BlockSpec((te, d), lambda i, m: (jnp.maximum(2 * m - 1, 0), 0)),
                  pl.BlockSpec((te, d), lambda i, m: (jnp.minimum(2 * m, n_tiles - 1), 0)),
                  pl.BlockSpec((d, 2 * te), lambda i, m: (0, jnp.maximum(m - 1, 0))),
                  pl.BlockSpec((d, tm), lambda i, m: (0, i)),
                  rows, rows, full_stat, full_stat],
        out_specs=pl.BlockSpec((d, tm), lambda i, m: (0, i)),
        out_shape=jax.ShapeDtypeStruct((d, s), jnp.float32),
        scratch_shapes=[pltpu.VMEM((te, tm), jnp.float32), pltpu.VMEM((te, tm), jnp.float32),
                        pltpu.VMEM((te, tm), jnp.bfloat16), pltpu.VMEM((te, tm), jnp.bfloat16)],
        compiler_params=_params(("arbitrary", "arbitrary")),
        name="peer",
    )(u_bf, u_bf, vt_bf, h2t, a0, e0, n1, e1)


def _final_kernel(ft_ref, x1_ref, gt_ref, lg_ref, lb_ref, o_ref, *, alpha):
    ffn = ft_ref[...].T
    t = alpha * x1_ref[...] + (1.0 + gt_ref[...]) * ffn
    o_ref[...] = _ln_rows(t) * lg_ref[...] + lb_ref[...]


def _final(ffn_t, x1, gt2, ln_g, ln_b, alpha):
    s, d = x1.shape
    tm = min(512, s)
    vec = pl.BlockSpec((1, d), lambda i: (0, 0))
    return pl.pallas_call(
        functools.partial(_final_kernel, alpha=alpha),
        grid=(s // tm,),
        in_specs=[pl.BlockSpec((d, tm), lambda i: (0, i)), pl.BlockSpec((tm, d), lambda i: (i, 0)),
                  vec, vec, vec],
        out_specs=pl.BlockSpec((tm, d), lambda i: (i, 0)),
        out_shape=jax.ShapeDtypeStruct((s, d), jnp.float32),
        compiler_params=_params(("arbitrary",)),
        name="final",
    )(ffn_t, x1, gt2, ln_g.reshape(1, -1), ln_b.reshape(1, -1))


def kernel(x, c, positions, w_ada, b_ada, w_in, lam_re, lam_im, log_dt, ssm_b_re, ssm_b_im, ssm_c_re,
           ssm_c_im, ssm_d, w_glu, b_glu, g_attn, g_ssm, w_out, ln1_g, ln1_b, w_qp, sub_keys, expert_u,
           expert_v, ln2_g, ln2_b):
    b, s, d = x.shape
    assert b == 1, "one sequence per call"
    depth = w_ada.shape[0]
    alpha = (2.0 * depth) ** 0.25
    bf = jnp.bfloat16
    x2 = x.reshape(s, d)
    pos = positions.reshape(s, 1)
    dilations = [dil for _, dil in DILATED_PATTERNS if dil > 1]
    for l in range(depth):
        mod = _ada(c, w_ada[l], b_ada[l])
        sh1, sc1, gt1, sh2, sc2, gt2 = [mod[:, k * d:(k + 1) * d] for k in range(ADA_CHUNKS)]
        z, *z_res = _inproj(x2, pos, sc1, sh1, w_in[l].astype(bf), dilations)
        z_dil = [z.reshape(1, s, -1) if dil == 1 else z_res[dilations.index(dil)]
                 for _, dil in DILATED_PATTERNS]
        attn = _attention(z, z_dil)
        yg = _s5(z, lam_re[l], lam_im[l], log_dt[l], ssm_b_re[l], ssm_b_im[l],
                 ssm_c_re[l], ssm_c_im[l], ssm_d[l])
        x1, h2 = _mixout(attn, yg, x2, w_glu[l].astype(bf), b_glu[l], g_attn[l], g_ssm[l],
                         w_out[l].astype(bf), gt1, ln1_g[l], ln1_b[l], sc2, sh2, alpha)
        a0, n1, e0, e1 = _peerq(h2, w_qp[l].astype(bf), sub_keys[l].astype(bf))
        ffn_t = _peer(h2.T, expert_u[l].astype(bf), expert_v[l].T.astype(bf), a0, n1, e0, e1)
        x2 = _final(ffn_t, x1, gt2, ln2_g[l], ln2_b[l], alpha)
    return x2.reshape(b, s, d)
```

```python
import functools
import math

import jax
import jax.numpy as jnp
from jax import lax
from jax.experimental import pallas as pl
from jax.experimental.pallas import tpu as pltpu

ATTN_HEADS = 16
HEAD_DIM = 64
ATTN_WIDTH = ATTN_HEADS * HEAD_DIM
SSM_GROUP = 16
SSM_STATE = 64
DILATED_PATTERNS = ((128, 1), (512, 4), (2048, 16))
ATTN_BLOCK = 128
ROPE_THETA = 10000.0
NEG_INF = -1e30
PEER_HEADS = 8
PEER_SUB_DIM = 128
PEER_N_KEYS = 128
PEER_TOPK = 16
LN_EPS = 1e-5
ADA_CHUNKS = 6

LANES = 128
SUBLANES = 8
VMEM_LIMIT_BYTES = 56 * 1024 * 1024

S5_CHUNK = 16
GROUPS_PER_BLOCK = LANES // SSM_GROUP
STATE_PER_BLOCK = GROUPS_PER_BLOCK * SSM_STATE
BIG_NEG = -3.0e38


def _params(sem):
    return pltpu.CompilerParams(dimension_semantics=sem, vmem_limit_bytes=VMEM_LIMIT_BYTES)


def _ln_rows(t):
    mu = jnp.mean(t, axis=-1, keepdims=True)
    d = t - mu
    var = jnp.mean(d * d, axis=-1, keepdims=True)
    return d * lax.rsqrt(var + LN_EPS)


def _ada_kernel(c_ref, w_ref, b_ref, o_ref):
    c = c_ref[...]
    sc = c * jax.nn.sigmoid(c)
    o_ref[...] = jnp.dot(sc, w_ref[...], preferred_element_type=jnp.float32,
                         precision=lax.Precision.HIGHEST) + b_ref[...]


def _ada(c, w_ada, b_ada):
    d = c.shape[-1]
    n = w_ada.shape[-1]
    tn = n // 8
    c8 = jnp.broadcast_to(c.reshape(1, d), (SUBLANES, d))
    out = pl.pallas_call(
        _ada_kernel,
        grid=(n // tn,),
        in_specs=[pl.BlockSpec((SUBLANES, d), lambda j: (0, 0)),
                  pl.BlockSpec((d, tn), lambda j: (0, j)),
                  pl.BlockSpec((1, tn), lambda j: (0, j))],
        out_specs=pl.BlockSpec((SUBLANES, tn), lambda j: (0, j)),
        out_shape=jax.ShapeDtypeStruct((SUBLANES, n), jnp.float32),
        compiler_params=_params(("arbitrary",)),
        name="ada",
    )(c8, w_ada, b_ada.reshape(1, n))
    return out[0:1]


def _inproj_kernel(x_ref, pos_ref, sc_ref, sh_ref, w_ref, z_ref, *rest, n_rope, dilations):
    dil_refs = rest[:len(dilations)]
    h_scr, cos_scr, sin_scr, z_scr = rest[len(dilations):]
    j = pl.program_id(1)
    tm, tn = z_ref.shape

    @pl.when(j == 0)
    def _():
        h = _ln_rows(x_ref[...]) * (1.0 + sc_ref[...]) + sh_ref[...]
        h_scr[...] = h.astype(jnp.bfloat16)
        lane = lax.broadcasted_iota(jnp.int32, (1, LANES), 1)
        fi = ((lane % HEAD_DIM) % (HEAD_DIM // 2)).astype(jnp.float32)
        inv_freq = jnp.exp(fi * (-math.log(ROPE_THETA) / (HEAD_DIM // 2)))
        ang = pos_ref[...].astype(jnp.float32) * inv_freq
        first_half = (lane % HEAD_DIM) < (HEAD_DIM // 2)
        cos_scr[...] = jnp.cos(ang)
        sin_scr[...] = jnp.where(first_half, -jnp.sin(ang), jnp.sin(ang))

    z = jnp.dot(h_scr[...], w_ref[...], preferred_element_type=jnp.float32)

    @pl.when(j < n_rope)
    def _():
        lane = lax.broadcasted_iota(jnp.int32, (1, LANES), 1)
        first_half = (lane % HEAD_DIM) < (HEAD_DIM // 2)
        cos = cos_scr[...]
        sin = sin_scr[...]
        for cc in range(tn // LANES):
            zc = z[:, cc * LANES:(cc + 1) * LANES]
            partner = jnp.where(first_half, pltpu.roll(zc, LANES - HEAD_DIM // 2, 1),
                                pltpu.roll(zc, HEAD_DIM // 2, 1))
            z_scr[cc] = zc * cos + partner * sin

    @pl.when(j >= n_rope)
    def _():
        for cc in range(tn // LANES):
            z_scr[cc] = z[:, cc * LANES:(cc + 1) * LANES]

    for cc in range(tn // LANES):
        z_ref[:, cc * LANES:(cc + 1) * LANES] = z_scr[cc].astype(z_ref.dtype)

    @pl.when(j < (3 * ATTN_WIDTH) // tn)
    def _():
        for dil, ref in zip(dilations, dil_refs):
            for r in range(dil):
                for cc in range(tn // LANES):
                    ref[r, :, cc * LANES:(cc + 1) * LANES] = (
                        z_scr[cc, pl.ds(r, tm // dil, stride=dil), :].astype(ref.dtype))


def _inproj(x2, pos, sc1, sh1, w_in_bf, dilations):
    s, d = x2.shape
    n = w_in_bf.shape[1]
    tm = min(1024, s)
    tn = 512
    n_rope = (2 * ATTN_WIDTH) // tn
    n_qkv = (3 * ATTN_WIDTH) // tn
    bf = jnp.bfloat16
    dil_specs = [pl.BlockSpec((dil, tm // dil, tn), lambda i, j: (0, i, jnp.minimum(j, n_qkv - 1)))
                 for dil in dilations]
    dil_shapes = [jax.ShapeDtypeStruct((dil, s // dil, 3 * ATTN_WIDTH), bf) for dil in dilations]
    return pl.pallas_call(
        functools.partial(_inproj_kernel, n_rope=n_rope, dilations=tuple(dilations)),
        grid=(s // tm, n // tn),
        in_specs=[pl.BlockSpec((tm, d), lambda i, j: (i, 0)),
                  pl.BlockSpec((tm, 1), lambda i, j: (i, 0)),
                  pl.BlockSpec((1, d), lambda i, j: (0, 0)),
                  pl.BlockSpec((1, d), lambda i, j: (0, 0)),
                  pl.BlockSpec((d, tn), lambda i, j: (0, j))],
        out_specs=[pl.BlockSpec((tm, tn), lambda i, j: (i, j))] + dil_specs,
        out_shape=[jax.ShapeDtypeStruct((s, n), bf)] + dil_shapes,
        scratch_shapes=[pltpu.VMEM((tm, d), bf),
                        pltpu.VMEM((tm, LANES), jnp.float32),
                        pltpu.VMEM((tm, LANES), jnp.float32),
                        pltpu.VMEM((tn // LANES, tm, LANES), jnp.float32)],
        compiler_params=_params(("arbitrary", "arbitrary")),
        name="inproj",
    )(x2, pos, sc1, sh1, w_in_bf)


def _attn_kernel(q_ref, kc_ref, vc_ref, kp_ref, vp_ref, o_ref, lse_ref, kwin, vt_win, *, nblk):
    i = pl.program_id(1)
    blk = ATTN_BLOCK
    kwin[0:blk, :] = kp_ref[...]
    kwin[blk:, :] = kc_ref[...]
    vt_win[:, 0:blk] = vp_ref[...].T
    vt_win[:, blk:] = vc_ref[...].T

    key = lax.broadcasted_iota(jnp.int32, (2 * blk, blk), 0)
    qry = lax.broadcasted_iota(jnp.int32, (2 * blk, blk), 1)
    lane = lax.broadcasted_iota(jnp.int32, (blk, LANES), 1)
    row = lax.broadcasted_iota(jnp.int32, (blk, LANES), 0)
    nt = (((1,), (1,)), ((), ()))
    scale = HEAD_DIM ** -0.5

    def body(j, carry):
        r0 = pl.multiple_of(j * blk, blk)
        first_key = jnp.where((i * nblk + j) > 0, 0, blk)
        mask = jnp.logical_and(key >= jnp.maximum(qry, first_key), key <= qry + blk)
        lse_rows = []
        for hp in range(ATTN_HEADS // 2):
            cs = slice(hp * LANES, (hp + 1) * LANES)
            q2 = q_ref[pl.ds(r0, blk), cs]
            k2 = kwin[pl.ds(r0, 2 * blk), cs]
            vt2 = vt_win[cs, pl.ds(r0, 2 * blk)]
            halves = []
            for hd in range(2):
                in_head = (lane < HEAD_DIM) if hd == 0 else (lane >= HEAD_DIM)
                qm = jnp.where(in_head, q2, jnp.zeros_like(q2)) * jnp.asarray(scale, q2.dtype)
                st = lax.dot_general(k2, qm, nt, preferred_element_type=jnp.float32)
                st = jnp.where(mask, st, NEG_INF)
                m = jnp.max(st, axis=0, keepdims=True)
                p = jnp.exp(st - m)
                l = jnp.sum(p, axis=0, keepdims=True)
                ot = jnp.dot(vt2, p.astype(vt2.dtype), preferred_element_type=jnp.float32)
                halves.append(ot / l)
                lse_rows.append(m + jnp.log(l))
            ot2 = jnp.where(row < HEAD_DIM, halves[0], halves[1])
            o_ref[pl.ds(r0, blk), cs] = ot2.T.astype(o_ref.dtype)
        lse_mat = jnp.zeros((LANES, blk), jnp.float32)
        for hidx, lse_h in enumerate(lse_rows):
            lse_mat = jnp.where(row == hidx, lse_h, lse_mat)
        lse_ref[pl.ds(r0, blk), :] = lse_mat.T
        return carry

    lax.fori_loop(0, nblk, body, 0)


def _attn_pattern(zd, dilation):
    _, length, zw = zd.shape
    aw = ATTN_WIDTH
    rows = min(1024, length)
    nblk = rows // ATTN_BLOCK
    cur = lambda off: pl.BlockSpec((None, rows, aw), lambda r, i: (r, i, off))
    prev = lambda off: pl.BlockSpec((None, ATTN_BLOCK, aw), lambda r, i: (r, jnp.maximum(i * nblk - 1, 0), off))
    return pl.pallas_call(
        functools.partial(_attn_kernel, nblk=nblk),
        grid=(dilation, length // rows),
        in_specs=[cur(0), cur(1), cur(2), prev(1), prev(2)],
        out_specs=[pl.BlockSpec((None, rows, aw), lambda r, i: (r, i, 0)),
                   pl.BlockSpec((None, rows, LANES), lambda r, i: (r, i, 0))],
        out_shape=[jax.ShapeDtypeStruct((dilation, length, aw), jnp.bfloat16),
                   jax.ShapeDtypeStruct((dilation, length, LANES), jnp.float32)],
        scratch_shapes=[pltpu.VMEM((rows + ATTN_BLOCK, aw), jnp.bfloat16),
                        pltpu.VMEM((aw, rows + ATTN_BLOCK), jnp.bfloat16)],
        compiler_params=_params(("arbitrary", "arbitrary")),
        name=f"attn_d{dilation}",
    )(zd, zd, zd, zd, zd)


def _attention(z, z_dil):
    outs = []
    for (window, dilation), zd in zip(DILATED_PATTERNS, z_dil):
        assert window // dilation == ATTN_BLOCK
        outs.append(_attn_pattern(zd, dilation))
    return outs


def _s5_param_kernel(lr_r, li_r, ld_r, bre_ref, bim_ref, cre_ref, cim_ref,
                     krev_ref, pblk_ref, qblk_ref, at_ref):
    t_chunk = S5_CHUNK
    nst = STATE_PER_BLOCK
    lr, li, ld = lr_r[...], li_r[...], ld_r[...]
    dt = jnp.exp(ld)
    mag = jnp.exp(lr * dt)
    ar, ai = mag * jnp.cos(li * dt), mag * jnp.sin(li * dt)
    den = lr * lr + li * li
    cr = ((ar - 1.0) * lr + ai * li) / den
    ci = (ai * lr - (ar - 1.0) * li) / den
    bre, bim = bre_ref[...], bim_ref[...]
    bbr = cr * bre - ci * bim
    bbi = cr * bim + ci * bre
    cre, cim = cre_ref[...], cim_ref[...]
    hi = lax.Precision.HIGHEST
    nt = (((1,), (1,)), ((), ()))
    er, ei = jnp.ones_like(ar), jnp.zeros_like(ai)
    for tau in range(t_chunk):
        pr = er * bbr - ei * bbi
        pi = er * bbi + ei * bbr
        blk = t_chunk - 1 - tau
        rows = slice(blk * LANES, (blk + 1) * LANES)
        pblk_ref[rows, 0:nst] = pr.astype(pblk_ref.dtype)
        pblk_ref[rows, nst:2 * nst] = pi.astype(pblk_ref.dtype)
        k_tau = (lax.dot_general(pr, cre, nt, preferred_element_type=jnp.float32, precision=hi)
                 - lax.dot_general(pi, cim, nt, preferred_element_type=jnp.float32, precision=hi))
        krev_ref[rows, :] = k_tau.astype(krev_ref.dtype)
        er, ei = er * ar - ei * ai, er * ai + ei * ar
        qblk_ref[tau, :, 0:nst] = (cre * er - cim * ei).astype(qblk_ref.dtype)
        qblk_ref[tau, :, nst:2 * nst] = (-(cre * ei + cim * er)).astype(qblk_ref.dtype)
    at_ref[:, 0:nst] = er
    at_ref[:, nst:2 * nst] = ei


def _s5_params(lam_re, lam_im, log_dt, b_re, b_im, c_re, c_im):
    g, n = lam_re.shape
    p = b_re.shape[-1]
    gb = GROUPS_PER_BLOCK
    nlb = g // gb
    nst = gb * n
    f32 = jnp.float32
    eye = jnp.eye(gb, dtype=f32)

    def row(a):
        return a.astype(f32).reshape(nlb, 1, nst)

    ldt = jnp.broadcast_to(log_dt.astype(f32)[:, None], (g, n))

    def b_blockdiag(b):
        b4 = b.astype(f32).reshape(nlb, gb, n, p)
        return jnp.einsum("lhnq,gh->lgqhn", b4, eye).reshape(nlb, gb * p, nst)

    def c_blockdiag(c):
        c4 = c.astype(f32).reshape(nlb, gb, p, n)
        return jnp.einsum("lhpn,gh->lgphn", c4, eye).reshape(nlb, gb * p, nst)

    tc = S5_CHUNK
    vec_r = pl.BlockSpec((None, 1, nst), lambda l: (l, 0, 0))
    mat = pl.BlockSpec((None, LANES, nst), lambda l: (l, 0, 0))
    return pl.pallas_call(
        _s5_param_kernel,
        grid=(nlb,),
        in_specs=[vec_r, vec_r, vec_r, mat, mat, mat, mat],
        out_specs=[pl.BlockSpec((None, tc * LANES, LANES), lambda l: (l, 0, 0)),
                   pl.BlockSpec((None, tc * LANES, 2 * nst), lambda l: (l, 0, 0)),
                   pl.BlockSpec((None, tc, LANES, 2 * nst), lambda l: (l, 0, 0, 0)),
                   pl.BlockSpec((None, 1, 2 * nst), lambda l: (l, 0, 0))],
        out_shape=[jax.ShapeDtypeStruct((nlb, tc * LANES, LANES), jnp.bfloat16),
                   jax.ShapeDtypeStruct((nlb, tc * LANES, 2 * nst), jnp.bfloat16),
                   jax.ShapeDtypeStruct((nlb, tc, LANES, 2 * nst), jnp.bfloat16),
                   jax.ShapeDtypeStruct((nlb, 1, 2 * nst), f32)],
        compiler_params=_params(("arbitrary",)),
        name="s5_params",
    )(row(lam_re), row(lam_im), row(ldt),
      b_blockdiag(b_re), b_blockdiag(b_im), c_blockdiag(c_re), c_blockdiag(c_im))


def _load_ucat(u_ref, uf_scr, ucat_scr, nc):
    uf_scr[...] = u_ref[...].astype(jnp.float32)
    for s in range(S5_CHUNK):
        ucat_scr[:, s * LANES:(s + 1) * LANES] = (
            uf_scr[pl.ds(s, nc, stride=S5_CHUNK), :].astype(ucat_scr.dtype))


def _s5_state_in_kernel(u_ref, pblk_ref, b_ref, uf_scr, ucat_scr, *, nc):
    _load_ucat(u_ref, uf_scr, ucat_scr, nc)
    b_ref[...] = jnp.dot(ucat_scr[...], pblk_ref[...], preferred_element_type=jnp.float32)


def _s5_scan_kernel(b_ref, at_ref, h_ref, *, nc, nlb):
    nst = STATE_PER_BLOCK
    a_re = at_ref[:, 0:nst]
    a_im = at_ref[:, nst:2 * nst]

    def body(c, carry):
        hr, hi = carry
        h_ref[c, :, 0:nst] = hr
        h_ref[c, :, nst:2 * nst] = hi
        bc = b_ref[c]
        return (a_re * hr - a_im * hi + bc[:, 0:nst], a_re * hi + a_im * hr + bc[:, nst:2 * nst])

    zero = jnp.zeros((nlb, nst), jnp.float32)
    lax.fori_loop(0, nc, body, (zero, zero))


def _s5_out_kernel(u_ref, h_ref, krev_ref, qblk_ref, dsk_ref, y_ref, uf_scr, ucat_scr, y_scr, *, nc):
    _load_ucat(u_ref, uf_scr, ucat_scr, nc)
    hb = h_ref[...].astype(jnp.bfloat16)
    dsk = dsk_ref[...]
    tc = S5_CHUNK
    for t in range(tc):
        kd = (t + 1) * LANES
        y = jnp.dot(ucat_scr[:, 0:kd], krev_ref[(tc - 1 - t) * LANES:, :],
                    preferred_element_type=jnp.float32)
        y = y + lax.dot_general(hb, qblk_ref[t], (((1,), (1,)), ((), ())), preferred_element_type=jnp.float32)
        y = y + dsk * uf_scr[pl.ds(t, nc, stride=tc), :]
        y = 0.5 * y * (1.0 + lax.erf(y * (2.0 ** -0.5)))
        y_scr[pl.ds(t, nc, stride=tc), :] = y
    y_ref[...] = y_scr[...].astype(y_ref.dtype)


def _s5(z, lam_re, lam_im, log_dt, b_re, b_im, c_re, c_im, d_skip):
    s = z.shape[0]
    nlb = lam_re.shape[0] // GROUPS_PER_BLOCK
    nc = s // S5_CHUNK
    nst2 = 2 * STATE_PER_BLOCK
    u_col0 = (3 * ATTN_WIDTH) // LANES
    krev, pblk, qblk, at = _s5_params(lam_re, lam_im, log_dt, b_re, b_im, c_re, c_im)
    u_spec = pl.BlockSpec((s, LANES), lambda l: (0, u_col0 + l))
    b2 = pl.pallas_call(
        functools.partial(_s5_state_in_kernel, nc=nc),
        grid=(nlb,),
        in_specs=[u_spec, pl.BlockSpec((None, S5_CHUNK * LANES, nst2), lambda l: (l, 0, 0))],
        out_specs=pl.BlockSpec((nc, nst2), lambda l: (0, l)),
        out_shape=jax.ShapeDtypeStruct((nc, nlb * nst2), jnp.float32),
        scratch_shapes=[pltpu.VMEM((s, LANES), jnp.float32),
                        pltpu.VMEM((nc, S5_CHUNK * LANES), jnp.bfloat16)],
        compiler_params=_params(("arbitrary",)),
        name="s5_state_in",
    )(z, pblk)
    h3 = pl.pallas_call(
        functools.partial(_s5_scan_kernel, nc=nc, nlb=nlb),
        out_shape=jax.ShapeDtypeStruct((nc, nlb, nst2), jnp.float32),
        compiler_params=pltpu.CompilerParams(vmem_limit_bytes=VMEM_LIMIT_BYTES),
        name="s5_scan",
    )(b2.reshape(nc, nlb, nst2), at.reshape(nlb, nst2))
    return pl.pallas_call(
        functools.partial(_s5_out_kernel, nc=nc),
        grid=(nlb,),
        in_specs=[u_spec,
                  pl.BlockSpec((nc, nst2), lambda l: (0, l)),
                  pl.BlockSpec((None, S5_CHUNK * LANES, LANES), lambda l: (l, 0, 0)),
                  pl.BlockSpec((None, S5_CHUNK, LANES, nst2), lambda l: (l, 0, 0, 0)),
                  pl.BlockSpec((1, LANES), lambda l: (0, l))],
        out_specs=pl.BlockSpec((s, LANES), lambda l: (0, l)),
        out_shape=jax.ShapeDtypeStruct((s, nlb * LANES), jnp.bfloat16),
        scratch_shapes=[pltpu.VMEM((s, LANES), jnp.float32),
                        pltpu.VMEM((nc, S5_CHUNK * LANES), jnp.bfloat16),
                        pltpu.VMEM((s, LANES), jnp.float32)],
        compiler_params=_params(("arbitrary",)),
        name="s5_out",
    )(z, h3.reshape(nc, nlb * nst2), krev, qblk, d_skip.astype(jnp.float32).reshape(1, -1))


def _mixout_kernel(*refs, alpha, n_pat):
    o_refs = refs[:n_pat]
    lse_refs = refs[n_pat:2 * n_pat]
    (yg_ref, x_ref, wglu_ref, bglu_ref, ga_ref, gs_ref, wout_ref,
     gt_ref, lg_ref, lb_ref, sc_ref, sh_ref, x1_ref, h2_ref, o_scr, lse_scr) = refs[2 * n_pat:]
    planes, tm, _ = o_scr.shape
    aw = planes * LANES

    def token_major(ref, scr):
        dil = ref.shape[0]
        if dil == 1:
            return ref[0].astype(jnp.float32)
        n_planes = ref.shape[2] // LANES
        for r in range(dil):
            for cc in range(n_planes):
                scr[cc, pl.ds(r, tm // dil, stride=dil), :] = (
                    ref[r, :, cc * LANES:(cc + 1) * LANES].astype(jnp.float32))
        return jnp.concatenate([scr[cc] for cc in range(n_planes)], axis=1)

    lses = [token_major(ref, lse_scr) for ref in lse_refs]
    top = functools.reduce(jnp.maximum, lses)
    ws = [jnp.exp(v - top) for v in lses]
    inv = 1.0 / functools.reduce(lambda u, v: u + v, ws)
    head_of_lane = lax.broadcasted_iota(jnp.int32, (LANES, aw), 1) // HEAD_DIM
    spread = (lax.broadcasted_iota(jnp.int32, (LANES, aw), 0) == head_of_lane).astype(jnp.bfloat16)
    spread2 = jnp.concatenate([spread, spread], axis=0)
    a = jnp.zeros((tm, aw), jnp.float32)
    for w, o_ref in zip(ws, o_refs):
        w = w * inv
        w_hi = w.astype(jnp.bfloat16)
        w_lo = (w - w_hi.astype(jnp.float32)).astype(jnp.bfloat16)
        wide = jnp.dot(jnp.concatenate([w_hi, w_lo], axis=1), spread2, preferred_element_type=jnp.float32)
        a = a + wide * token_major(o_ref, o_scr)
    ra = a * lax.rsqrt(jnp.mean(a * a, axis=-1, keepdims=True) + LN_EPS) * ga_ref[...]
    yg = yg_ref[...]
    y = yg.astype(jnp.float32)
    gate = jax.nn.sigmoid(jnp.dot(yg, wglu_ref[...], preferred_element_type=jnp.float32) + bglu_ref[...])
    y = y * gate
    ry = y * lax.rsqrt(jnp.mean(y * y, axis=-1, keepdims=True) + LN_EPS) * gs_ref[...]
    mix = (jnp.dot(ra.astype(jnp.bfloat16), wout_ref[0:aw, :], preferred_element_type=jnp.float32)
           + jnp.dot(ry.astype(jnp.bfloat16), wout_ref[aw:, :], preferred_element_type=jnp.float32))
    t = alpha * x_ref[...] + (1.0 + gt_ref[...]) * mix
    x1 = _ln_rows(t) * lg_ref[...] + lb_ref[...]
    x1_ref[...] = x1
    h2_ref[...] = (_ln_rows(x1) * (1.0 + sc_ref[...]) + sh_ref[...]).astype(h2_ref.dtype)


def _mixout(attn_parts, yg, x2, w_glu_bf, b_glu, g_attn, g_ssm, w_out_bf, gt1, ln_g, ln_b, sc2, sh2, alpha):
    s, d = x2.shape
    aw = attn_parts[0][0].shape[-1]
    sw = yg.shape[1]
    n_pat = len(attn_parts)
    tm = min(512, s)
    row = lambda i: (i, 0)
    fix = lambda i: (0, 0)
    vec = lambda n: pl.BlockSpec((1, n), fix)
    res = lambda a: pl.BlockSpec((a.shape[0], tm // a.shape[0], a.shape[2]), lambda i: (0, i, 0))
    return pl.pallas_call(
        functools.partial(_mixout_kernel, alpha=alpha, n_pat=n_pat),
        grid=(s // tm,),
        in_specs=([res(o) for o, _ in attn_parts] + [res(v) for _, v in attn_parts]
                  + [pl.BlockSpec((tm, sw), row), pl.BlockSpec((tm, d), row),
                     pl.BlockSpec((sw, sw), fix), vec(sw), vec(aw), vec(sw),
                     pl.BlockSpec((aw + sw, d), fix), vec(d), vec(d), vec(d), vec(d), vec(d)]),
        out_specs=[pl.BlockSpec((tm, d), row), pl.BlockSpec((tm, d), row)],
        out_shape=[jax.ShapeDtypeStruct((s, d), jnp.float32), jax.ShapeDtypeStruct((s, d), jnp.bfloat16)],
        scratch_shapes=[pltpu.VMEM((aw // LANES, tm, LANES), jnp.float32),
                        pltpu.VMEM((1, tm, LANES), jnp.float32)],
        compiler_params=_params(("arbitrary",)),
        name="mixout",
    )(*[o for o, _ in attn_parts], *[l for _, l in attn_parts], yg, x2, w_glu_bf, b_glu.reshape(1, -1),
      g_attn.reshape(1, -1), g_ssm.reshape(1, -1), w_out_bf, gt1, ln_g.reshape(1, -1), ln_b.reshape(1, -1),
      sc2, sh2)


def _top_values(sc, k, want_rank=False):
    vals = []
    rank = jnp.full(sc.shape, float(k), jnp.float32) if want_rank else None
    for j in range(k):
        m = jnp.max(sc, axis=0, keepdims=True)
        vals.append(m)
        hit = sc == m
        if want_rank:
            rank = jnp.where(hit, float(j), rank)
        sc = jnp.where(hit, BIG_NEG, sc)
    return vals, rank


def _peerq_kernel(h2_ref, wqp_ref, keys_ref, c0_ref, r1_ref, e0_ref, e1_ref, sc_scr):
    k = PEER_TOPK
    tm = h2_ref.shape[0]
    qp = jnp.dot(h2_ref[...], wqp_ref[...], preferred_element_type=jnp.float32).astype(jnp.bfloat16)
    nt = (((1,), (1,)), ((), ()))
    for hi in range(2 * PEER_HEADS):
        col = hi * PEER_SUB_DIM
        sc_scr[hi] = lax.dot_general(keys_ref[hi % 2], qp[:, col:col + PEER_SUB_DIM], nt,
                                     preferred_element_type=jnp.float32)
    def body(h, carry):
        s0 = sc_scr[2 * h]
        s1 = sc_scr[2 * h + 1]
        top0, _ = _top_values(s0, k)
        top1, rank1 = _top_values(s1, k, want_rank=True)
        cands = [top0[i] + top1[j] for i in range(k) for j in range(k) if (i + 1) * (j + 1) <= k]
        pad = (-len(cands)) % SUBLANES
        cmat = jnp.concatenate(cands + [jnp.full_like(cands[0], BIG_NEG)] * pad, axis=0)
        best, _ = _top_values(cmat, k)
        thr = best[k - 1]
        m0, m1 = top0[0], top1[0]
        zsum = jnp.sum(jnp.where(cmat >= thr, jnp.exp(cmat - (m0 + m1)), 0.0), axis=0, keepdims=True)
        count0 = jnp.zeros_like(s0)
        for j in range(k):
            count0 = jnp.where(s0 + top1[j] >= thr, float(j + 1), count0)
        c0_ref[h] = count0
        r1_ref[h] = rank1.astype(r1_ref.dtype)
        e0_ref[h] = jnp.exp(s0 - m0) / zsum
        e1_ref[h] = jnp.exp(s1 - m1).astype(e1_ref.dtype)
        return carry

    lax.fori_loop(0, PEER_HEADS, body, 0)


def _peerq(h2, w_qp_bf, keys_bf):
    s, d = h2.shape
    tm = min(512, s)
    nk = keys_bf.shape[1]
    stat = pl.BlockSpec((PEER_HEADS, nk, tm), lambda i: (0, 0, i))
    words = jax.ShapeDtypeStruct((PEER_HEADS, nk, s), jnp.float32)
    halfs = jax.ShapeDtypeStruct((PEER_HEADS, nk, s), jnp.bfloat16)
    return pl.pallas_call(
        _peerq_kernel,
        grid=(s // tm,),
        in_specs=[pl.BlockSpec((tm, d), lambda i: (i, 0)),
                  pl.BlockSpec(w_qp_bf.shape, lambda i: (0, 0)),
                  pl.BlockSpec(keys_bf.shape, lambda i: (0, 0, 0))],
        out_specs=[stat, stat, stat, stat],
        out_shape=[words, halfs, words, halfs],
        scratch_shapes=[pltpu.VMEM((2 * PEER_HEADS, nk, tm), jnp.float32)],
        compiler_params=_params(("arbitrary",)),
        name="peerq",
    )(h2, w_qp_bf, keys_bf)


def _peer_kernel(ua_ref, ub_ref, vt_ref, h2t_ref, c0_ref, e0_ref, r1_ref, e1_ref, o_ref,
                 a_s0, a_s1, w_s0, w_s1, *, n_pairs, lane_chunk):
    m = pl.program_id(1)
    nk = PEER_N_KEYS
    te, tm = a_s0.shape
    rows_per_tile = te // nk
    packed = 2 * SUBLANES
    bf = jnp.bfloat16

    def stage_a(u_ref, dst):
        dst[...] = jnp.dot(u_ref[...], h2t_ref[...], preferred_element_type=jnp.float32)

    def row_tile(ref, h, i0, ls):
        return jnp.broadcast_to(ref[h, i0:i0 + 1, ls], (packed, lane_chunk)).astype(bf)

    def stage_b(src, dst, row_off, ls):
        for rr in range(rows_per_tile):
            rows = slice(rr * nk, (rr + 1) * nk)
            i0 = row_off + rr
            a = src[rows, ls]
            act = 0.5 * a * (1.0 + lax.erf(a * (2.0 ** -0.5)))
            gate = jnp.zeros((nk // packed, packed, lane_chunk), bf)
            for h in range(PEER_HEADS):
                c0 = row_tile(c0_ref, h, i0, ls)
                e0 = row_tile(e0_ref, h, i0, ls)
                r1 = r1_ref[h, :, ls].reshape(nk // packed, packed, lane_chunk)
                e1 = e1_ref[h, :, ls].reshape(nk // packed, packed, lane_chunk)
                gate = gate + jnp.where(r1 < c0, e0 * e1, jnp.zeros_like(e1))
            w = gate * act.astype(bf).reshape(nk // packed, packed, lane_chunk)
            dst[rows, ls] = w.reshape(nk, lane_chunk)

    def tile(src, dst, half, row_off):
        for c in range(tm // lane_chunk):
            ls = slice(c * lane_chunk, (c + 1) * lane_chunk)
            stage_b(src, dst, row_off, ls)
            o_ref[:, ls] += jnp.dot(vt_ref[:, half * te:(half + 1) * te], dst[:, ls],
                                    preferred_element_type=jnp.float32)

    @pl.when(m == 0)
    def _():
        o_ref[...] = jnp.zeros_like(o_ref)
        stage_a(ub_ref, a_s0)

    @pl.when(jnp.logical_and(m > 0, m < n_pairs))
    def _():
        stage_a(ua_ref, a_s1)
        tile(a_s0, w_s0, 0, 0)
        stage_a(ub_ref, a_s0)
        tile(a_s1, w_s1, 1, rows_per_tile)

    @pl.when(m == n_pairs)
    def _():
        stage_a(ua_ref, a_s1)
        tile(a_s0, w_s0, 0, 0)
        tile(a_s1, w_s1, 1, rows_per_tile)


def _peer(h2t, u_bf, vt_bf, a0, n1, e0, e1):
    d, s = h2t.shape
    ne = u_bf.shape[0]
    nk = PEER_N_KEYS
    tm = min(512, s)
    te = (SUBLANES // 2) * nk
    n_tiles = ne // te
    n_pairs = n_tiles // 2
    rows = pl.BlockSpec((PEER_HEADS, SUBLANES, tm), lambda i, m: (0, jnp.maximum(m - 1, 0), i))
    full_stat = pl.BlockSpec((PEER_HEADS, nk, tm), lambda i, m: (0, 0, i))
    return pl.pallas_call(
        functools.partial(_peer_kernel, n_pairs=n_pairs, lane_chunk=min(256, tm)),
        grid=(s // tm, n_pairs + 1),
        in_specs=[pl.BlockSpec((te, d), lambda i, m: (jnp.maximum(2 * m - 1, 0), 0)),
                  pl.BlockSpec((te, d), lambda i, m: (jnp.minimum(2 * m, n_tiles - 1), 0)),
                  pl.BlockSpec((d, 2 * te), lambda i, m: (0, jnp.maximum(m - 1, 0))),
                  pl.BlockSpec((d, tm), lambda i, m: (0, i)),
                  rows, rows, full_stat, full_stat],
        out_specs=pl.BlockSpec((d, tm), lambda i, m: (0, i)),
        out_shape=jax.ShapeDtypeStruct((d, s), jnp.float32),
        scratch_shapes=[pltpu.VMEM((te, tm), jnp.float32), pltpu.VMEM((te, tm), jnp.float32),
                        pltpu.VMEM((te, tm), jnp.bfloat16), pltpu.VMEM((te, tm), jnp.bfloat16)],
        compiler_params=_params(("arbitrary", "arbitrary")),
        name="peer",
    )(u_bf, u_bf, vt_bf, h2t, a0, e0, n1, e1)


def _final_kernel(ft_ref, x1_ref, gt_ref, lg_ref, lb_ref, o_ref, *, alpha):
    ffn = ft_ref[...].T
    t = alpha * x1_ref[...] + (1.0 + gt_ref[...]) * ffn
    o_ref[...] = _ln_rows(t) * lg_ref[...] + lb_ref[...]


def _final(ffn_t, x1, gt2, ln_g, ln_b, alpha):
    s, d = x1.shape
    tm = min(512, s)
    vec = pl.BlockSpec((1, d), lambda i: (0, 0))
    return pl.pallas_call(
        functools.partial(_final_kernel, alpha=alpha),
        grid=(s // tm,),
        in_specs=[pl.BlockSpec((d, tm), lambda i: (0, i)), pl.BlockSpec((tm, d), lambda i: (i, 0)),
                  vec, vec, vec],
        out_specs=pl.BlockSpec((tm, d), lambda i: (i, 0)),
        out_shape=jax.ShapeDtypeStruct((s, d), jnp.float32),
        compiler_params=_params(("arbitrary",)),
        name="final",
    )(ffn_t, x1, gt2, ln_g.reshape(1, -1), ln_b.reshape(1, -1))


def kernel(x, c, positions, w_ada, b_ada, w_in, lam_re, lam_im, log_dt, ssm_b_re, ssm_b_im, ssm_c_re,
           ssm_c_im, ssm_d, w_glu, b_glu, g_attn, g_ssm, w_out, ln1_g, ln1_b, w_qp, sub_keys, expert_u,
           expert_v, ln2_g, ln2_b):
    b, s, d = x.shape
    assert b == 1, "one sequence per call"
    depth = w_ada.shape[0]
    alpha = (2.0 * depth) ** 0.25
    bf = jnp.bfloat16
    x2 = x.reshape(s, d)
    pos = positions.reshape(s, 1)
    dilations = [dil for _, dil in DILATED_PATTERNS if dil > 1]
    for l in range(depth):
        mod = _ada(c, w_ada[l], b_ada[l])
        sh1, sc1, gt1, sh2, sc2, gt2 = [mod[:, k * d:(k + 1) * d] for k in range(ADA_CHUNKS)]
        z, *z_res = _inproj(x2, pos, sc1, sh1, w_in[l].astype(bf), dilations)
        z_dil = [z.reshape(1, s, -1) if dil == 1 else z_res[dilations.index(dil)]
                 for _, dil in DILATED_PATTERNS]
        attn = _attention(z, z_dil)
        yg = _s5(z, lam_re[l], lam_im[l], log_dt[l], ssm_b_re[l], ssm_b_im[l],
                 ssm_c_re[l], ssm_c_im[l], ssm_d[l])
        x1, h2 = _mixout(attn, yg, x2, w_glu[l].astype(bf), b_glu[l], g_attn[l], g_ssm[l],
                         w_out[l].astype(bf), gt1, ln1_g[l], ln1_b[l], sc2, sh2, alpha)
        a0, n1, e0, e1 = _peerq(h2, w_qp[l].astype(bf), sub_keys[l].astype(bf))
        ffn_t = _peer(h2.T, expert_u[l].astype(bf), expert_v[l].T.astype(bf), a0, n1, e0, e1)
        x2 = _final(ffn_t, x1, gt2, ln2_g[l], ln2_b[l], alpha)
    return x2.reshape(b, s, d)
```

```python
import functools
import math

import jax
import jax.numpy as jnp
from jax import lax
from jax.experimental import pallas as pl
from jax.experimental.pallas import tpu as pltpu

ATTN_HEADS = 16
HEAD_DIM = 64
ATTN_WIDTH = ATTN_HEADS * HEAD_DIM
SSM_GROUP = 16
SSM_STATE = 64
DILATED_PATTERNS = ((128, 1), (512, 4), (2048, 16))
ATTN_BLOCK = 128
ROPE_THETA = 10000.0
NEG_INF = -1e30
PEER_HEADS = 8
PEER_SUB_DIM = 128
PEER_N_KEYS = 128
PEER_TOPK = 16
LN_EPS = 1e-5
ADA_CHUNKS = 6

LANES = 128
SUBLANES = 8
VMEM_LIMIT_BYTES = 56 * 1024 * 1024

S5_CHUNK = 16
GROUPS_PER_BLOCK = LANES // SSM_GROUP
STATE_PER_BLOCK = GROUPS_PER_BLOCK * SSM_STATE
BIG_NEG = -3.0e38
QK_LOG2_SCALE = HEAD_DIM ** -0.5 * math.log2(math.e)


def _params(sem):
    return pltpu.CompilerParams(dimension_semantics=sem, vmem_limit_bytes=VMEM_LIMIT_BYTES)


def _ln_rows(t):
    mu = jnp.mean(t, axis=-1, keepdims=True)
    d = t - mu
    var = jnp.mean(d * d, axis=-1, keepdims=True)
    return d * lax.rsqrt(var + LN_EPS)


def _ada_kernel(c_ref, w_ref, b_ref, o_ref):
    c = c_ref[...]
    sc = c * jax.nn.sigmoid(c)
    o_ref[...] = jnp.sum(sc * w_ref[...], axis=0, keepdims=True) + b_ref[...]


def _ada(c, w_ada, b_ada):
    d = c.shape[-1]
    n = w_ada.shape[-1]
    tn = n // 8
    return pl.pallas_call(
        _ada_kernel,
        grid=(n // tn,),
        in_specs=[pl.BlockSpec((d, 1), lambda j: (0, 0)),
                  pl.BlockSpec((d, tn), lambda j: (0, j)),
                  pl.BlockSpec((1, tn), lambda j: (0, j))],
        out_specs=pl.BlockSpec((1, tn), lambda j: (0, j)),
        out_shape=jax.ShapeDtypeStruct((1, n), jnp.float32),
        compiler_params=_params(("arbitrary",)),
        name="ada",
    )(c.reshape(d, 1), w_ada, b_ada.reshape(1, n))


def _inproj_kernel(x_ref, pos_ref, sc_ref, sh_ref, w_ref, z_ref, *rest, n_rope, dilations):
    dil_refs = rest[:len(dilations)]
    h_scr, cos_scr, sin_scr, z_scr = rest[len(dilations):]
    j = pl.program_id(1)
    tm, tn = z_ref.shape

    @pl.when(j == 0)
    def _():
        h = _ln_rows(x_ref[...]) * (1.0 + sc_ref[...]) + sh_ref[...]
        h_scr[...] = h.astype(jnp.bfloat16)
        lane = lax.broadcasted_iota(jnp.int32, (1, LANES), 1)
        fi = ((lane % HEAD_DIM) % (HEAD_DIM // 2)).astype(jnp.float32)
        inv_freq = jnp.exp(fi * (-math.log(ROPE_THETA) / (HEAD_DIM // 2)))
        ang = pos_ref[...].astype(jnp.float32) * inv_freq
        first_half = (lane % HEAD_DIM) < (HEAD_DIM // 2)
        cos_scr[...] = jnp.cos(ang)
        sin_scr[...] = jnp.where(first_half, -jnp.sin(ang), jnp.sin(ang))

    z = jnp.dot(h_scr[...], w_ref[...], preferred_element_type=jnp.float32)

    @pl.when(j < n_rope)
    def _():
        lane = lax.broadcasted_iota(jnp.int32, (1, LANES), 1)
        first_half = (lane % HEAD_DIM) < (HEAD_DIM // 2)
        q_mult = jnp.where(j < n_rope // 2, QK_LOG2_SCALE, 1.0)
        cos = cos_scr[...] * q_mult
        sin = sin_scr[...] * q_mult
        for cc in range(tn // LANES):
            zc = z[:, cc * LANES:(cc + 1) * LANES]
            partner = jnp.where(first_half, pltpu.roll(zc, LANES - HEAD_DIM // 2, 1),
                                pltpu.roll(zc, HEAD_DIM // 2, 1))
            z_scr[cc] = zc * cos + partner * sin

    @pl.when(j >= n_rope)
    def _():
        for cc in range(tn // LANES):
            z_scr[cc] = z[:, cc * LANES:(cc + 1) * LANES]

    for cc in range(tn // LANES):
        z_ref[:, cc * LANES:(cc + 1) * LANES] = z_scr[cc].astype(z_ref.dtype)

    @pl.when(j < (3 * ATTN_WIDTH) // tn)
    def _():
        for dil, ref in zip(dilations, dil_refs):
            for r in range(dil):
                for cc in range(tn // LANES):
                    ref[r, :, cc * LANES:(cc + 1) * LANES] = (
                        z_scr[cc, pl.ds(r, tm // dil, stride=dil), :].astype(ref.dtype))


def _inproj(x2, pos, sc1, sh1, w_in_bf, dilations):
    s, d = x2.shape
    n = w_in_bf.shape[1]
    tm = min(1024, s)
    tn = 512
    n_rope = (2 * ATTN_WIDTH) // tn
    n_qkv = (3 * ATTN_WIDTH) // tn
    bf = jnp.bfloat16
    dil_specs = [pl.BlockSpec((dil, tm // dil, tn), lambda i, j: (0, i, jnp.minimum(j, n_qkv - 1)))
                 for dil in dilations]
    dil_shapes = [jax.ShapeDtypeStruct((dil, s // dil, 3 * ATTN_WIDTH), bf) for dil in dilations]
    return pl.pallas_call(
        functools.partial(_inproj_kernel, n_rope=n_rope, dilations=tuple(dilations)),
        grid=(s // tm, n // tn),
        in_specs=[pl.BlockSpec((tm, d), lambda i, j: (i, 0)),
                  pl.BlockSpec((tm, 1), lambda i, j: (i, 0)),
                  pl.BlockSpec((1, d), lambda i, j: (0, 0)),
                  pl.BlockSpec((1, d), lambda i, j: (0, 0)),
                  pl.BlockSpec((d, tn), lambda i, j: (0, j))],
        out_specs=[pl.BlockSpec((tm, tn), lambda i, j: (i, j))] + dil_specs,
        out_shape=[jax.ShapeDtypeStruct((s, n), bf)] + dil_shapes,
        scratch_shapes=[pltpu.VMEM((tm, d), bf),
                        pltpu.VMEM((tm, LANES), jnp.float32),
                        pltpu.VMEM((tm, LANES), jnp.float32),
                        pltpu.VMEM((tn // LANES, tm, LANES), jnp.float32)],
        compiler_params=_params(("arbitrary", "arbitrary")),
        name="inproj",
    )(x2, pos, sc1, sh1, w_in_bf)


def _attn_kernel(q_ref, kc_ref, vc_ref, kp_ref, vp_ref, o_ref, lse_ref, kwin, vt_win, *, nblk):
    i = pl.program_id(1)
    blk = ATTN_BLOCK
    kwin[0:blk, :] = kp_ref[...]
    kwin[blk:, :] = kc_ref[...]
    vt_win[:, 0:blk] = vp_ref[...].T
    vt_win[:, blk:] = vc_ref[...].T

    key = lax.broadcasted_iota(jnp.int32, (2 * blk, blk), 0)
    qry = lax.broadcasted_iota(jnp.int32, (2 * blk, blk), 1)
    lane = lax.broadcasted_iota(jnp.int32, (blk, LANES), 1)
    row = lax.broadcasted_iota(jnp.int32, (blk, LANES), 0)
    nt = (((1,), (1,)), ((), ()))

    def body(j, carry):
        r0 = pl.multiple_of(j * blk, blk)
        first_key = jnp.where((i * nblk + j) > 0, 0, blk)
        mask = jnp.logical_and(key >= jnp.maximum(qry, first_key), key <= qry + blk)
        lse_rows = []
        for hp in range(ATTN_HEADS // 2):
            cs = slice(hp * LANES, (hp + 1) * LANES)
            q2 = q_ref[pl.ds(r0, blk), cs]
            k2 = kwin[pl.ds(r0, 2 * blk), cs]
            vt2 = vt_win[cs, pl.ds(r0, 2 * blk)]
            halves = []
            for hd in range(2):
                in_head = (lane < HEAD_DIM) if hd == 0 else (lane >= HEAD_DIM)
                qm = jnp.where(in_head, q2, jnp.zeros_like(q2))
                st = lax.dot_general(k2, qm, nt, preferred_element_type=jnp.float32)
                st = jnp.where(mask, st, NEG_INF)
                m = jnp.max(st, axis=0, keepdims=True)
                p = jnp.exp2(st - m)
                l = jnp.sum(p, axis=0, keepdims=True)
                ot = jnp.dot(vt2, p.astype(vt2.dtype), preferred_element_type=jnp.float32)
                halves.append(ot / l)
                lse_rows.append(m * math.log(2.0) + jnp.log(l))
            ot2 = jnp.where(row < HEAD_DIM, halves[0], halves[1])
            o_ref[pl.ds(r0, blk), cs] = ot2.T.astype(o_ref.dtype)
        lse_mat = jnp.zeros((LANES, blk), jnp.float32)
        for hidx, lse_h in enumerate(lse_rows):
            lse_mat = jnp.where(row == hidx, lse_h, lse_mat)
        lse_ref[pl.ds(r0, blk), :] = lse_mat.T
        return carry

    lax.fori_loop(0, nblk, body, 0)


def _attn_pattern(zd, dilation):
    _, length, zw = zd.shape
    aw = ATTN_WIDTH
    rows = min(1024, length)
    nblk = rows // ATTN_BLOCK
    cur = lambda off: pl.BlockSpec((None, rows, aw), lambda r, i: (r, i, off))
    prev = lambda off: pl.BlockSpec((None, ATTN_BLOCK, aw), lambda r, i: (r, jnp.maximum(i * nblk - 1, 0), off))
    return pl.pallas_call(
        functools.partial(_attn_kernel, nblk=nblk),
        grid=(dilation, length // rows),
        in_specs=[cur(0), cur(1), cur(2), prev(1), prev(2)],
        out_specs=[pl.BlockSpec((None, rows, aw), lambda r, i: (r, i, 0)),
                   pl.BlockSpec((None, rows, LANES), lambda r, i: (r, i, 0))],
        out_shape=[jax.ShapeDtypeStruct((dilation, length, aw), jnp.bfloat16),
                   jax.ShapeDtypeStruct((dilation, length, LANES), jnp.float32)],
        scratch_shapes=[pltpu.VMEM((rows + ATTN_BLOCK, aw), jnp.bfloat16),
                        pltpu.VMEM((aw, rows + ATTN_BLOCK), jnp.bfloat16)],
        compiler_params=_params(("arbitrary", "arbitrary")),
        name=f"attn_d{dilation}",
    )(zd, zd, zd, zd, zd)


def _attention(z, z_dil):
    outs = []
    for (window, dilation), zd in zip(DILATED_PATTERNS, z_dil):
        assert window // dilation == ATTN_BLOCK
        outs.append(_attn_pattern(zd, dilation))
    return outs


def _s5_param_kernel(lr_r, li_r, ld_r, bre_ref, bim_ref, cre_ref, cim_ref,
                     krev_ref, pblk_ref, qblk_ref, at_ref):
    t_chunk = S5_CHUNK
    nst = STATE_PER_BLOCK
    lr, li, ld = lr_r[...], li_r[...], ld_r[...]
    dt = jnp.exp(ld)
    mag = jnp.exp(lr * dt)
    ar, ai = mag * jnp.cos(li * dt), mag * jnp.sin(li * dt)
    den = lr * lr + li * li
    cr = ((ar - 1.0) * lr + ai * li) / den
    ci = (ai * lr - (ar - 1.0) * li) / den
    bre, bim = bre_ref[...], bim_ref[...]
    bbr = cr * bre - ci * bim
    bbi = cr * bim + ci * bre
    cre, cim = cre_ref[...], cim_ref[...]
    cre_b, cim_b = cre.astype(pblk_ref.dtype), cim.astype(pblk_ref.dtype)
    nt = (((1,), (1,)), ((), ()))
    er, ei = jnp.ones_like(ar), jnp.zeros_like(ai)
    for tau in range(t_chunk):
        pr = er * bbr - ei * bbi
        pi = er * bbi + ei * bbr
        blk = t_chunk - 1 - tau
        rows = slice(blk * LANES, (blk + 1) * LANES)
        prb, pib = pr.astype(pblk_ref.dtype), pi.astype(pblk_ref.dtype)
        pblk_ref[rows, 0:nst] = prb
        pblk_ref[rows, nst:2 * nst] = pib
        k_tau = (lax.dot_general(prb, cre_b, nt, preferred_element_type=jnp.float32)
                 - lax.dot_general(pib, cim_b, nt, preferred_element_type=jnp.float32))
        krev_ref[rows, :] = k_tau.astype(krev_ref.dtype)
        er, ei = er * ar - ei * ai, er * ai + ei * ar
        qblk_ref[tau, :, 0:nst] = (cre * er - cim * ei).astype(qblk_ref.dtype)
        qblk_ref[tau, :, nst:2 * nst] = (-(cre * ei + cim * er)).astype(qblk_ref.dtype)
    at_ref[:, 0:nst] = er
    at_ref[:, nst:2 * nst] = ei


def _s5_params(lam_re, lam_im, log_dt, b_re, b_im, c_re, c_im):
    g, n = lam_re.shape
    p = b_re.shape[-1]
    gb = GROUPS_PER_BLOCK
    nlb = g // gb
    nst = gb * n
    f32 = jnp.float32
    eye = jnp.eye(gb, dtype=f32)

    def row(a):
        return a.astype(f32).reshape(nlb, 1, nst)

    ldt = jnp.broadcast_to(log_dt.astype(f32)[:, None], (g, n))

    def b_blockdiag(b):
        b4 = b.astype(f32).reshape(nlb, gb, n, p)
        return jnp.einsum("lhnq,gh->lgqhn", b4, eye).reshape(nlb, gb * p, nst)

    def c_blockdiag(c):
        c4 = c.astype(f32).reshape(nlb, gb, p, n)
        return jnp.einsum("lhpn,gh->lgphn", c4, eye).reshape(nlb, gb * p, nst)

    tc = S5_CHUNK
    vec_r = pl.BlockSpec((None, 1, nst), lambda l: (l, 0, 0))
    mat = pl.BlockSpec((None, LANES, nst), lambda l: (l, 0, 0))
    return pl.pallas_call(
        _s5_param_kernel,
        grid=(nlb,),
        in_specs=[vec_r, vec_r, vec_r, mat, mat, mat, mat],
        out_specs=[pl.BlockSpec((None, tc * LANES, LANES), lambda l: (l, 0, 0)),
                   pl.BlockSpec((None, tc * LANES, 2 * nst), lambda l: (l, 0, 0)),
                   pl.BlockSpec((None, tc, LANES, 2 * nst), lambda l: (l, 0, 0, 0)),
                   pl.BlockSpec((None, 1, 2 * nst), lambda l: (l, 0, 0))],
        out_shape=[jax.ShapeDtypeStruct((nlb, tc * LANES, LANES), jnp.bfloat16),
                   jax.ShapeDtypeStruct((nlb, tc * LANES, 2 * nst), jnp.bfloat16),
                   jax.ShapeDtypeStruct((nlb, tc, LANES, 2 * nst), jnp.bfloat16),
                   jax.ShapeDtypeStruct((nlb, 1, 2 * nst), f32)],
        compiler_params=_params(("arbitrary",)),
        name="s5_params",
    )(row(lam_re), row(lam_im), row(ldt),
      b_blockdiag(b_re), b_blockdiag(b_im), c_blockdiag(c_re), c_blockdiag(c_im))


def _load_ucat(u_ref, uf_scr, ucat_scr, nc):
    uf_scr[...] = u_ref[...].astype(jnp.float32)
    for s in range(S5_CHUNK):
        ucat_scr[:, s * LANES:(s + 1) * LANES] = (
            uf_scr[pl.ds(s, nc, stride=S5_CHUNK), :].astype(ucat_scr.dtype))


def _s5_state_in_kernel(u_ref, pblk_ref, b_ref, uf_scr, ucat_scr, *, nc):
    _load_ucat(u_ref, uf_scr, ucat_scr, nc)
    b_ref[...] = jnp.dot(ucat_scr[...], pblk_ref[...], preferred_element_type=jnp.float32)


def _s5_scan_kernel(b_ref, at_ref, h_ref, *, nc, nlb):
    nst = STATE_PER_BLOCK
    a_re = at_ref[:, 0:nst]
    a_im = at_ref[:, nst:2 * nst]

    def body(c, carry):
        hr, hi = carry
        h_ref[c, :, 0:nst] = hr
        h_ref[c, :, nst:2 * nst] = hi
        bc = b_ref[c]
        return (a_re * hr - a_im * hi + bc[:, 0:nst], a_re * hi + a_im * hr + bc[:, nst:2 * nst])

    zero = jnp.zeros((nlb, nst), jnp.float32)
    lax.fori_loop(0, nc, body, (zero, zero))


def _s5_out_kernel(u_ref, h_ref, krev_ref, qblk_ref, dsk_ref, y_ref, uf_scr, ucat_scr, y_scr, *, nc):
    _load_ucat(u_ref, uf_scr, ucat_scr, nc)
    hb = h_ref[...].astype(jnp.bfloat16)
    dsk = dsk_ref[...]
    tc = S5_CHUNK
    for t in range(tc):
        kd = (t + 1) * LANES
        y = jnp.dot(ucat_scr[:, 0:kd], krev_ref[(tc - 1 - t) * LANES:, :],
                    preferred_element_type=jnp.float32)
        y = y + lax.dot_general(hb, qblk_ref[t], (((1,), (1,)), ((), ())), preferred_element_type=jnp.float32)
        y = y + dsk * uf_scr[pl.ds(t, nc, stride=tc), :]
        y = 0.5 * y * (1.0 + lax.erf(y * (2.0 ** -0.5)))
        y_scr[pl.ds(t, nc, stride=tc), :] = y
    y_ref[...] = y_scr[...].astype(y_ref.dtype)


def _s5(z, lam_re, lam_im, log_dt, b_re, b_im, c_re, c_im, d_skip):
    s = z.shape[0]
    nlb = lam_re.shape[0] // GROUPS_PER_BLOCK
    nc = s // S5_CHUNK
    nst2 = 2 * STATE_PER_BLOCK
    u_col0 = (3 * ATTN_WIDTH) // LANES
    krev, pblk, qblk, at = _s5_params(lam_re, lam_im, log_dt, b_re, b_im, c_re, c_im)
    u_spec = pl.BlockSpec((s, LANES), lambda l: (0, u_col0 + l))
    b2 = pl.pallas_call(
        functools.partial(_s5_state_in_kernel, nc=nc),
        grid=(nlb,),
        in_specs=[u_spec, pl.BlockSpec((None, S5_CHUNK * LANES, nst2), lambda l: (l, 0, 0))],
        out_specs=pl.BlockSpec((nc, nst2), lambda l: (0, l)),
        out_shape=jax.ShapeDtypeStruct((nc, nlb * nst2), jnp.float32),
        scratch_shapes=[pltpu.VMEM((s, LANES), jnp.float32),
                        pltpu.VMEM((nc, S5_CHUNK * LANES), jnp.bfloat16)],
        compiler_params=_params(("arbitrary",)),
        name="s5_state_in",
    )(z, pblk)
    h3 = pl.pallas_call(
        functools.partial(_s5_scan_kernel, nc=nc, nlb=nlb),
        out_shape=jax.ShapeDtypeStruct((nc, nlb, nst2), jnp.float32),
        compiler_params=pltpu.CompilerParams(vmem_limit_bytes=VMEM_LIMIT_BYTES),
        name="s5_scan",
    )(b2.reshape(nc, nlb, nst2), at.reshape(nlb, nst2))
    return pl.pallas_call(
        functools.partial(_s5_out_kernel, nc=nc),
        grid=(nlb,),
        in_specs=[u_spec,
                  pl.BlockSpec((nc, nst2), lambda l: (0, l)),
                  pl.BlockSpec((None, S5_CHUNK * LANES, LANES), lambda l: (l, 0, 0)),
                  pl.BlockSpec((None, S5_CHUNK, LANES, nst2), lambda l: (l, 0, 0, 0)),
                  pl.BlockSpec((1, LANES), lambda l: (0, l))],
        out_specs=pl.BlockSpec((s, LANES), lambda l: (0, l)),
        out_shape=jax.ShapeDtypeStruct((s, nlb * LANES), jnp.bfloat16),
        scratch_shapes=[pltpu.VMEM((s, LANES), jnp.float32),
                        pltpu.VMEM((nc, S5_CHUNK * LANES), jnp.bfloat16),
                        pltpu.VMEM((s, LANES), jnp.float32)],
        compiler_params=_params(("arbitrary",)),
        name="s5_out",
    )(z, h3.reshape(nc, nlb * nst2), krev, qblk, d_skip.astype(jnp.float32).reshape(1, -1))


def _mixout_kernel(*refs, alpha, n_pat):
    o_refs = refs[:n_pat]
    lse_refs = refs[n_pat:2 * n_pat]
    (yg_ref, x_ref, wglu_ref, bglu_ref, ga_ref, gs_ref, wout_ref,
     gt_ref, lg_ref, lb_ref, sc_ref, sh_ref, x1_ref, h2_ref, o_scr, lse_scr) = refs[2 * n_pat:]
    planes, tm, _ = o_scr.shape
    aw = planes * LANES

    def token_major(ref, scr):
        dil = ref.shape[0]
        if dil == 1:
            return ref[0].astype(jnp.float32)
        n_planes = ref.shape[2] // LANES
        for r in range(dil):
            for cc in range(n_planes):
                scr[cc, pl.ds(r, tm // dil, stride=dil), :] = (
                    ref[r, :, cc * LANES:(cc + 1) * LANES].astype(jnp.float32))
        return jnp.concatenate([scr[cc] for cc in range(n_planes)], axis=1)

    lses = [token_major(ref, lse_scr) for ref in lse_refs]
    top = functools.reduce(jnp.maximum, lses)
    ws = [jnp.exp(v - top) for v in lses]
    inv = 1.0 / functools.reduce(lambda u, v: u + v, ws)
    head_of_lane = lax.broadcasted_iota(jnp.int32, (LANES, aw), 1) // HEAD_DIM
    spread = (lax.broadcasted_iota(jnp.int32, (LANES, aw), 0) == head_of_lane).astype(jnp.bfloat16)
    spread2 = jnp.concatenate([spread, spread], axis=0)
    a = jnp.zeros((tm, aw), jnp.float32)
    for w, o_ref in zip(ws, o_refs):
        w = w * inv
        w_hi = w.astype(jnp.bfloat16)
        w_lo = (w - w_hi.astype(jnp.float32)).astype(jnp.bfloat16)
        wide = jnp.dot(jnp.concatenate([w_hi, w_lo], axis=1), spread2, preferred_element_type=jnp.float32)
        a = a + wide * token_major(o_ref, o_scr)
    ra = a * lax.rsqrt(jnp.mean(a * a, axis=-1, keepdims=True) + LN_EPS) * ga_ref[...]
    yg = yg_ref[...]
    y = yg.astype(jnp.float32)
    gate = jax.nn.sigmoid(jnp.dot(yg, wglu_ref[...], preferred_element_type=jnp.float32) + bglu_ref[...])
    y = y * gate
    ry = y * lax.rsqrt(jnp.mean(y * y, axis=-1, keepdims=True) + LN_EPS) * gs_ref[...]
    mix = (jnp.dot(ra.astype(jnp.bfloat16), wout_ref[0:aw, :], preferred_element_type=jnp.float32)
           + jnp.dot(ry.astype(jnp.bfloat16), wout_ref[aw:, :], preferred_element_type=jnp.float32))
    t = alpha * x_ref[...] + (1.0 + gt_ref[...]) * mix
    x1 = _ln_rows(t) * lg_ref[...] + lb_ref[...]
    x1_ref[...] = x1
    h2_ref[...] = (_ln_rows(x1) * (1.0 + sc_ref[...]) + sh_ref[...]).astype(h2_ref.dtype)


def _mixout(attn_parts, yg, x2, w_glu_bf, b_glu, g_attn, g_ssm, w_out_bf, gt1, ln_g, ln_b, sc2, sh2, alpha):
    s, d = x2.shape
    aw = attn_parts[0][0].shape[-1]
    sw = yg.shape[1]
    n_pat = len(attn_parts)
    tm = min(512, s)
    row = lambda i: (i, 0)
    fix = lambda i: (0, 0)
    vec = lambda n: pl.BlockSpec((1, n), fix)
    res = lambda a: pl.BlockSpec((a.shape[0], tm // a.shape[0], a.shape[2]), lambda i: (0, i, 0))
    return pl.pallas_call(
        functools.partial(_mixout_kernel, alpha=alpha, n_pat=n_pat),
        grid=(s // tm,),
        in_specs=([res(o) for o, _ in attn_parts] + [res(v) for _, v in attn_parts]
                  + [pl.BlockSpec((tm, sw), row), pl.BlockSpec((tm, d), row),
                     pl.BlockSpec((sw, sw), fix), vec(sw), vec(aw), vec(sw),
                     pl.BlockSpec((aw + sw, d), fix), vec(d), vec(d), vec(d), vec(d), vec(d)]),
        out_specs=[pl.BlockSpec((tm, d), row), pl.BlockSpec((tm, d), row)],
        out_shape=[jax.ShapeDtypeStruct((s, d), jnp.float32), jax.ShapeDtypeStruct((s, d), jnp.bfloat16)],
        scratch_shapes=[pltpu.VMEM((aw // LANES, tm, LANES), jnp.float32),
                        pltpu.VMEM((1, tm, LANES), jnp.float32)],
        compiler_params=_params(("arbitrary",)),
        name="mixout",
    )(*[o for o, _ in attn_parts], *[l for _, l in attn_parts], yg, x2, w_glu_bf, b_glu.reshape(1, -1),
      g_attn.reshape(1, -1), g_ssm.reshape(1, -1), w_out_bf, gt1, ln_g.reshape(1, -1), ln_b.reshape(1, -1),
      sc2, sh2)


def _top_values(sc, k, want_rank=False):
    vals = []
    rank = jnp.full(sc.shape, float(k), jnp.float32) if want_rank else None
    for j in range(k):
        m = jnp.max(sc, axis=0, keepdims=True)
        vals.append(m)
        hit = sc == m
        if want_rank:
            rank = jnp.where(hit, float(j), rank)
        sc = jnp.where(hit, BIG_NEG, sc)
    return vals, rank


def _peerq_kernel(h2_ref, wqp_ref, keys_ref, c0_ref, r1_ref, e0_ref, e1_ref, sc_scr):
    k = PEER_TOPK
    tm = h2_ref.shape[0]
    qp = jnp.dot(h2_ref[...], wqp_ref[...], preferred_element_type=jnp.float32).astype(jnp.bfloat16)
    nt = (((1,), (1,)), ((), ()))
    for hi in range(2 * PEER_HEADS):
        col = hi * PEER_SUB_DIM
        sc_scr[hi] = lax.dot_general(keys_ref[hi % 2], qp[:, col:col + PEER_SUB_DIM], nt,
                                     preferred_element_type=jnp.float32)
    def body(h, carry):
        s0 = sc_scr[2 * h]
        s1 = sc_scr[2 * h + 1]
        top0, _ = _top_values(s0, k)
        top1, rank1 = _top_values(s1, k, want_rank=True)
        cands = [top0[i] + top1[j] for i in range(k) for j in range(k) if (i + 1) * (j + 1) <= k]
        pad = (-len(cands)) % SUBLANES
        cmat = jnp.concatenate(cands + [jnp.full_like(cands[0], BIG_NEG)] * pad, axis=0)
        best, _ = _top_values(cmat, k)
        thr = best[k - 1]
        m0, m1 = top0[0], top1[0]
        zsum = jnp.sum(jnp.where(cmat >= thr, jnp.exp(cmat - (m0 + m1)), 0.0), axis=0, keepdims=True)
        count0 = jnp.zeros_like(s0)
        for j in range(k):
            count0 = jnp.where(s0 + top1[j] >= thr, float(j + 1), count0)
        c0_ref[h] = count0
        r1_ref[h] = rank1.astype(r1_ref.dtype)
        e0_ref[h] = jnp.exp(s0 - m0) / zsum
        e1_ref[h] = jnp.exp(s1 - m1).astype(e1_ref.dtype)
        return carry

    lax.fori_loop(0, PEER_HEADS, body, 0)


def _peerq(h2, w_qp_bf, keys_bf):
    s, d = h2.shape
    tm = min(512, s)
    nk = keys_bf.shape[1]
    stat = pl.BlockSpec((PEER_HEADS, nk, tm), lambda i: (0, 0, i))
    words = jax.ShapeDtypeStruct((PEER_HEADS, nk, s), jnp.float32)
    halfs = jax.ShapeDtypeStruct((PEER_HEADS, nk, s), jnp.bfloat16)
    return pl.pallas_call(
        _peerq_kernel,
        grid=(s // tm,),
        in_specs=[pl.BlockSpec((tm, d), lambda i: (i, 0)),
                  pl.BlockSpec(w_qp_bf.shape, lambda i: (0, 0)),
                  pl.BlockSpec(keys_bf.shape, lambda i: (0, 0, 0))],
        out_specs=[stat, stat, stat, stat],
        out_shape=[words, halfs, words, halfs],
        scratch_shapes=[pltpu.VMEM((2 * PEER_HEADS, nk, tm), jnp.float32)],
        compiler_params=_params(("arbitrary",)),
        name="peerq",
    )(h2, w_qp_bf, keys_bf)


def _peer_kernel(ua_ref, ub_ref, vt_ref, h2t_ref, c0_ref, e0_ref, r1_ref, e1_ref, o_ref,
                 a_s0, a_s1, w_s0, w_s1, *, n_pairs, lane_chunk):
    m = pl.program_id(1)
    nk = PEER_N_KEYS
    te, tm = a_s0.shape
    rows_per_tile = te // nk
    packed = 2 * SUBLANES
    bf = jnp.bfloat16

    def stage_a(u_ref, dst):
        dst[...] = jnp.dot(u_ref[...], h2t_ref[...], preferred_element_type=jnp.float32)

    def row_tile(ref, h, i0, ls):
        return jnp.broadcast_to(ref[h, i0:i0 + 1, ls], (packed, lane_chunk)).astype(bf)

    def stage_b(src, dst, row_off, ls):
        for rr in range(rows_per_tile):
            rows = slice(rr * nk, (rr + 1) * nk)
            i0 = row_off + rr
            a = src[rows, ls]
            act = 0.5 * a * (1.0 + lax.erf(a * (2.0 ** -0.5)))
            gate = jnp.zeros((nk // packed, packed, lane_chunk), bf)
            for h in range(PEER_HEADS):
                c0 = row_tile(c0_ref, h, i0, ls)
                e0 = row_tile(e0_ref, h, i0, ls)
                r1 = r1_ref[h, :, ls].reshape(nk // packed, packed, lane_chunk)
                e1 = e1_ref[h, :, ls].reshape(nk // packed, packed, lane_chunk)
                gate = gate + jnp.where(r1 < c0, e0 * e1, jnp.zeros_like(e1))
            w = gate * act.astype(bf).reshape(nk // packed, packed, lane_chunk)
            dst[rows, ls] = w.reshape(nk, lane_chunk)

    def tile(src, dst, half, row_off):
        for c in range(tm // lane_chunk):
            ls = slice(c * lane_chunk, (c + 1) * lane_chunk)
            stage_b(src, dst, row_off, ls)
            o_ref[:, ls] += jnp.dot(vt_ref[:, half * te:(half + 1) * te], dst[:, ls],
                                    preferred_element_type=jnp.float32)

    @pl.when(m == 0)
    def _():
        o_ref[...] = jnp.zeros_like(o_ref)
        stage_a(ub_ref, a_s0)

    @pl.when(jnp.logical_and(m > 0, m < n_pairs))
    def _():
        stage_a(ua_ref, a_s1)
        tile(a_s0, w_s0, 0, 0)
        stage_a(ub_ref, a_s0)
        tile(a_s1, w_s1, 1, rows_per_tile)

    @pl.when(m == n_pairs)
    def _():
        stage_a(ua_ref, a_s1)
        tile(a_s0, w_s0, 0, 0)
        tile(a_s1, w_s1, 1, rows_per_tile)


def _peer(h2t, u_bf, vt_bf, a0, n1, e0, e1):
    d, s = h2t.shape
    ne = u_bf.shape[0]
    nk = PEER_N_KEYS
    tm = min(512, s)
    te = (SUBLANES // 2) * nk
    n_tiles = ne // te
    n_pairs = n_tiles // 2
    rows = pl.BlockSpec((PEER_HEADS, SUBLANES, tm), lambda i, m: (0, jnp.maximum(m - 1, 0), i))
    full_stat = pl.BlockSpec((PEER_HEADS, nk, tm), lambda i, m: (0, 0, i))
    return pl.pallas_call(
        functools.partial(_peer_kernel, n_pairs=n_pairs, lane_chunk=min(256, tm)),
        grid=(s // tm, n_pairs + 1),
        in_specs=[pl.BlockSpec((te, d), lambda i, m: (jnp.maximum(2 * m - 1, 0), 0)),
                  pl.BlockSpec((te, d), lambda i, m: (jnp.minimum(2 * m, n_tiles - 1), 0)),
                  pl.BlockSpec((d, 2 * te), lambda i, m: (0, jnp.maximum(m - 1, 0))),
                  pl.BlockSpec((d, tm), lambda i, m: (0, i)),
                  rows, rows, full_stat, full_stat],
        out_specs=pl.BlockSpec((d, tm), lambda i, m: (0, i)),
        out_shape=jax.ShapeDtypeStruct((d, s), jnp.float32),
        scratch_shapes=[pltpu.VMEM((te, tm), jnp.float32), pltpu.VMEM((te, tm), jnp.float32),
                        pltpu.VMEM((te, tm), jnp.bfloat16), pltpu.VMEM((te, tm), jnp.bfloat16)],
        compiler_params=_params(("arbitrary", "arbitrary")),
        name="peer",
    )(u_bf, u_bf, vt_bf, h2t, a0, e0, n1, e1)


def _final_kernel(ft_ref, x1_ref, gt_ref, lg_ref, lb_ref, o_ref, *, alpha):
    ffn = ft_ref[...].T
    t = alpha * x1_ref[...] + (1.0 + gt_ref[...]) * ffn
    o_ref[...] = _ln_rows(t) * lg_ref[...] + lb_ref[...]


def _final(ffn_t, x1, gt2, ln_g, ln_b, alpha):
    s, d = x1.shape
    tm = min(512, s)
    vec = pl.BlockSpec((1, d), lambda i: (0, 0))
    return pl.pallas_call(
        functools.partial(_final_kernel, alpha=alpha),
        grid=(s // tm,),
        in_specs=[pl.BlockSpec((d, tm), lambda i: (0, i)), pl.BlockSpec((tm, d), lambda i: (i, 0)),
                  vec, vec, vec],
        out_specs=pl.BlockSpec((tm, d), lambda i: (i, 0)),
        out_shape=jax.ShapeDtypeStruct((s, d), jnp.float32),
        compiler_params=_params(("arbitrary",)),
        name="final",
    )(ffn_t, x1, gt2, ln_g.reshape(1, -1), ln_b.reshape(1, -1))


def kernel(x, c, positions, w_ada, b_ada, w_in, lam_re, lam_im, log_dt, ssm_b_re, ssm_b_im, ssm_c_re,
           ssm_c_im, ssm_d, w_glu, b_glu, g_attn, g_ssm, w_out, ln1_g, ln1_b, w_qp, sub_keys, expert_u,
           expert_v, ln2_g, ln2_b):
    b, s, d = x.shape
    assert b == 1, "one sequence per call"
    depth = w_ada.shape[0]
    alpha = (2.0 * depth) ** 0.25
    bf = jnp.bfloat16
    x2 = x.reshape(s, d)
    pos = positions.reshape(s, 1)
    dilations = [dil for _, dil in DILATED_PATTERNS if dil > 1]
    for l in range(depth):
        mod = _ada(c, w_ada[l], b_ada[l])
        sh1, sc1, gt1, sh2, sc2, gt2 = [mod[:, k * d:(k + 1) * d] for k in range(ADA_CHUNKS)]
        z, *z_res = _inproj(x2, pos, sc1, sh1, w_in[l].astype(bf), dilations)
        z_dil = [z.reshape(1, s, -1) if dil == 1 else z_res[dilations.index(dil)]
                 for _, dil in DILATED_PATTERNS]
        attn = _attention(z, z_dil)
        yg = _s5(z, lam_re[l], lam_im[l], log_dt[l], ssm_b_re[l], ssm_b_im[l],
                 ssm_c_re[l], ssm_c_im[l], ssm_d[l])
        x1, h2 = _mixout(attn, yg, x2, w_glu[l].astype(bf), b_glu[l], g_attn[l], g_ssm[l],
                         w_out[l].astype(bf), gt1, ln1_g[l], ln1_b[l], sc2, sh2, alpha)
        a0, n1, e0, e1 = _peerq(h2, w_qp[l].astype(bf), sub_keys[l].astype(bf))
        ffn_t = _peer(h2.T, expert_u[l].astype(bf), expert_v[l].T.astype(bf), a0, n1, e0, e1)
        x2 = _final(ffn_t, x1, gt2, ln2_g[l], ln2_b[l], alpha)
    return x2.reshape(b, s, d)
```

```python
import functools
import math

import jax
import jax.numpy as jnp
from jax import lax
from jax.experimental import pallas as pl
from jax.experimental.pallas import tpu as pltpu

ATTN_HEADS = 16
HEAD_DIM = 64
ATTN_WIDTH = ATTN_HEADS * HEAD_DIM
SSM_GROUP = 16
SSM_STATE = 64
DILATED_PATTERNS = ((128, 1), (512, 4), (2048, 16))
ATTN_BLOCK = 128
ROPE_THETA = 10000.0
NEG_INF = -1e30
PEER_HEADS = 8
PEER_SUB_DIM = 128
PEER_N_KEYS = 128
PEER_TOPK = 16
LN_EPS = 1e-5
ADA_CHUNKS = 6

LANES = 128
SUBLANES = 8
VMEM_LIMIT_BYTES = 56 * 1024 * 1024

S5_CHUNK = 16
GROUPS_PER_BLOCK = LANES // SSM_GROUP
STATE_PER_BLOCK = GROUPS_PER_BLOCK * SSM_STATE
BIG_NEG = -3.0e38
QK_LOG2_SCALE = HEAD_DIM ** -0.5 * math.log2(math.e)


def _params(sem):
    return pltpu.CompilerParams(dimension_semantics=sem, vmem_limit_bytes=VMEM_LIMIT_BYTES)


def _ln_rows(t):
    mu = jnp.mean(t, axis=-1, keepdims=True)
    d = t - mu
    var = jnp.mean(d * d, axis=-1, keepdims=True)
    return d * lax.rsqrt(var + LN_EPS)


def _ada_kernel(c_ref, w_ref, b_ref, o_ref):
    c = c_ref[...]
    sc = c * jax.nn.sigmoid(c)
    o_ref[...] = jnp.sum(sc * w_ref[...], axis=0, keepdims=True) + b_ref[...]


def _ada(c, w_ada, b_ada):
    d = c.shape[-1]
    n = w_ada.shape[-1]
    tn = n // 8
    return pl.pallas_call(
        _ada_kernel,
        grid=(n // tn,),
        in_specs=[pl.BlockSpec((d, 1), lambda j: (0, 0)),
                  pl.BlockSpec((d, tn), lambda j: (0, j)),
                  pl.BlockSpec((1, tn), lambda j: (0, j))],
        out_specs=pl.BlockSpec((1, tn), lambda j: (0, j)),
        out_shape=jax.ShapeDtypeStruct((1, n), jnp.float32),
        compiler_params=_params(("arbitrary",)),
        name="ada",
    )(c.reshape(d, 1), w_ada, b_ada.reshape(1, n))


def _inproj_kernel(x_ref, pos_ref, sc_ref, sh_ref, w_ref, z_ref, *rest, n_rope, dilations):
    dil_refs = rest[:len(dilations)]
    h_scr, cos_scr, sin_scr, z_scr = rest[len(dilations):]
    j = pl.program_id(1)
    tm, tn = z_ref.shape

    @pl.when(j == 0)
    def _():
        h = _ln_rows(x_ref[...]) * (1.0 + sc_ref[...]) + sh_ref[...]
        h_scr[...] = h.astype(jnp.bfloat16)
        lane = lax.broadcasted_iota(jnp.int32, (1, LANES), 1)
        fi = ((lane % HEAD_DIM) % (HEAD_DIM // 2)).astype(jnp.float32)
        inv_freq = jnp.exp(fi * (-math.log(ROPE_THETA) / (HEAD_DIM // 2)))
        ang = pos_ref[...].astype(jnp.float32) * inv_freq
        first_half = (lane % HEAD_DIM) < (HEAD_DIM // 2)
        cos_scr[...] = jnp.cos(ang)
        sin_scr[...] = jnp.where(first_half, -jnp.sin(ang), jnp.sin(ang))

    z = jnp.dot(h_scr[...], w_ref[...], preferred_element_type=jnp.float32)

    @pl.when(j < n_rope)
    def _():
        lane = lax.broadcasted_iota(jnp.int32, (1, LANES), 1)
        first_half = (lane % HEAD_DIM) < (HEAD_DIM // 2)
        q_mult = jnp.where(j < n_rope // 2, QK_LOG2_SCALE, 1.0)
        cos = cos_scr[...] * q_mult
        sin = sin_scr[...] * q_mult
        for cc in range(tn // LANES):
            zc = z[:, cc * LANES:(cc + 1) * LANES]
            partner = jnp.where(first_half, pltpu.roll(zc, LANES - HEAD_DIM // 2, 1),
                                pltpu.roll(zc, HEAD_DIM // 2, 1))
            z_scr[cc] = zc * cos + partner * sin

    @pl.when(j >= n_rope)
    def _():
        for cc in range(tn // LANES):
            z_scr[cc] = z[:, cc * LANES:(cc + 1) * LANES]

    for cc in range(tn // LANES):
        z_ref[:, cc * LANES:(cc + 1) * LANES] = z_scr[cc].astype(z_ref.dtype)

    @pl.when(j < (3 * ATTN_WIDTH) // tn)
    def _():
        for dil, ref in zip(dilations, dil_refs):
            for r in range(dil):
                for cc in range(tn // LANES):
                    ref[r, :, cc * LANES:(cc + 1) * LANES] = (
                        z_scr[cc, pl.ds(r, tm // dil, stride=dil), :].astype(ref.dtype))


def _inproj(x2, pos, sc1, sh1, w_in_bf, dilations):
    s, d = x2.shape
    n = w_in_bf.shape[1]
    tm = min(1024, s)
    tn = 512
    n_rope = (2 * ATTN_WIDTH) // tn
    n_qkv = (3 * ATTN_WIDTH) // tn
    bf = jnp.bfloat16
    dil_specs = [pl.BlockSpec((dil, tm // dil, tn), lambda i, j: (0, i, jnp.minimum(j, n_qkv - 1)))
                 for dil in dilations]
    dil_shapes = [jax.ShapeDtypeStruct((dil, s // dil, 3 * ATTN_WIDTH), bf) for dil in dilations]
    return pl.pallas_call(
        functools.partial(_inproj_kernel, n_rope=n_rope, dilations=tuple(dilations)),
        grid=(s // tm, n // tn),
        in_specs=[pl.BlockSpec((tm, d), lambda i, j: (i, 0)),
                  pl.BlockSpec((tm, 1), lambda i, j: (i, 0)),
                  pl.BlockSpec((1, d), lambda i, j: (0, 0)),
                  pl.BlockSpec((1, d), lambda i, j: (0, 0)),
                  pl.BlockSpec((d, tn), lambda i, j: (0, j))],
        out_specs=[pl.BlockSpec((tm, tn), lambda i, j: (i, j))] + dil_specs,
        out_shape=[jax.ShapeDtypeStruct((s, n), bf)] + dil_shapes,
        scratch_shapes=[pltpu.VMEM((tm, d), bf),
                        pltpu.VMEM((tm, LANES), jnp.float32),
                        pltpu.VMEM((tm, LANES), jnp.float32),
                        pltpu.VMEM((tn // LANES, tm, LANES), jnp.float32)],
        compiler_params=_params(("arbitrary", "arbitrary")),
        name="inproj",
    )(x2, pos, sc1, sh1, w_in_bf)


def _attn_kernel(q_ref, kc_ref, vc_ref, kp_ref, vp_ref, o_ref, lse_ref, kwin, vt_win, *, nblk):
    i = pl.program_id(1)
    blk = ATTN_BLOCK
    kwin[0:blk, :] = kp_ref[...]
    kwin[blk:, :] = kc_ref[...]
    vt_win[:, 0:blk] = vp_ref[...].T
    vt_win[:, blk:] = vc_ref[...].T

    key = lax.broadcasted_iota(jnp.int32, (2 * blk, blk), 0)
    qry = lax.broadcasted_iota(jnp.int32, (2 * blk, blk), 1)
    lane = lax.broadcasted_iota(jnp.int32, (blk, LANES), 1)
    row = lax.broadcasted_iota(jnp.int32, (blk, LANES), 0)
    nt = (((1,), (1,)), ((), ()))

    def body(j, carry):
        r0 = pl.multiple_of(j * blk, blk)
        first_key = jnp.where((i * nblk + j) > 0, 0, blk)
        mask = jnp.logical_and(key >= jnp.maximum(qry, first_key), key <= qry + blk)
        lse_rows = []
        for hp in range(ATTN_HEADS // 2):
            cs = slice(hp * LANES, (hp + 1) * LANES)
            q2 = q_ref[pl.ds(r0, blk), cs]
            k2 = kwin[pl.ds(r0, 2 * blk), cs]
            vt2 = vt_win[cs, pl.ds(r0, 2 * blk)]
            halves = []
            for hd in range(2):
                in_head = (lane < HEAD_DIM) if hd == 0 else (lane >= HEAD_DIM)
                qm = jnp.where(in_head, q2, jnp.zeros_like(q2))
                st = lax.dot_general(k2, qm, nt, preferred_element_type=jnp.float32)
                st = jnp.where(mask, st, NEG_INF)
                m = jnp.max(st, axis=0, keepdims=True)
                p = jnp.exp2(st - m)
                l = jnp.sum(p, axis=0, keepdims=True)
                ot = jnp.dot(vt2, p.astype(vt2.dtype), preferred_element_type=jnp.float32)
                halves.append(ot / l)
                lse_rows.append(m * math.log(2.0) + jnp.log(l))
            ot2 = jnp.where(row < HEAD_DIM, halves[0], halves[1])
            o_ref[pl.ds(r0, blk), cs] = ot2.T.astype(o_ref.dtype)
        lse_mat = jnp.zeros((LANES, blk), jnp.float32)
        for hidx, lse_h in enumerate(lse_rows):
            lse_mat = jnp.where(row == hidx, lse_h, lse_mat)
        lse_ref[pl.ds(r0, blk), :] = lse_mat.T
        return carry

    lax.fori_loop(0, nblk, body, 0)


def _attn_pattern(zd, dilation):
    _, length, zw = zd.shape
    aw = ATTN_WIDTH
    rows = min(1024, length)
    nblk = rows // ATTN_BLOCK
    cur = lambda off: pl.BlockSpec((None, rows, aw), lambda r, i: (r, i, off))
    prev = lambda off: pl.BlockSpec((None, ATTN_BLOCK, aw), lambda r, i: (r, jnp.maximum(i * nblk - 1, 0), off))
    return pl.pallas_call(
        functools.partial(_attn_kernel, nblk=nblk),
        grid=(dilation, length // rows),
        in_specs=[cur(0), cur(1), cur(2), prev(1), prev(2)],
        out_specs=[pl.BlockSpec((None, rows, aw), lambda r, i: (r, i, 0)),
                   pl.BlockSpec((None, rows, LANES), lambda r, i: (r, i, 0))],
        out_shape=[jax.ShapeDtypeStruct((dilation, length, aw), jnp.bfloat16),
                   jax.ShapeDtypeStruct((dilation, length, LANES), jnp.float32)],
        scratch_shapes=[pltpu.VMEM((rows + ATTN_BLOCK, aw), jnp.bfloat16),
                        pltpu.VMEM((aw, rows + ATTN_BLOCK), jnp.bfloat16)],
        compiler_params=_params(("arbitrary", "arbitrary")),
        name=f"attn_d{dilation}",
    )(zd, zd, zd, zd, zd)


def _attention(z, z_dil):
    outs = []
    for (window, dilation), zd in zip(DILATED_PATTERNS, z_dil):
        assert window // dilation == ATTN_BLOCK
        outs.append(_attn_pattern(zd, dilation))
    return outs


def _s5_param_kernel(lr_r, li_r, ld_r, bre_ref, bim_ref, cre_ref, cim_ref,
                     krev_ref, pblk_ref, qblk_ref, at_ref):
    t_chunk = S5_CHUNK
    nst = STATE_PER_BLOCK
    lr, li, ld = lr_r[...], li_r[...], ld_r[...]
    dt = jnp.exp(ld)
    mag = jnp.exp(lr * dt)
    ar, ai = mag * jnp.cos(li * dt), mag * jnp.sin(li * dt)
    den = lr * lr + li * li
    cr = ((ar - 1.0) * lr + ai * li) / den
    ci = (ai * lr - (ar - 1.0) * li) / den
    bre, bim = bre_ref[...], bim_ref[...]
    bbr = cr * bre - ci * bim
    bbi = cr * bim + ci * bre
    cre, cim = cre_ref[...], cim_ref[...]
    cre_b, cim_b = cre.astype(pblk_ref.dtype), cim.astype(pblk_ref.dtype)
    nt = (((1,), (1,)), ((), ()))
    er, ei = jnp.ones_like(ar), jnp.zeros_like(ai)
    for tau in range(t_chunk):
        pr = er * bbr - ei * bbi
        pi = er * bbi + ei * bbr
        blk = t_chunk - 1 - tau
        rows = slice(blk * LANES, (blk + 1) * LANES)
        prb, pib = pr.astype(pblk_ref.dtype), pi.astype(pblk_ref.dtype)
        pblk_ref[rows, 0:nst] = prb
        pblk_ref[rows, nst:2 * nst] = pib
        k_tau = (lax.dot_general(prb, cre_b, nt, preferred_element_type=jnp.float32)
                 - lax.dot_general(pib, cim_b, nt, preferred_element_type=jnp.float32))
        krev_ref[rows, :] = k_tau.astype(krev_ref.dtype)
        er, ei = er * ar - ei * ai, er * ai + ei * ar
        qblk_ref[tau, :, 0:nst] = (cre * er - cim * ei).astype(qblk_ref.dtype)
        qblk_ref[tau, :, nst:2 * nst] = (-(cre * ei + cim * er)).astype(qblk_ref.dtype)
    krev_ref[t_chunk * LANES:, :] = jnp.zeros((LANES, LANES), krev_ref.dtype)
    at_ref[:, 0:nst] = er
    at_ref[:, nst:2 * nst] = ei


def _s5_params(lam_re, lam_im, log_dt, b_re, b_im, c_re, c_im):
    g, n = lam_re.shape
    p = b_re.shape[-1]
    gb = GROUPS_PER_BLOCK
    nlb = g // gb
    nst = gb * n
    f32 = jnp.float32
    eye = jnp.eye(gb, dtype=f32)

    def row(a):
        return a.astype(f32).reshape(nlb, 1, nst)

    ldt = jnp.broadcast_to(log_dt.astype(f32)[:, None], (g, n))

    def b_blockdiag(b):
        b4 = b.astype(f32).reshape(nlb, gb, n, p)
        return jnp.einsum("lhnq,gh->lgqhn", b4, eye).reshape(nlb, gb * p, nst)

    def c_blockdiag(c):
        c4 = c.astype(f32).reshape(nlb, gb, p, n)
        return jnp.einsum("lhpn,gh->lgphn", c4, eye).reshape(nlb, gb * p, nst)

    tc = S5_CHUNK
    vec_r = pl.BlockSpec((None, 1, nst), lambda l: (l, 0, 0))
    mat = pl.BlockSpec((None, LANES, nst), lambda l: (l, 0, 0))
    return pl.pallas_call(
        _s5_param_kernel,
        grid=(nlb,),
        in_specs=[vec_r, vec_r, vec_r, mat, mat, mat, mat],
        out_specs=[pl.BlockSpec((None, (tc + 1) * LANES, LANES), lambda l: (l, 0, 0)),
                   pl.BlockSpec((None, tc * LANES, 2 * nst), lambda l: (l, 0, 0)),
                   pl.BlockSpec((None, tc, LANES, 2 * nst), lambda l: (l, 0, 0, 0)),
                   pl.BlockSpec((None, 1, 2 * nst), lambda l: (l, 0, 0))],
        out_shape=[jax.ShapeDtypeStruct((nlb, (tc + 1) * LANES, LANES), jnp.bfloat16),
                   jax.ShapeDtypeStruct((nlb, tc * LANES, 2 * nst), jnp.bfloat16),
                   jax.ShapeDtypeStruct((nlb, tc, LANES, 2 * nst), jnp.bfloat16),
                   jax.ShapeDtypeStruct((nlb, 1, 2 * nst), f32)],
        compiler_params=_params(("arbitrary",)),
        name="s5_params",
    )(row(lam_re), row(lam_im), row(ldt),
      b_blockdiag(b_re), b_blockdiag(b_im), c_blockdiag(c_re), c_blockdiag(c_im))


def _load_ucat(u_ref, uf_scr, ucat_scr, nc):
    uf_scr[...] = u_ref[...].astype(jnp.float32)
    for s in range(S5_CHUNK):
        ucat_scr[:, s * LANES:(s + 1) * LANES] = (
            uf_scr[pl.ds(s, nc, stride=S5_CHUNK), :].astype(ucat_scr.dtype))


def _s5_state_in_kernel(u_ref, pblk_ref, b_ref, uf_scr, ucat_scr, *, nc):
    _load_ucat(u_ref, uf_scr, ucat_scr, nc)
    b_ref[...] = jnp.dot(ucat_scr[...], pblk_ref[...], preferred_element_type=jnp.float32)


def _s5_scan_kernel(b_ref, at_ref, h_ref, *, nc, nlb):
    nst = STATE_PER_BLOCK
    a_re = at_ref[:, 0:nst]
    a_im = at_ref[:, nst:2 * nst]

    def body(c, carry):
        hr, hi = carry
        h_ref[c, :, 0:nst] = hr
        h_ref[c, :, nst:2 * nst] = hi
        bc = b_ref[c]
        return (a_re * hr - a_im * hi + bc[:, 0:nst], a_re * hi + a_im * hr + bc[:, nst:2 * nst])

    zero = jnp.zeros((nlb, nst), jnp.float32)
    lax.fori_loop(0, nc, body, (zero, zero))


def _s5_out_kernel(u_ref, h_ref, krev_ref, qblk_ref, dsk_ref, y_ref, uf_scr, ucat_scr, y_scr, *, nc):
    _load_ucat(u_ref, uf_scr, ucat_scr, nc)
    hb = h_ref[...].astype(jnp.bfloat16)
    dsk = dsk_ref[...]
    tc = S5_CHUNK
    nt = (((1,), (1,)), ((), ()))
    for t in range(0, tc, 2):
        kd = (t + 2) * LANES
        r_t = (tc - 1 - t) * LANES
        taps = jnp.concatenate([krev_ref[r_t:r_t + kd, :], krev_ref[r_t - LANES:r_t - LANES + kd, :]], axis=1)
        y2 = jnp.dot(ucat_scr[:, 0:kd], taps, preferred_element_type=jnp.float32)
        carry_in = qblk_ref[t:t + 2].reshape(2 * LANES, qblk_ref.shape[2])
        y2 = y2 + lax.dot_general(hb, carry_in, nt, preferred_element_type=jnp.float32)
        for s in range(2):
            y = y2[:, s * LANES:(s + 1) * LANES] + dsk * uf_scr[pl.ds(t + s, nc, stride=tc), :]
            y = 0.5 * y * (1.0 + lax.erf(y * (2.0 ** -0.5)))
            y_scr[pl.ds(t + s, nc, stride=tc), :] = y
    y_ref[...] = y_scr[...].astype(y_ref.dtype)


def _s5(z, lam_re, lam_im, log_dt, b_re, b_im, c_re, c_im, d_skip):
    s = z.shape[0]
    nlb = lam_re.shape[0] // GROUPS_PER_BLOCK
    nc = s // S5_CHUNK
    nst2 = 2 * STATE_PER_BLOCK
    u_col0 = (3 * ATTN_WIDTH) // LANES
    krev, pblk, qblk, at = _s5_params(lam_re, lam_im, log_dt, b_re, b_im, c_re, c_im)
    u_spec = pl.BlockSpec((s, LANES), lambda l: (0, u_col0 + l))
    b2 = pl.pallas_call(
        functools.partial(_s5_state_in_kernel, nc=nc),
        grid=(nlb,),
        in_specs=[u_spec, pl.BlockSpec((None, S5_CHUNK * LANES, nst2), lambda l: (l, 0, 0))],
        out_specs=pl.BlockSpec((nc, nst2), lambda l: (0, l)),
        out_shape=jax.ShapeDtypeStruct((nc, nlb * nst2), jnp.float32),
        scratch_shapes=[pltpu.VMEM((s, LANES), jnp.float32),
                        pltpu.VMEM((nc, S5_CHUNK * LANES), jnp.bfloat16)],
        compiler_params=_params(("arbitrary",)),
        name="s5_state_in",
    )(z, pblk)
    h3 = pl.pallas_call(
        functools.partial(_s5_scan_kernel, nc=nc, nlb=nlb),
        out_shape=jax.ShapeDtypeStruct((nc, nlb, nst2), jnp.float32),
        compiler_params=pltpu.CompilerParams(vmem_limit_bytes=VMEM_LIMIT_BYTES),
        name="s5_scan",
    )(b2.reshape(nc, nlb, nst2), at.reshape(nlb, nst2))
    return pl.pallas_call(
        functools.partial(_s5_out_kernel, nc=nc),
        grid=(nlb,),
        in_specs=[u_spec,
                  pl.BlockSpec((nc, nst2), lambda l: (0, l)),
                  pl.BlockSpec((None, (S5_CHUNK + 1) * LANES, LANES), lambda l: (l, 0, 0)),
                  pl.BlockSpec((None, S5_CHUNK, LANES, nst2), lambda l: (l, 0, 0, 0)),
                  pl.BlockSpec((1, LANES), lambda l: (0, l))],
        out_specs=pl.BlockSpec((s, LANES), lambda l: (0, l)),
        out_shape=jax.ShapeDtypeStruct((s, nlb * LANES), jnp.bfloat16),
        scratch_shapes=[pltpu.VMEM((s, LANES), jnp.float32),
                        pltpu.VMEM((nc, S5_CHUNK * LANES), jnp.bfloat16),
                        pltpu.VMEM((s, LANES), jnp.float32)],
        compiler_params=_params(("arbitrary",)),
        name="s5_out",
    )(z, h3.reshape(nc, nlb * nst2), krev, qblk, d_skip.astype(jnp.float32).reshape(1, -1))


def _mixout_kernel(*refs, alpha, n_pat):
    o_refs = refs[:n_pat]
    lse_refs = refs[n_pat:2 * n_pat]
    (yg_ref, x_ref, wglu_ref, bglu_ref, ga_ref, gs_ref, wout_ref,
     gt_ref, lg_ref, lb_ref, sc_ref, sh_ref, x1_ref, h2_ref, o_scr, lse_scr) = refs[2 * n_pat:]
    planes, tm, _ = o_scr.shape
    aw = planes * LANES

    def token_major(ref, scr):
        dil = ref.shape[0]
        if dil == 1:
            return ref[0].astype(jnp.float32)
        n_planes = ref.shape[2] // LANES
        for r in range(dil):
            for cc in range(n_planes):
                scr[cc, pl.ds(r, tm // dil, stride=dil), :] = (
                    ref[r, :, cc * LANES:(cc + 1) * LANES].astype(jnp.float32))
        return jnp.concatenate([scr[cc] for cc in range(n_planes)], axis=1)

    lses = [token_major(ref, lse_scr) for ref in lse_refs]
    top = functools.reduce(jnp.maximum, lses)
    ws = [jnp.exp(v - top) for v in lses]
    inv = 1.0 / functools.reduce(lambda u, v: u + v, ws)
    head_of_lane = lax.broadcasted_iota(jnp.int32, (LANES, aw), 1) // HEAD_DIM
    spread = (lax.broadcasted_iota(jnp.int32, (LANES, aw), 0) == head_of_lane).astype(jnp.bfloat16)
    spread2 = jnp.concatenate([spread, spread], axis=0)
    a = jnp.zeros((tm, aw), jnp.float32)
    for w, o_ref in zip(ws, o_refs):
        w = w * inv
        w_hi = w.astype(jnp.bfloat16)
        w_lo = (w - w_hi.astype(jnp.float32)).astype(jnp.bfloat16)
        wide = jnp.dot(jnp.concatenate([w_hi, w_lo], axis=1), spread2, preferred_element_type=jnp.float32)
        a = a + wide * token_major(o_ref, o_scr)
    ra = a * lax.rsqrt(jnp.mean(a * a, axis=-1, keepdims=True) + LN_EPS) * ga_ref[...]
    yg = yg_ref[...]
    y = yg.astype(jnp.float32)
    gate = jax.nn.sigmoid(jnp.dot(yg, wglu_ref[...], preferred_element_type=jnp.float32) + bglu_ref[...])
    y = y * gate
    ry = y * lax.rsqrt(jnp.mean(y * y, axis=-1, keepdims=True) + LN_EPS) * gs_ref[...]
    mix = (jnp.dot(ra.astype(jnp.bfloat16), wout_ref[0:aw, :], preferred_element_type=jnp.float32)
           + jnp.dot(ry.astype(jnp.bfloat16), wout_ref[aw:, :], preferred_element_type=jnp.float32))
    t = alpha * x_ref[...] + (1.0 + gt_ref[...]) * mix
    x1 = _ln_rows(t) * lg_ref[...] + lb_ref[...]
    x1_ref[...] = x1
    h2_ref[...] = (_ln_rows(x1) * (1.0 + sc_ref[...]) + sh_ref[...]).astype(h2_ref.dtype)


def _mixout(attn_parts, yg, x2, w_glu_bf, b_glu, g_attn, g_ssm, w_out_bf, gt1, ln_g, ln_b, sc2, sh2, alpha):
    s, d = x2.shape
    aw = attn_parts[0][0].shape[-1]
    sw = yg.shape[1]
    n_pat = len(attn_parts)
    tm = min(512, s)
    row = lambda i: (i, 0)
    fix = lambda i: (0, 0)
    vec = lambda n: pl.BlockSpec((1, n), fix)
    res = lambda a: pl.BlockSpec((a.shape[0], tm // a.shape[0], a.shape[2]), lambda i: (0, i, 0))
    return pl.pallas_call(
        functools.partial(_mixout_kernel, alpha=alpha, n_pat=n_pat),
        grid=(s // tm,),
        in_specs=([res(o) for o, _ in attn_parts] + [res(v) for _, v in attn_parts]
                  + [pl.BlockSpec((tm, sw), row), pl.BlockSpec((tm, d), row),
                     pl.BlockSpec((sw, sw), fix), vec(sw), vec(aw), vec(sw),
                     pl.BlockSpec((aw + sw, d), fix), vec(d), vec(d), vec(d), vec(d), vec(d)]),
        out_specs=[pl.BlockSpec((tm, d), row), pl.BlockSpec((tm, d), row)],
        out_shape=[jax.ShapeDtypeStruct((s, d), jnp.float32), jax.ShapeDtypeStruct((s, d), jnp.bfloat16)],
        scratch_shapes=[pltpu.VMEM((aw // LANES, tm, LANES), jnp.float32),
                        pltpu.VMEM((1, tm, LANES), jnp.float32)],
        compiler_params=_params(("arbitrary",)),
        name="mixout",
    )(*[o for o, _ in attn_parts], *[l for _, l in attn_parts], yg, x2, w_glu_bf, b_glu.reshape(1, -1),
      g_attn.reshape(1, -1), g_ssm.reshape(1, -1), w_out_bf, gt1, ln_g.reshape(1, -1), ln_b.reshape(1, -1),
      sc2, sh2)


def _top_values(sc, k, want_rank=False):
    vals = []
    rank = jnp.full(sc.shape, float(k), jnp.float32) if want_rank else None
    for j in range(k):
        m = jnp.max(sc, axis=0, keepdims=True)
        vals.append(m)
        hit = sc == m
        if want_rank:
            rank = jnp.where(hit, float(j), rank)
        sc = jnp.where(hit, BIG_NEG, sc)
    return vals, rank


def _peerq_kernel(h2_ref, wqp_ref, keys_ref, c0_ref, r1_ref, e0_ref, e1_ref, sc_scr):
    k = PEER_TOPK
    tm = h2_ref.shape[0]
    qp = jnp.dot(h2_ref[...], wqp_ref[...], preferred_element_type=jnp.float32).astype(jnp.bfloat16)
    nt = (((1,), (1,)), ((), ()))
    for hi in range(2 * PEER_HEADS):
        col = hi * PEER_SUB_DIM
        sc_scr[hi] = lax.dot_general(keys_ref[hi % 2], qp[:, col:col + PEER_SUB_DIM], nt,
                                     preferred_element_type=jnp.float32)
    def body(h, carry):
        s0 = sc_scr[2 * h]
        s1 = sc_scr[2 * h + 1]
        top0, _ = _top_values(s0, k)
        top1, rank1 = _top_values(s1, k, want_rank=True)
        cands = [top0[i] + top1[j] for i in range(k) for j in range(k) if (i + 1) * (j + 1) <= k]
        pad = (-len(cands)) % SUBLANES
        cmat = jnp.concatenate(cands + [jnp.full_like(cands[0], BIG_NEG)] * pad, axis=0)
        best, _ = _top_values(cmat, k)
        thr = best[k - 1]
        m0, m1 = top0[0], top1[0]
        zsum = jnp.sum(jnp.where(cmat >= thr, jnp.exp(cmat - (m0 + m1)), 0.0), axis=0, keepdims=True)
        count0 = jnp.zeros_like(s0)
        for j in range(k):
            count0 = jnp.where(s0 + top1[j] >= thr, float(j + 1), count0)
        c0_ref[h] = count0
        r1_ref[h] = rank1.astype(r1_ref.dtype)
        e0_ref[h] = jnp.exp(s0 - m0) / zsum
        e1_ref[h] = jnp.exp(s1 - m1).astype(e1_ref.dtype)
        return carry

    lax.fori_loop(0, PEER_HEADS, body, 0)


def _peerq(h2, w_qp_bf, keys_bf):
    s, d = h2.shape
    tm = min(512, s)
    nk = keys_bf.shape[1]
    stat = pl.BlockSpec((PEER_HEADS, nk, tm), lambda i: (0, 0, i))
    words = jax.ShapeDtypeStruct((PEER_HEADS, nk, s), jnp.float32)
    halfs = jax.ShapeDtypeStruct((PEER_HEADS, nk, s), jnp.bfloat16)
    return pl.pallas_call(
        _peerq_kernel,
        grid=(s // tm,),
        in_specs=[pl.BlockSpec((tm, d), lambda i: (i, 0)),
                  pl.BlockSpec(w_qp_bf.shape, lambda i: (0, 0)),
                  pl.BlockSpec(keys_bf.shape, lambda i: (0, 0, 0))],
        out_specs=[stat, stat, stat, stat],
        out_shape=[words, halfs, words, halfs],
        scratch_shapes=[pltpu.VMEM((2 * PEER_HEADS, nk, tm), jnp.float32)],
        compiler_params=_params(("arbitrary",)),
        name="peerq",
    )(h2, w_qp_bf, keys_bf)


def _peer_kernel(ua_ref, ub_ref, vt_ref, h2t_ref, c0_ref, e0_ref, r1_ref, e1_ref, o_ref,
                 a_s0, a_s1, w_s0, w_s1, *, n_pairs, lane_chunk):
    m = pl.program_id(1)
    nk = PEER_N_KEYS
    te, tm = a_s0.shape
    rows_per_tile = te // nk
    packed = 2 * SUBLANES
    bf = jnp.bfloat16

    def stage_a(u_ref, dst):
        dst[...] = jnp.dot(u_ref[...], h2t_ref[...], preferred_element_type=jnp.float32)

    def row_tile(ref, h, i0, ls):
        return jnp.broadcast_to(ref[h, i0:i0 + 1, ls], (packed, lane_chunk)).astype(bf)

    def stage_b(src, dst, row_off, ls):
        for rr in range(rows_per_tile):
            rows = slice(rr * nk, (rr + 1) * nk)
            i0 = row_off + rr
            a = src[rows, ls]
            act = 0.5 * a * (1.0 + lax.erf(a * (2.0 ** -0.5)))
            gate = jnp.zeros((nk // packed, packed, lane_chunk), bf)
            for h in range(PEER_HEADS):
                c0 = row_tile(c0_ref, h, i0, ls)
                e0 = row_tile(e0_ref, h, i0, ls)
                r1 = r1_ref[h, :, ls].reshape(nk // packed, packed, lane_chunk)
                e1 = e1_ref[h, :, ls].reshape(nk // packed, packed, lane_chunk)
                gate = gate + jnp.where(r1 < c0, e0 * e1, jnp.zeros_like(e1))
            w = gate * act.astype(bf).reshape(nk // packed, packed, lane_chunk)
            dst[rows, ls] = w.reshape(nk, lane_chunk)

    def tile(src, dst, half, row_off):
        for c in range(tm // lane_chunk):
            ls = slice(c * lane_chunk, (c + 1) * lane_chunk)
            stage_b(src, dst, row_off, ls)
            o_ref[:, ls] += jnp.dot(vt_ref[:, half * te:(half + 1) * te], dst[:, ls],
                                    preferred_element_type=jnp.float32)

    @pl.when(m == 0)
    def _():
        o_ref[...] = jnp.zeros_like(o_ref)
        stage_a(ub_ref, a_s0)

    @pl.when(jnp.logical_and(m > 0, m < n_pairs))
    def _():
        stage_a(ua_ref, a_s1)
        tile(a_s0, w_s0, 0, 0)
        stage_a(ub_ref, a_s0)
        tile(a_s1, w_s1, 1, rows_per_tile)

    @pl.when(m == n_pairs)
    def _():
        stage_a(ua_ref, a_s1)
        tile(a_s0, w_s0, 0, 0)
        tile(a_s1, w_s1, 1, rows_per_tile)


def _peer(h2t, u_bf, vt_bf, a0, n1, e0, e1):
    d, s = h2t.shape
    ne = u_bf.shape[0]
    nk = PEER_N_KEYS
    tm = min(512, s)
    te = (SUBLANES // 2) * nk
    n_tiles = ne // te
    n_pairs = n_tiles // 2
    rows = pl.BlockSpec((PEER_HEADS, SUBLANES, tm), lambda i, m: (0, jnp.maximum(m - 1, 0), i))
    full_stat = pl.BlockSpec((PEER_HEADS, nk, tm), lambda i, m: (0, 0, i))
    return pl.pallas_call(
        functools.partial(_peer_kernel, n_pairs=n_pairs, lane_chunk=min(256, tm)),
        grid=(s // tm, n_pairs + 1),
        in_specs=[pl.BlockSpec((te, d), lambda i, m: (jnp.maximum(2 * m - 1, 0), 0)),
                  pl.BlockSpec((te, d), lambda i, m: (jnp.minimum(2 * m, n_tiles - 1), 0)),
                  pl.BlockSpec((d, 2 * te), lambda i, m: (0, jnp.maximum(m - 1, 0))),
                  pl.BlockSpec((d, tm), lambda i, m: (0, i)),
                  rows, rows, full_stat, full_stat],
        out_specs=pl.BlockSpec((d, tm), lambda i, m: (0, i)),
        out_shape=jax.ShapeDtypeStruct((d, s), jnp.float32),
        scratch_shapes=[pltpu.VMEM((te, tm), jnp.float32), pltpu.VMEM((te, tm), jnp.float32),
                        pltpu.VMEM((te, tm), jnp.bfloat16), pltpu.VMEM((te, tm), jnp.bfloat16)],
        compiler_params=_params(("arbitrary", "arbitrary")),
        name="peer",
    )(u_bf, u_bf, vt_bf, h2t, a0, e0, n1, e1)


def _final_kernel(ft_ref, x1_ref, gt_ref, lg_ref, lb_ref, o_ref, *, alpha):
    ffn = ft_ref[...].T
    t = alpha * x1_ref[...] + (1.0 + gt_ref[...]) * ffn
    o_ref[...] = _ln_rows(t) * lg_ref[...] + lb_ref[...]


def _final(ffn_t, x1, gt2, ln_g, ln_b, alpha):
    s, d = x1.shape
    tm = min(512, s)
    vec = pl.BlockSpec((1, d), lambda i: (0, 0))
    return pl.pallas_call(
        functools.partial(_final_kernel, alpha=alpha),
        grid=(s // tm,),
        in_specs=[pl.BlockSpec((d, tm), lambda i: (0, i)), pl.BlockSpec((tm, d), lambda i: (i, 0)),
                  vec, vec, vec],
        out_specs=pl.BlockSpec((tm, d), lambda i: (i, 0)),
        out_shape=jax.ShapeDtypeStruct((s, d), jnp.float32),
        compiler_params=_params(("arbitrary",)),
        name="final",
    )(ffn_t, x1, gt2, ln_g.reshape(1, -1), ln_b.reshape(1, -1))


def kernel(x, c, positions, w_ada, b_ada, w_in, lam_re, lam_im, log_dt, ssm_b_re, ssm_b_im, ssm_c_re,
           ssm_c_im, ssm_d, w_glu, b_glu, g_attn, g_ssm, w_out, ln1_g, ln1_b, w_qp, sub_keys, expert_u,
           expert_v, ln2_g, ln2_b):
    b, s, d = x.shape
    assert b == 1, "one sequence per call"
    depth = w_ada.shape[0]
    alpha = (2.0 * depth) ** 0.25
    bf = jnp.bfloat16
    x2 = x.reshape(s, d)
    pos = positions.reshape(s, 1)
    dilations = [dil for _, dil in DILATED_PATTERNS if dil > 1]
    for l in range(depth):
        mod = _ada(c, w_ada[l], b_ada[l])
        sh1, sc1, gt1, sh2, sc2, gt2 = [mod[:, k * d:(k + 1) * d] for k in range(ADA_CHUNKS)]
        z, *z_res = _inproj(x2, pos, sc1, sh1, w_in[l].astype(bf), dilations)
        z_dil = [z.reshape(1, s, -1) if dil == 1 else z_res[dilations.index(dil)]
                 for _, dil in DILATED_PATTERNS]
        attn = _attention(z, z_dil)
        yg = _s5(z, lam_re[l], lam_im[l], log_dt[l], ssm_b_re[l], ssm_b_im[l],
                 ssm_c_re[l], ssm_c_im[l], ssm_d[l])
        x1, h2 = _mixout(attn, yg, x2, w_glu[l].astype(bf), b_glu[l], g_attn[l], g_ssm[l],
                         w_out[l].astype(bf), gt1, ln1_g[l], ln1_b[l], sc2, sh2, alpha)
        a0, n1, e0, e1 = _peerq(h2, w_qp[l].astype(bf), sub_keys[l].astype(bf))
        ffn_t = _peer(h2.T, expert_u[l].astype(bf), expert_v[l].T.astype(bf), a0, n1, e0, e1)
        x2 = _final(ffn_t, x1, gt2, ln2_g[l], ln2_b[l], alpha)
    return x2.reshape(b, s, d)
```

```python
import functools
import math

import jax
import jax.numpy as jnp
from jax import lax
from jax.experimental import pallas as pl
from jax.experimental.pallas import tpu as pltpu

ATTN_HEADS = 16
HEAD_DIM = 64
ATTN_WIDTH = ATTN_HEADS * HEAD_DIM
SSM_GROUP = 16
SSM_STATE = 64
DILATED_PATTERNS = ((128, 1), (512, 4), (2048, 16))
ATTN_BLOCK = 128
ROPE_THETA = 10000.0
NEG_INF = -1e30
PEER_HEADS = 8
PEER_SUB_DIM = 128
PEER_N_KEYS = 128
PEER_TOPK = 16
LN_EPS = 1e-5
ADA_CHUNKS = 6

LANES = 128
SUBLANES = 8
VMEM_LIMIT_BYTES = 56 * 1024 * 1024

S5_CHUNK = 16
GROUPS_PER_BLOCK = LANES // SSM_GROUP
STATE_PER_BLOCK = GROUPS_PER_BLOCK * SSM_STATE
BIG_NEG = -3.0e38
QK_LOG2_SCALE = HEAD_DIM ** -0.5 * math.log2(math.e)


def _params(sem):
    return pltpu.CompilerParams(dimension_semantics=sem, vmem_limit_bytes=VMEM_LIMIT_BYTES)


def _ln_rows(t):
    mu = jnp.mean(t, axis=-1, keepdims=True)
    d = t - mu
    var = jnp.mean(d * d, axis=-1, keepdims=True)
    return d * lax.rsqrt(var + LN_EPS)


def _ada_kernel(c_ref, w_ref, b_ref, o_ref):
    c = c_ref[...]
    sc = c * jax.nn.sigmoid(c)
    o_ref[...] = jnp.sum(sc * w_ref[...], axis=0, keepdims=True) + b_ref[...]


def _ada(c, w_ada, b_ada):
    d = c.shape[-1]
    n = w_ada.shape[-1]
    tn = n // 8
    return pl.pallas_call(
        _ada_kernel,
        grid=(n // tn,),
        in_specs=[pl.BlockSpec((d, 1), lambda j: (0, 0)),
                  pl.BlockSpec((d, tn), lambda j: (0, j)),
                  pl.BlockSpec((1, tn), lambda j: (0, j))],
        out_specs=pl.BlockSpec((1, tn), lambda j: (0, j)),
        out_shape=jax.ShapeDtypeStruct((1, n), jnp.float32),
        compiler_params=_params(("arbitrary",)),
        name="ada",
    )(c.reshape(d, 1), w_ada, b_ada.reshape(1, n))


def _inproj_kernel(x_ref, pos_ref, sc_ref, sh_ref, w_ref, z_ref, *rest, n_rope, dilations):
    dil_refs = rest[:len(dilations)]
    h_scr, cos_scr, sin_scr, z_scr = rest[len(dilations):]
    j = pl.program_id(1)
    tm, tn = z_ref.shape

    @pl.when(j == 0)
    def _():
        h = _ln_rows(x_ref[...]) * (1.0 + sc_ref[...]) + sh_ref[...]
        h_scr[...] = h.astype(jnp.bfloat16)
        lane = lax.broadcasted_iota(jnp.int32, (1, LANES), 1)
        fi = ((lane % HEAD_DIM) % (HEAD_DIM // 2)).astype(jnp.float32)
        inv_freq = jnp.exp(fi * (-math.log(ROPE_THETA) / (HEAD_DIM // 2)))
        ang = pos_ref[...].astype(jnp.float32) * inv_freq
        first_half = (lane % HEAD_DIM) < (HEAD_DIM // 2)
        cos_scr[...] = jnp.cos(ang)
        sin_scr[...] = jnp.where(first_half, -jnp.sin(ang), jnp.sin(ang))

    z = jnp.dot(h_scr[...], w_ref[...], preferred_element_type=jnp.float32)

    @pl.when(j < n_rope)
    def _():
        lane = lax.broadcasted_iota(jnp.int32, (1, LANES), 1)
        first_half = (lane % HEAD_DIM) < (HEAD_DIM // 2)
        q_mult = jnp.where(j < n_rope // 2, QK_LOG2_SCALE, 1.0)
        cos = cos_scr[...] * q_mult
        sin = sin_scr[...] * q_mult
        for cc in range(tn // LANES):
            zc = z[:, cc * LANES:(cc + 1) * LANES]
            partner = jnp.where(first_half, pltpu.roll(zc, LANES - HEAD_DIM // 2, 1),
                                pltpu.roll(zc, HEAD_DIM // 2, 1))
            z_scr[cc] = zc * cos + partner * sin

    @pl.when(j >= n_rope)
    def _():
        for cc in range(tn // LANES):
            z_scr[cc] = z[:, cc * LANES:(cc + 1) * LANES]

    for cc in range(tn // LANES):
        z_ref[:, cc * LANES:(cc + 1) * LANES] = z_scr[cc].astype(z_ref.dtype)

    @pl.when(j < (3 * ATTN_WIDTH) // tn)
    def _():
        for dil, ref in zip(dilations, dil_refs):
            for r in range(dil):
                for cc in range(tn // LANES):
                    ref[r, :, cc * LANES:(cc + 1) * LANES] = (
                        z_scr[cc, pl.ds(r, tm // dil, stride=dil), :].astype(ref.dtype))


def _inproj(x2, pos, sc1, sh1, w_in_bf, dilations):
    s, d = x2.shape
    n = w_in_bf.shape[1]
    tm = min(1024, s)
    tn = 512
    n_rope = (2 * ATTN_WIDTH) // tn
    n_qkv = (3 * ATTN_WIDTH) // tn
    bf = jnp.bfloat16
    dil_specs = [pl.BlockSpec((dil, tm // dil, tn), lambda i, j: (0, i, jnp.minimum(j, n_qkv - 1)))
                 for dil in dilations]
    dil_shapes = [jax.ShapeDtypeStruct((dil, s // dil, 3 * ATTN_WIDTH), bf) for dil in dilations]
    return pl.pallas_call(
        functools.partial(_inproj_kernel, n_rope=n_rope, dilations=tuple(dilations)),
        grid=(s // tm, n // tn),
        in_specs=[pl.BlockSpec((tm, d), lambda i, j: (i, 0)),
                  pl.BlockSpec((tm, 1), lambda i, j: (i, 0)),
                  pl.BlockSpec((1, d), lambda i, j: (0, 0)),
                  pl.BlockSpec((1, d), lambda i, j: (0, 0)),
                  pl.BlockSpec((d, tn), lambda i, j: (0, j))],
        out_specs=[pl.BlockSpec((tm, tn), lambda i, j: (i, j))] + dil_specs,
        out_shape=[jax.ShapeDtypeStruct((s, n), bf)] + dil_shapes,
        scratch_shapes=[pltpu.VMEM((tm, d), bf),
                        pltpu.VMEM((tm, LANES), jnp.float32),
                        pltpu.VMEM((tm, LANES), jnp.float32),
                        pltpu.VMEM((tn // LANES, tm, LANES), jnp.float32)],
        compiler_params=_params(("arbitrary", "arbitrary")),
        name="inproj",
    )(x2, pos, sc1, sh1, w_in_bf)


def _attn_kernel(q_ref, kc_ref, vc_ref, kp_ref, vp_ref, o_ref, lse_ref, kwin, vt_win, *, nblk):
    i = pl.program_id(1)
    blk = ATTN_BLOCK
    kwin[0:blk, :] = kp_ref[...]
    kwin[blk:, :] = kc_ref[...]
    vt_win[:, 0:blk] = vp_ref[...].T
    vt_win[:, blk:] = vc_ref[...].T

    key = lax.broadcasted_iota(jnp.int32, (2 * blk, blk), 0)
    qry = lax.broadcasted_iota(jnp.int32, (2 * blk, blk), 1)
    lane = lax.broadcasted_iota(jnp.int32, (blk, LANES), 1)
    row = lax.broadcasted_iota(jnp.int32, (blk, LANES), 0)
    nt = (((1,), (1,)), ((), ()))

    def body(j, carry):
        r0 = pl.multiple_of(j * blk, blk)
        first_key = jnp.where((i * nblk + j) > 0, 0, blk)
        mask = jnp.logical_and(key >= jnp.maximum(qry, first_key), key <= qry + blk)
        lse_rows = []
        for hp in range(ATTN_HEADS // 2):
            cs = slice(hp * LANES, (hp + 1) * LANES)
            q2 = q_ref[pl.ds(r0, blk), cs]
            k2 = kwin[pl.ds(r0, 2 * blk), cs]
            vt2 = vt_win[cs, pl.ds(r0, 2 * blk)]
            halves = []
            for hd in range(2):
                in_head = (lane < HEAD_DIM) if hd == 0 else (lane >= HEAD_DIM)
                qm = jnp.where(in_head, q2, jnp.zeros_like(q2))
                st = lax.dot_general(k2, qm, nt, preferred_element_type=jnp.float32)
                st = jnp.where(mask, st, NEG_INF)
                m = jnp.max(st, axis=0, keepdims=True)
                p = jnp.exp2(st - m)
                l = jnp.sum(p, axis=0, keepdims=True)
                ot = jnp.dot(vt2, p.astype(vt2.dtype), preferred_element_type=jnp.float32)
                halves.append(ot / l)
                lse_rows.append(m * math.log(2.0) + jnp.log(l))
            ot2 = jnp.where(row < HEAD_DIM, halves[0], halves[1])
            o_ref[pl.ds(r0, blk), cs] = ot2.T.astype(o_ref.dtype)
        lse_mat = jnp.zeros((LANES, blk), jnp.float32)
        for hidx, lse_h in enumerate(lse_rows):
            lse_mat = jnp.where(row == hidx, lse_h, lse_mat)
        lse_ref[pl.ds(r0, blk), :] = lse_mat.T
        return carry

    lax.fori_loop(0, nblk, body, 0)


def _attn_pattern(zd, dilation):
    _, length, zw = zd.shape
    aw = ATTN_WIDTH
    rows = min(1024, length)
    nblk = rows // ATTN_BLOCK
    cur = lambda off: pl.BlockSpec((None, rows, aw), lambda r, i: (r, i, off))
    prev = lambda off: pl.BlockSpec((None, ATTN_BLOCK, aw), lambda r, i: (r, jnp.maximum(i * nblk - 1, 0), off))
    return pl.pallas_call(
        functools.partial(_attn_kernel, nblk=nblk),
        grid=(dilation, length // rows),
        in_specs=[cur(0), cur(1), cur(2), prev(1), prev(2)],
        out_specs=[pl.BlockSpec((None, rows, aw), lambda r, i: (r, i, 0)),
                   pl.BlockSpec((None, rows, LANES), lambda r, i: (r, i, 0))],
        out_shape=[jax.ShapeDtypeStruct((dilation, length, aw), jnp.bfloat16),
                   jax.ShapeDtypeStruct((dilation, length, LANES), jnp.float32)],
        scratch_shapes=[pltpu.VMEM((rows + ATTN_BLOCK, aw), jnp.bfloat16),
                        pltpu.VMEM((aw, rows + ATTN_BLOCK), jnp.bfloat16)],
        compiler_params=_params(("arbitrary", "arbitrary")),
        name=f"attn_d{dilation}",
    )(zd, zd, zd, zd, zd)


def _attention(z, z_dil):
    outs = []
    for (window, dilation), zd in zip(DILATED_PATTERNS, z_dil):
        assert window // dilation == ATTN_BLOCK
        outs.append(_attn_pattern(zd, dilation))
    return outs


def _s5_param_kernel(lr_r, li_r, ld_r, bre_ref, bim_ref, cre_ref, cim_ref,
                     krev_ref, pblk_ref, qblk_ref, at_ref):
    t_chunk = S5_CHUNK
    nst = STATE_PER_BLOCK
    lr, li, ld = lr_r[...], li_r[...], ld_r[...]
    dt = jnp.exp(ld)
    mag = jnp.exp(lr * dt)
    ar, ai = mag * jnp.cos(li * dt), mag * jnp.sin(li * dt)
    den = lr * lr + li * li
    cr = ((ar - 1.0) * lr + ai * li) / den
    ci = (ai * lr - (ar - 1.0) * li) / den
    bre, bim = bre_ref[...], bim_ref[...]
    bbr = cr * bre - ci * bim
    bbi = cr * bim + ci * bre
    cre, cim = cre_ref[...], cim_ref[...]
    cre_b, cim_b = cre.astype(pblk_ref.dtype), cim.astype(pblk_ref.dtype)
    nt = (((1,), (1,)), ((), ()))
    er, ei = jnp.ones_like(ar), jnp.zeros_like(ai)
    for tau in range(t_chunk):
        pr = er * bbr - ei * bbi
        pi = er * bbi + ei * bbr
        blk = t_chunk - 1 - tau
        rows = slice(blk * LANES, (blk + 1) * LANES)
        prb, pib = pr.astype(pblk_ref.dtype), pi.astype(pblk_ref.dtype)
        pblk_ref[rows, 0:nst] = prb
        pblk_ref[rows, nst:2 * nst] = pib
        k_tau = (lax.dot_general(prb, cre_b, nt, preferred_element_type=jnp.float32)
                 - lax.dot_general(pib, cim_b, nt, preferred_element_type=jnp.float32))
        krev_ref[rows, :] = k_tau.astype(krev_ref.dtype)
        er, ei = er * ar - ei * ai, er * ai + ei * ar
        qblk_ref[tau, :, 0:nst] = (cre * er - cim * ei).astype(qblk_ref.dtype)
        qblk_ref[tau, :, nst:2 * nst] = (-(cre * ei + cim * er)).astype(qblk_ref.dtype)
    krev_ref[t_chunk * LANES:, :] = jnp.zeros((LANES, LANES), krev_ref.dtype)
    at_ref[:, 0:nst] = er
    at_ref[:, nst:2 * nst] = ei


def _s5_params(lam_re, lam_im, log_dt, b_re, b_im, c_re, c_im):
    g, n = lam_re.shape
    p = b_re.shape[-1]
    gb = GROUPS_PER_BLOCK
    nlb = g // gb
    nst = gb * n
    f32 = jnp.float32
    eye = jnp.eye(gb, dtype=f32)

    def row(a):
        return a.astype(f32).reshape(nlb, 1, nst)

    ldt = jnp.broadcast_to(log_dt.astype(f32)[:, None], (g, n))

    def b_blockdiag(b):
        b4 = b.astype(f32).reshape(nlb, gb, n, p)
        return jnp.einsum("lhnq,gh->lgqhn", b4, eye).reshape(nlb, gb * p, nst)

    def c_blockdiag(c):
        c4 = c.astype(f32).reshape(nlb, gb, p, n)
        return jnp.einsum("lhpn,gh->lgphn", c4, eye).reshape(nlb, gb * p, nst)

    tc = S5_CHUNK
    vec_r = pl.BlockSpec((None, 1, nst), lambda l: (l, 0, 0))
    mat = pl.BlockSpec((None, LANES, nst), lambda l: (l, 0, 0))
    return pl.pallas_call(
        _s5_param_kernel,
        grid=(nlb,),
        in_specs=[vec_r, vec_r, vec_r, mat, mat, mat, mat],
        out_specs=[pl.BlockSpec((None, (tc + 1) * LANES, LANES), lambda l: (l, 0, 0)),
                   pl.BlockSpec((None, tc * LANES, 2 * nst), lambda l: (l, 0, 0)),
                   pl.BlockSpec((None, tc, LANES, 2 * nst), lambda l: (l, 0, 0, 0)),
                   pl.BlockSpec((None, 1, 2 * nst), lambda l: (l, 0, 0))],
        out_shape=[jax.ShapeDtypeStruct((nlb, (tc + 1) * LANES, LANES), jnp.bfloat16),
                   jax.ShapeDtypeStruct((nlb, tc * LANES, 2 * nst), jnp.bfloat16),
                   jax.ShapeDtypeStruct((nlb, tc, LANES, 2 * nst), jnp.bfloat16),
                   jax.ShapeDtypeStruct((nlb, 1, 2 * nst), f32)],
        compiler_params=_params(("arbitrary",)),
        name="s5_params",
    )(row(lam_re), row(lam_im), row(ldt),
      b_blockdiag(b_re), b_blockdiag(b_im), c_blockdiag(c_re), c_blockdiag(c_im))


def _load_ucat(u_ref, uf_scr, ucat_scr, nc):
    uf_scr[...] = u_ref[...].astype(jnp.float32)
    for s in range(S5_CHUNK):
        ucat_scr[:, s * LANES:(s + 1) * LANES] = (
            uf_scr[pl.ds(s, nc, stride=S5_CHUNK), :].astype(ucat_scr.dtype))


def _s5_state_in_kernel(u_ref, pblk_ref, b_ref, uf_scr, ucat_scr, *, nc):
    _load_ucat(u_ref, uf_scr, ucat_scr, nc)
    b_ref[...] = jnp.dot(ucat_scr[...], pblk_ref[...], preferred_element_type=jnp.float32)


def _s5_scan_kernel(b_ref, at_ref, h_ref, *, nc, nlb):
    nst = STATE_PER_BLOCK
    a_re = at_ref[:, 0:nst]
    a_im = at_ref[:, nst:2 * nst]

    def body(c, carry):
        hr, hi = carry
        h_ref[c, :, 0:nst] = hr
        h_ref[c, :, nst:2 * nst] = hi
        bc = b_ref[c]
        return (a_re * hr - a_im * hi + bc[:, 0:nst], a_re * hi + a_im * hr + bc[:, nst:2 * nst])

    zero = jnp.zeros((nlb, nst), jnp.float32)
    lax.fori_loop(0, nc, body, (zero, zero))


def _s5_out_kernel(u_ref, h_ref, krev_ref, qblk_ref, dsk_ref, y_ref, uf_scr, ucat_scr, y_scr, *, nc):
    _load_ucat(u_ref, uf_scr, ucat_scr, nc)
    hb = h_ref[...].astype(jnp.bfloat16)
    dsk = dsk_ref[...]
    tc = S5_CHUNK
    nt = (((1,), (1,)), ((), ()))
    for t in range(0, tc, 2):
        kd = (t + 2) * LANES
        r_t = (tc - 1 - t) * LANES
        taps = jnp.concatenate([krev_ref[r_t:r_t + kd, :], krev_ref[r_t - LANES:r_t - LANES + kd, :]], axis=1)
        y2 = jnp.dot(ucat_scr[:, 0:kd], taps, preferred_element_type=jnp.float32)
        carry_in = qblk_ref[t:t + 2].reshape(2 * LANES, qblk_ref.shape[2])
        y2 = y2 + lax.dot_general(hb, carry_in, nt, preferred_element_type=jnp.float32)
        for s in range(2):
            y = y2[:, s * LANES:(s + 1) * LANES] + dsk * uf_scr[pl.ds(t + s, nc, stride=tc), :]
            y = 0.5 * y * (1.0 + lax.erf(y * (2.0 ** -0.5)))
            y_scr[pl.ds(t + s, nc, stride=tc), :] = y
    y_ref[...] = y_scr[...].astype(y_ref.dtype)


def _s5(z, lam_re, lam_im, log_dt, b_re, b_im, c_re, c_im, d_skip):
    s = z.shape[0]
    nlb = lam_re.shape[0] // GROUPS_PER_BLOCK
    nc = s // S5_CHUNK
    nst2 = 2 * STATE_PER_BLOCK
    u_col0 = (3 * ATTN_WIDTH) // LANES
    krev, pblk, qblk, at = _s5_params(lam_re, lam_im, log_dt, b_re, b_im, c_re, c_im)
    u_spec = pl.BlockSpec((s, LANES), lambda l: (0, u_col0 + l))
    b2 = pl.pallas_call(
        functools.partial(_s5_state_in_kernel, nc=nc),
        grid=(nlb,),
        in_specs=[u_spec, pl.BlockSpec((None, S5_CHUNK * LANES, nst2), lambda l: (l, 0, 0))],
        out_specs=pl.BlockSpec((nc, nst2), lambda l: (0, l)),
        out_shape=jax.ShapeDtypeStruct((nc, nlb * nst2), jnp.float32),
        scratch_shapes=[pltpu.VMEM((s, LANES), jnp.float32),
                        pltpu.VMEM((nc, S5_CHUNK * LANES), jnp.bfloat16)],
        compiler_params=_params(("arbitrary",)),
        name="s5_state_in",
    )(z, pblk)
    h3 = pl.pallas_call(
        functools.partial(_s5_scan_kernel, nc=nc, nlb=nlb),
        out_shape=jax.ShapeDtypeStruct((nc, nlb, nst2), jnp.float32),
        compiler_params=pltpu.CompilerParams(vmem_limit_bytes=VMEM_LIMIT_BYTES),
        name="s5_scan",
    )(b2.reshape(nc, nlb, nst2), at.reshape(nlb, nst2))
    return pl.pallas_call(
        functools.partial(_s5_out_kernel, nc=nc),
        grid=(nlb,),
        in_specs=[u_spec,
                  pl.BlockSpec((nc, nst2), lambda l: (0, l)),
                  pl.BlockSpec((None, (S5_CHUNK + 1) * LANES, LANES), lambda l: (l, 0, 0)),
                  pl.BlockSpec((None, S5_CHUNK, LANES, nst2), lambda l: (l, 0, 0, 0)),
                  pl.BlockSpec((1, LANES), lambda l: (0, l))],
        out_specs=pl.BlockSpec((s, LANES), lambda l: (0, l)),
        out_shape=jax.ShapeDtypeStruct((s, nlb * LANES), jnp.bfloat16),
        scratch_shapes=[pltpu.VMEM((s, LANES), jnp.float32),
                        pltpu.VMEM((nc, S5_CHUNK * LANES), jnp.bfloat16),
                        pltpu.VMEM((s, LANES), jnp.float32)],
        compiler_params=_params(("arbitrary",)),
        name="s5_out",
    )(z, h3.reshape(nc, nlb * nst2), krev, qblk, d_skip.astype(jnp.float32).reshape(1, -1))


def _mixout_kernel(*refs, alpha, n_pat):
    o_refs = refs[:n_pat]
    lse_refs = refs[n_pat:2 * n_pat]
    (yg_ref, x_ref, wglu_ref, bglu_ref, ga_ref, gs_ref, wout_ref,
     gt_ref, lg_ref, lb_ref, sc_ref, sh_ref, x1_ref, h2_ref, o_scr, lse_scr) = refs[2 * n_pat:]
    planes, tm, _ = o_scr.shape
    aw = planes * LANES

    def token_major(ref, scr):
        dil = ref.shape[0]
        if dil == 1:
            return ref[0].astype(jnp.float32)
        n_planes = ref.shape[2] // LANES
        for r in range(dil):
            for cc in range(n_planes):
                scr[cc, pl.ds(r, tm // dil, stride=dil), :] = (
                    ref[r, :, cc * LANES:(cc + 1) * LANES].astype(jnp.float32))
        return jnp.concatenate([scr[cc] for cc in range(n_planes)], axis=1)

    lses = [token_major(ref, lse_scr) for ref in lse_refs]
    top = functools.reduce(jnp.maximum, lses)
    ws = [jnp.exp(v - top) for v in lses]
    inv = 1.0 / functools.reduce(lambda u, v: u + v, ws)
    head_of_lane = lax.broadcasted_iota(jnp.int32, (LANES, aw), 1) // HEAD_DIM
    spread = (lax.broadcasted_iota(jnp.int32, (LANES, aw), 0) == head_of_lane).astype(jnp.bfloat16)
    spread2 = jnp.concatenate([spread, spread], axis=0)
    a = jnp.zeros((tm, aw), jnp.float32)
    for w, o_ref in zip(ws, o_refs):
        w = w * inv
        w_hi = w.astype(jnp.bfloat16)
        w_lo = (w - w_hi.astype(jnp.float32)).astype(jnp.bfloat16)
        wide = jnp.dot(jnp.concatenate([w_hi, w_lo], axis=1), spread2, preferred_element_type=jnp.float32)
        a = a + wide * token_major(o_ref, o_scr)
    ra = a * lax.rsqrt(jnp.mean(a * a, axis=-1, keepdims=True) + LN_EPS) * ga_ref[...]
    yg = yg_ref[...]
    y = yg.astype(jnp.float32)
    gate = jax.nn.sigmoid(jnp.dot(yg, wglu_ref[...], preferred_element_type=jnp.float32) + bglu_ref[...])
    y = y * gate
    ry = y * lax.rsqrt(jnp.mean(y * y, axis=-1, keepdims=True) + LN_EPS) * gs_ref[...]
    mix = (jnp.dot(ra.astype(jnp.bfloat16), wout_ref[0:aw, :], preferred_element_type=jnp.float32)
           + jnp.dot(ry.astype(jnp.bfloat16), wout_ref[aw:, :], preferred_element_type=jnp.float32))
    t = alpha * x_ref[...] + (1.0 + gt_ref[...]) * mix
    x1 = _ln_rows(t) * lg_ref[...] + lb_ref[...]
    x1_ref[...] = x1
    h2_ref[...] = (_ln_rows(x1) * (1.0 + sc_ref[...]) + sh_ref[...]).astype(h2_ref.dtype)


def _mixout(attn_parts, yg, x2, w_glu_bf, b_glu, g_attn, g_ssm, w_out_bf, gt1, ln_g, ln_b, sc2, sh2, alpha):
    s, d = x2.shape
    aw = attn_parts[0][0].shape[-1]
    sw = yg.shape[1]
    n_pat = len(attn_parts)
    tm = min(512, s)
    row = lambda i: (i, 0)
    fix = lambda i: (0, 0)
    vec = lambda n: pl.BlockSpec((1, n), fix)
    res = lambda a: pl.BlockSpec((a.shape[0], tm // a.shape[0], a.shape[2]), lambda i: (0, i, 0))
    return pl.pallas_call(
        functools.partial(_mixout_kernel, alpha=alpha, n_pat=n_pat),
        grid=(s // tm,),
        in_specs=([res(o) for o, _ in attn_parts] + [res(v) for _, v in attn_parts]
                  + [pl.BlockSpec((tm, sw), row), pl.BlockSpec((tm, d), row),
                     pl.BlockSpec((sw, sw), fix), vec(sw), vec(aw), vec(sw),
                     pl.BlockSpec((aw + sw, d), fix), vec(d), vec(d), vec(d), vec(d), vec(d)]),
        out_specs=[pl.BlockSpec((tm, d), row), pl.BlockSpec((tm, d), row)],
        out_shape=[jax.ShapeDtypeStruct((s, d), jnp.float32), jax.ShapeDtypeStruct((s, d), jnp.bfloat16)],
        scratch_shapes=[pltpu.VMEM((aw // LANES, tm, LANES), jnp.float32),
                        pltpu.VMEM((1, tm, LANES), jnp.float32)],
        compiler_params=_params(("arbitrary",)),
        name="mixout",
    )(*[o for o, _ in attn_parts], *[l for _, l in attn_parts], yg, x2, w_glu_bf, b_glu.reshape(1, -1),
      g_attn.reshape(1, -1), g_ssm.reshape(1, -1), w_out_bf, gt1, ln_g.reshape(1, -1), ln_b.reshape(1, -1),
      sc2, sh2)


def _top_values(sc, k, want_rank=False):
    vals = []
    rank = jnp.full(sc.shape, float(k), jnp.float32) if want_rank else None
    for j in range(k):
        m = jnp.max(sc, axis=0, keepdims=True)
        vals.append(m)
        hit = sc == m
        if want_rank:
            rank = jnp.where(hit, float(j), rank)
        sc = jnp.where(hit, BIG_NEG, sc)
    return vals, rank


def _peerq_kernel(h2_ref, wqp_ref, keys_ref, c0_ref, r1_ref, e0_ref, e1_ref, sc_scr):
    k = PEER_TOPK
    tm = h2_ref.shape[0]
    qp = jnp.dot(h2_ref[...], wqp_ref[...], preferred_element_type=jnp.float32).astype(jnp.bfloat16)
    nt = (((1,), (1,)), ((), ()))
    for hi in range(2 * PEER_HEADS):
        col = hi * PEER_SUB_DIM
        sc_scr[hi] = lax.dot_general(keys_ref[hi % 2], qp[:, col:col + PEER_SUB_DIM], nt,
                                     preferred_element_type=jnp.float32)
    def body(h, carry):
        s0 = sc_scr[2 * h]
        s1 = sc_scr[2 * h + 1]
        top0, _ = _top_values(s0, k)
        top1, rank1 = _top_values(s1, k, want_rank=True)
        cands = [top0[i] + top1[j] for i in range(k) for j in range(k) if (i + 1) * (j + 1) <= k]
        pad = (-len(cands)) % SUBLANES
        cmat = jnp.concatenate(cands + [jnp.full_like(cands[0], BIG_NEG)] * pad, axis=0)
        best, _ = _top_values(cmat, k)
        thr = best[k - 1]
        m0, m1 = top0[0], top1[0]
        zsum = jnp.sum(jnp.where(cmat >= thr, jnp.exp(cmat - (m0 + m1)), 0.0), axis=0, keepdims=True)
        count0 = jnp.zeros_like(s0)
        for j in range(k):
            count0 = jnp.where(s0 + top1[j] >= thr, float(j + 1), count0)
        c0_ref[h] = count0
        r1_ref[h] = rank1.astype(r1_ref.dtype)
        e0_ref[h] = jnp.exp(s0 - m0) / zsum
        e1_ref[h] = jnp.exp(s1 - m1).astype(e1_ref.dtype)
        return carry

    lax.fori_loop(0, PEER_HEADS, body, 0)


def _peerq(h2, w_qp_bf, keys_bf):
    s, d = h2.shape
    tm = min(512, s)
    nk = keys_bf.shape[1]
    stat = pl.BlockSpec((PEER_HEADS, nk, tm), lambda i: (0, 0, i))
    words = jax.ShapeDtypeStruct((PEER_HEADS, nk, s), jnp.float32)
    halfs = jax.ShapeDtypeStruct((PEER_HEADS, nk, s), jnp.bfloat16)
    return pl.pallas_call(
        _peerq_kernel,
        grid=(s // tm,),
        in_specs=[pl.BlockSpec((tm, d), lambda i: (i, 0)),
                  pl.BlockSpec(w_qp_bf.shape, lambda i: (0, 0)),
                  pl.BlockSpec(keys_bf.shape, lambda i: (0, 0, 0))],
        out_specs=[stat, stat, stat, stat],
        out_shape=[words, halfs, words, halfs],
        scratch_shapes=[pltpu.VMEM((2 * PEER_HEADS, nk, tm), jnp.float32)],
        compiler_params=_params(("arbitrary",)),
        name="peerq",
    )(h2, w_qp_bf, keys_bf)


def _peer_kernel(ua_ref, ub_ref, vt_ref, h2t_ref, c0_ref, e0_ref, r1_ref, e1_ref, o_ref,
                 a_s0, a_s1, w_s0, w_s1, *, n_pairs, lane_chunk):
    m = pl.program_id(1)
    nk = PEER_N_KEYS
    te, tm = a_s0.shape
    rows_per_tile = te // nk
    packed = 2 * SUBLANES
    bf = jnp.bfloat16

    def stage_a(u_ref, dst):
        dst[...] = jnp.dot(u_ref[...], h2t_ref[...], preferred_element_type=jnp.float32)

    def row_tile(ref, h, i0, ls):
        return jnp.broadcast_to(ref[h, i0:i0 + 1, ls], (packed, lane_chunk)).astype(bf)

    def stage_b(src, dst, row_off, ls):
        for rr in range(rows_per_tile):
            rows = slice(rr * nk, (rr + 1) * nk)
            i0 = row_off + rr
            a = src[rows, ls]
            act = 0.5 * a * (1.0 + lax.erf(a * (2.0 ** -0.5)))
            gate = jnp.zeros((nk // packed, packed, lane_chunk), bf)
            for h in range(PEER_HEADS):
                c0 = row_tile(c0_ref, h, i0, ls)
                e0 = row_tile(e0_ref, h, i0, ls)
                r1 = r1_ref[h, :, ls].reshape(nk // packed, packed, lane_chunk)
                e1 = e1_ref[h, :, ls].reshape(nk // packed, packed, lane_chunk)
                gate = gate + jnp.where(r1 < c0, e0 * e1, jnp.zeros_like(e1))
            w = gate * act.astype(bf).reshape(nk // packed, packed, lane_chunk)
            dst[rows, ls] = w.reshape(nk, lane_chunk)

    def tile(src, dst, half, row_off):
        for c in range(tm // lane_chunk):
            ls = slice(c * lane_chunk, (c + 1) * lane_chunk)
            stage_b(src, dst, row_off, ls)
            o_ref[:, ls] += jnp.dot(vt_ref[:, half * te:(half + 1) * te], dst[:, ls],
                                    preferred_element_type=jnp.float32)

    @pl.when(m == 0)
    def _():
        o_ref[...] = jnp.zeros_like(o_ref)
        stage_a(ub_ref, a_s0)

    @pl.when(jnp.logical_and(m > 0, m < n_pairs))
    def _():
        stage_a(ua_ref, a_s1)
        tile(a_s0, w_s0, 0, 0)
        stage_a(ub_ref, a_s0)
        tile(a_s1, w_s1, 1, rows_per_tile)

    @pl.when(m == n_pairs)
    def _():
        stage_a(ua_ref, a_s1)
        tile(a_s0, w_s0, 0, 0)
        tile(a_s1, w_s1, 1, rows_per_tile)


def _peer(h2t, u_bf, vt_bf, a0, n1, e0, e1):
    d, s = h2t.shape
    ne = u_bf.shape[0]
    nk = PEER_N_KEYS
    tm = min(512, s)
    te = (SUBLANES // 2) * nk
    n_tiles = ne // te
    n_pairs = n_tiles // 2
    rows = pl.BlockSpec((PEER_HEADS, SUBLANES, tm), lambda i, m: (0, jnp.maximum(m - 1, 0), i))
    full_stat = pl.BlockSpec((PEER_HEADS, nk, tm), lambda i, m: (0, 0, i))
    return pl.pallas_call(
        functools.partial(_peer_kernel, n_pairs=n_pairs, lane_chunk=min(256, tm)),
        grid=(s // tm, n_pairs + 1),
        in_specs=[pl.BlockSpec((te, d), lambda i, m: (jnp.maximum(2 * m - 1, 0), 0)),
                  pl.BlockSpec((te, d), lambda i, m: (jnp.minimum(2 * m, n_tiles - 1), 0)),
                  pl.BlockSpec((d, 2 * te), lambda i, m: (0, jnp.maximum(m - 1, 0))),
                  pl.BlockSpec((d, tm), lambda i, m: (0, i)),
                  rows, rows, full_stat, full_stat],
        out_specs=pl.BlockSpec((d, tm), lambda i, m: (0, i)),
        out_shape=jax.ShapeDtypeStruct((d, s), jnp.float32),
        scratch_shapes=[pltpu.VMEM((te, tm), jnp.float32), pltpu.VMEM((te, tm), jnp.float32),
                        pltpu.VMEM((te, tm), jnp.bfloat16), pltpu.VMEM((te, tm), jnp.bfloat16)],
        compiler_params=_params(("arbitrary", "arbitrary")),
        name="peer",
    )(u_bf, u_bf, vt_bf, h2t, a0, e0, n1, e1)


def _transpose_cast_kernel(v_ref, o_ref):
    o_ref[...] = v_ref[...].T.astype(o_ref.dtype)


def _transpose_cast(v, dtype):
    ne, d = v.shape
    te = min(512, ne)
    return pl.pallas_call(
        _transpose_cast_kernel,
        grid=(ne // te,),
        in_specs=[pl.BlockSpec((te, d), lambda j: (j, 0))],
        out_specs=pl.BlockSpec((d, te), lambda j: (0, j)),
        out_shape=jax.ShapeDtypeStruct((d, ne), dtype),
        compiler_params=_params(("arbitrary",)),
        name="expert_v_t",
    )(v)


def _final_kernel(ft_ref, x1_ref, gt_ref, lg_ref, lb_ref, o_ref, *, alpha):
    ffn = ft_ref[...].T
    t = alpha * x1_ref[...] + (1.0 + gt_ref[...]) * ffn
    o_ref[...] = _ln_rows(t) * lg_ref[...] + lb_ref[...]


def _final(ffn_t, x1, gt2, ln_g, ln_b, alpha):
    s, d = x1.shape
    tm = min(512, s)
    vec = pl.BlockSpec((1, d), lambda i: (0, 0))
    return pl.pallas_call(
        functools.partial(_final_kernel, alpha=alpha),
        grid=(s // tm,),
        in_specs=[pl.BlockSpec((d, tm), lambda i: (0, i)), pl.BlockSpec((tm, d), lambda i: (i, 0)),
                  vec, vec, vec],
        out_specs=pl.BlockSpec((tm, d), lambda i: (i, 0)),
        out_shape=jax.ShapeDtypeStruct((s, d), jnp.float32),
        compiler_params=_params(("arbitrary",)),
        name="final",
    )(ffn_t, x1, gt2, ln_g.reshape(1, -1), ln_b.reshape(1, -1))


def kernel(x, c, positions, w_ada, b_ada, w_in, lam_re, lam_im, log_dt, ssm_b_re, ssm_b_im, ssm_c_re,
           ssm_c_im, ssm_d, w_glu, b_glu, g_attn, g_ssm, w_out, ln1_g, ln1_b, w_qp, sub_keys, expert_u,
           expert_v, ln2_g, ln2_b):
    b, s, d = x.shape
    assert b == 1, "one sequence per call"
    depth = w_ada.shape[0]
    alpha = (2.0 * depth) ** 0.25
    bf = jnp.bfloat16
    x2 = x.reshape(s, d)
    pos = positions.reshape(s, 1)
    dilations = [dil for _, dil in DILATED_PATTERNS if dil > 1]
    for l in range(depth):
        mod = _ada(c, w_ada[l], b_ada[l])
        sh1, sc1, gt1, sh2, sc2, gt2 = [mod[:, k * d:(k + 1) * d] for k in range(ADA_CHUNKS)]
        z, *z_res = _inproj(x2, pos, sc1, sh1, w_in[l].astype(bf), dilations)
        z_dil = [z.reshape(1, s, -1) if dil == 1 else z_res[dilations.index(dil)]
                 for _, dil in DILATED_PATTERNS]
        attn = _attention(z, z_dil)
        yg = _s5(z, lam_re[l], lam_im[l], log_dt[l], ssm_b_re[l], ssm_b_im[l],
                 ssm_c_re[l], ssm_c_im[l], ssm_d[l])
        x1, h2 = _mixout(attn, yg, x2, w_glu[l].astype(bf), b_glu[l], g_attn[l], g_ssm[l],
                         w_out[l].astype(bf), gt1, ln1_g[l], ln1_b[l], sc2, sh2, alpha)
        a0, n1, e0, e1 = _peerq(h2, w_qp[l].astype(bf), sub_keys[l].astype(bf))
        ffn_t = _peer(h2.T, expert_u[l].astype(bf), _transpose_cast(expert_v[l], bf), a0, n1, e0, e1)
        x2 = _final(ffn_t, x1, gt2, ln2_g[l], ln2_b[l], alpha)
    return x2.reshape(b, s, d)
```

```python
import functools
import math

import jax
import jax.numpy as jnp
from jax import lax
from jax.experimental import pallas as pl
from jax.experimental.pallas import tpu as pltpu

ATTN_HEADS = 16
HEAD_DIM = 64
ATTN_WIDTH = ATTN_HEADS * HEAD_DIM
SSM_GROUP = 16
SSM_STATE = 64
DILATED_PATTERNS = ((128, 1), (512, 4), (2048, 16))
ATTN_BLOCK = 128
ROPE_THETA = 10000.0
NEG_INF = -1e30
PEER_HEADS = 8
PEER_SUB_DIM = 128
PEER_N_KEYS = 128
PEER_TOPK = 16
LN_EPS = 1e-5
ADA_CHUNKS = 6

LANES = 128
SUBLANES = 8
VMEM_LIMIT_BYTES = 56 * 1024 * 1024

S5_CHUNK = 16
GROUPS_PER_BLOCK = LANES // SSM_GROUP
STATE_PER_BLOCK = GROUPS_PER_BLOCK * SSM_STATE
BIG_NEG = -3.0e38
QK_LOG2_SCALE = HEAD_DIM ** -0.5 * math.log2(math.e)


def _params(sem):
    return pltpu.CompilerParams(dimension_semantics=sem, vmem_limit_bytes=VMEM_LIMIT_BYTES)


def _ln_rows(t):
    mu = jnp.mean(t, axis=-1, keepdims=True)
    d = t - mu
    var = jnp.mean(d * d, axis=-1, keepdims=True)
    return d * lax.rsqrt(var + LN_EPS)


def _ada_kernel(c_ref, w_ref, b_ref, o_ref):
    c = c_ref[...]
    sc = c * jax.nn.sigmoid(c)
    o_ref[...] = jnp.sum(sc * w_ref[...], axis=0, keepdims=True) + b_ref[...]


def _ada(c, w_ada, b_ada):
    d = c.shape[-1]
    n = w_ada.shape[-1]
    tn = n // 8
    return pl.pallas_call(
        _ada_kernel,
        grid=(n // tn,),
        in_specs=[pl.BlockSpec((d, 1), lambda j: (0, 0)),
                  pl.BlockSpec((d, tn), lambda j: (0, j)),
                  pl.BlockSpec((1, tn), lambda j: (0, j))],
        out_specs=pl.BlockSpec((1, tn), lambda j: (0, j)),
        out_shape=jax.ShapeDtypeStruct((1, n), jnp.float32),
        compiler_params=_params(("arbitrary",)),
        name="ada",
    )(c.reshape(d, 1), w_ada, b_ada.reshape(1, n))


def _inproj_kernel(x_ref, pos_ref, sc_ref, sh_ref, w_ref, z_ref, *rest, n_rope, dilations):
    dil_refs = rest[:len(dilations)]
    h_scr, cos_scr, sin_scr, z_scr, stage_scr = rest[len(dilations):]
    j = pl.program_id(1)
    tm, tn = z_ref.shape

    @pl.when(j == 0)
    def _():
        h = _ln_rows(x_ref[...]) * (1.0 + sc_ref[...]) + sh_ref[...]
        h_scr[...] = h.astype(jnp.bfloat16)
        lane = lax.broadcasted_iota(jnp.int32, (1, LANES), 1)
        fi = ((lane % HEAD_DIM) % (HEAD_DIM // 2)).astype(jnp.float32)
        inv_freq = jnp.exp(fi * (-math.log(ROPE_THETA) / (HEAD_DIM // 2)))
        ang = pos_ref[...].astype(jnp.float32) * inv_freq
        first_half = (lane % HEAD_DIM) < (HEAD_DIM // 2)
        cos_scr[...] = jnp.cos(ang)
        sin_scr[...] = jnp.where(first_half, -jnp.sin(ang), jnp.sin(ang))

    z = jnp.dot(h_scr[...], w_ref[...], preferred_element_type=jnp.float32)

    @pl.when(j < n_rope)
    def _():
        lane = lax.broadcasted_iota(jnp.int32, (1, LANES), 1)
        first_half = (lane % HEAD_DIM) < (HEAD_DIM // 2)
        q_mult = jnp.where(j < n_rope // 2, QK_LOG2_SCALE, 1.0)
        cos = cos_scr[...] * q_mult
        sin = sin_scr[...] * q_mult
        for cc in range(tn // LANES):
            zc = z[:, cc * LANES:(cc + 1) * LANES]
            partner = jnp.where(first_half, pltpu.roll(zc, LANES - HEAD_DIM // 2, 1),
                                pltpu.roll(zc, HEAD_DIM // 2, 1))
            z_scr[cc] = zc * cos + partner * sin

    @pl.when(j >= n_rope)
    def _():
        for cc in range(tn // LANES):
            z_scr[cc] = z[:, cc * LANES:(cc + 1) * LANES]

    for cc in range(tn // LANES):
        z_ref[:, cc * LANES:(cc + 1) * LANES] = z_scr[cc].astype(z_ref.dtype)

    @pl.when(j < (3 * ATTN_WIDTH) // tn)
    def _():
        base = 1
        for dil, ref in zip(dilations, dil_refs):
            step = dil // base if dil % base == 0 else dil
            for r in range(dil):
                for cc in range(tn // LANES):
                    if step == dil:
                        rows = z_scr[cc, pl.ds(r, tm // dil, stride=dil), :]
                    else:
                        rows = stage_scr[cc, pl.ds((r % base) * (tm // base) + r // base, tm // dil, stride=step), :]
                    ref[r, :, cc * LANES:(cc + 1) * LANES] = rows.astype(ref.dtype)
                    if dil != dilations[-1]:
                        stage_scr[cc, r * (tm // dil):(r + 1) * (tm // dil), :] = rows
            base = dil


def _inproj(x2, pos, sc1, sh1, w_in_bf, dilations):
    s, d = x2.shape
    n = w_in_bf.shape[1]
    tm = min(1024, s)
    tn = 512
    n_rope = (2 * ATTN_WIDTH) // tn
    n_qkv = (3 * ATTN_WIDTH) // tn
    bf = jnp.bfloat16
    dil_specs = [pl.BlockSpec((dil, tm // dil, tn), lambda i, j: (0, i, jnp.minimum(j, n_qkv - 1)))
                 for dil in dilations]
    dil_shapes = [jax.ShapeDtypeStruct((dil, s // dil, 3 * ATTN_WIDTH), bf) for dil in dilations]
    return pl.pallas_call(
        functools.partial(_inproj_kernel, n_rope=n_rope, dilations=tuple(dilations)),
        grid=(s // tm, n // tn),
        in_specs=[pl.BlockSpec((tm, d), lambda i, j: (i, 0)),
                  pl.BlockSpec((tm, 1), lambda i, j: (i, 0)),
                  pl.BlockSpec((1, d), lambda i, j: (0, 0)),
                  pl.BlockSpec((1, d), lambda i, j: (0, 0)),
                  pl.BlockSpec((d, tn), lambda i, j: (0, j))],
        out_specs=[pl.BlockSpec((tm, tn), lambda i, j: (i, j))] + dil_specs,
        out_shape=[jax.ShapeDtypeStruct((s, n), bf)] + dil_shapes,
        scratch_shapes=[pltpu.VMEM((tm, d), bf),
                        pltpu.VMEM((tm, LANES), jnp.float32),
                        pltpu.VMEM((tm, LANES), jnp.float32),
                        pltpu.VMEM((tn // LANES, tm, LANES), jnp.float32),
                        pltpu.VMEM((tn // LANES, tm, LANES), jnp.float32)],
        compiler_params=_params(("arbitrary", "arbitrary")),
        name="inproj",
    )(x2, pos, sc1, sh1, w_in_bf)


def _attn_kernel(q_ref, kc_ref, vc_ref, kp_ref, vp_ref, o_ref, lse_ref, kwin, vt_win, *, nblk):
    i = pl.program_id(1)
    blk = ATTN_BLOCK
    kwin[0:blk, :] = kp_ref[...]
    kwin[blk:, :] = kc_ref[...]
    vt_win[:, 0:blk] = vp_ref[...].T
    vt_win[:, blk:] = vc_ref[...].T

    key = lax.broadcasted_iota(jnp.int32, (2 * blk, blk), 0)
    qry = lax.broadcasted_iota(jnp.int32, (2 * blk, blk), 1)
    lane = lax.broadcasted_iota(jnp.int32, (blk, LANES), 1)
    row = lax.broadcasted_iota(jnp.int32, (blk, LANES), 0)
    nt = (((1,), (1,)), ((), ()))

    def body(j, carry):
        r0 = pl.multiple_of(j * blk, blk)
        first_key = jnp.where((i * nblk + j) > 0, 0, blk)
        mask = jnp.logical_and(key >= jnp.maximum(qry, first_key), key <= qry + blk)
        lse_rows = []
        for hp in range(ATTN_HEADS // 2):
            cs = slice(hp * LANES, (hp + 1) * LANES)
            q2 = q_ref[pl.ds(r0, blk), cs]
            k2 = kwin[pl.ds(r0, 2 * blk), cs]
            vt2 = vt_win[cs, pl.ds(r0, 2 * blk)]
            halves = []
            for hd in range(2):
                in_head = (lane < HEAD_DIM) if hd == 0 else (lane >= HEAD_DIM)
                qm = jnp.where(in_head, q2, jnp.zeros_like(q2))
                st = lax.dot_general(k2, qm, nt, preferred_element_type=jnp.float32)
                st = jnp.where(mask, st, NEG_INF)
                m = jnp.max(st, axis=0, keepdims=True)
                p = jnp.exp2(st - m)
                l = jnp.sum(p, axis=0, keepdims=True)
                ot = jnp.dot(vt2, p.astype(vt2.dtype), preferred_element_type=jnp.float32)
                halves.append(ot / l)
                lse_rows.append(m * math.log(2.0) + jnp.log(l))
            ot2 = jnp.where(row < HEAD_DIM, halves[0], halves[1])
            o_ref[pl.ds(r0, blk), cs] = ot2.T.astype(o_ref.dtype)
        lse_mat = jnp.zeros((LANES, blk), jnp.float32)
        for hidx, lse_h in enumerate(lse_rows):
            lse_mat = jnp.where(row == hidx, lse_h, lse_mat)
        lse_ref[pl.ds(r0, blk), :] = lse_mat.T
        return carry

    lax.fori_loop(0, nblk, body, 0)


def _attn_pattern(zd, dilation):
    _, length, zw = zd.shape
    aw = ATTN_WIDTH
    rows = min(1024, length)
    nblk = rows // ATTN_BLOCK
    cur = lambda off: pl.BlockSpec((None, rows, aw), lambda r, i: (r, i, off))
    prev = lambda off: pl.BlockSpec((None, ATTN_BLOCK, aw), lambda r, i: (r, jnp.maximum(i * nblk - 1, 0), off))
    return pl.pallas_call(
        functools.partial(_attn_kernel, nblk=nblk),
        grid=(dilation, length // rows),
        in_specs=[cur(0), cur(1), cur(2), prev(1), prev(2)],
        out_specs=[pl.BlockSpec((None, rows, aw), lambda r, i: (r, i, 0)),
                   pl.BlockSpec((None, rows, LANES), lambda r, i: (r, i, 0))],
        out_shape=[jax.ShapeDtypeStruct((dilation, length, aw), jnp.bfloat16),
                   jax.ShapeDtypeStruct((dilation, length, LANES), jnp.float32)],
        scratch_shapes=[pltpu.VMEM((rows + ATTN_BLOCK, aw), jnp.bfloat16),
                        pltpu.VMEM((aw, rows + ATTN_BLOCK), jnp.bfloat16)],
        compiler_params=_params(("arbitrary", "arbitrary")),
        name=f"attn_d{dilation}",
    )(zd, zd, zd, zd, zd)


def _attention(z, z_dil):
    outs = []
    for (window, dilation), zd in zip(DILATED_PATTERNS, z_dil):
        assert window // dilation == ATTN_BLOCK
        outs.append(_attn_pattern(zd, dilation))
    return outs


def _s5_param_kernel(lr_r, li_r, ld_r, bre_ref, bim_ref, cre_ref, cim_ref,
                     krev_ref, pblk_ref, qblk_ref, at_ref):
    t_chunk = S5_CHUNK
    nst = STATE_PER_BLOCK
    lr, li, ld = lr_r[...], li_r[...], ld_r[...]
    dt = jnp.exp(ld)
    mag = jnp.exp(lr * dt)
    ar, ai = mag * jnp.cos(li * dt), mag * jnp.sin(li * dt)
    den = lr * lr + li * li
    cr = ((ar - 1.0) * lr + ai * li) / den
    ci = (ai * lr - (ar - 1.0) * li) / den
    bre, bim = bre_ref[...], bim_ref[...]
    bbr = cr * bre - ci * bim
    bbi = cr * bim + ci * bre
    cre, cim = cre_ref[...], cim_ref[...]
    cre_b, cim_b = cre.astype(pblk_ref.dtype), cim.astype(pblk_ref.dtype)
    nt = (((1,), (1,)), ((), ()))
    er, ei = jnp.ones_like(ar), jnp.zeros_like(ai)
    for tau in range(t_chunk):
        pr = er * bbr - ei * bbi
        pi = er * bbi + ei * bbr
        blk = t_chunk - 1 - tau
        rows = slice(blk * LANES, (blk + 1) * LANES)
        prb, pib = pr.astype(pblk_ref.dtype), pi.astype(pblk_ref.dtype)
        pblk_ref[rows, 0:nst] = prb
        pblk_ref[rows, nst:2 * nst] = pib
        k_tau = (lax.dot_general(prb, cre_b, nt, preferred_element_type=jnp.float32)
                 - lax.dot_general(pib, cim_b, nt, preferred_element_type=jnp.float32))
        krev_ref[rows, :] = k_tau.astype(krev_ref.dtype)
        er, ei = er * ar - ei * ai, er * ai + ei * ar
        qblk_ref[tau, :, 0:nst] = (cre * er - cim * ei).astype(qblk_ref.dtype)
        qblk_ref[tau, :, nst:2 * nst] = (-(cre * ei + cim * er)).astype(qblk_ref.dtype)
    krev_ref[t_chunk * LANES:, :] = jnp.zeros((LANES, LANES), krev_ref.dtype)
    at_ref[:, 0:nst] = er
    at_ref[:, nst:2 * nst] = ei


def _s5_params(lam_re, lam_im, log_dt, b_re, b_im, c_re, c_im):
    g, n = lam_re.shape
    p = b_re.shape[-1]
    gb = GROUPS_PER_BLOCK
    nlb = g // gb
    nst = gb * n
    f32 = jnp.float32
    eye = jnp.eye(gb, dtype=f32)

    def row(a):
        return a.astype(f32).reshape(nlb, 1, nst)

    ldt = jnp.broadcast_to(log_dt.astype(f32)[:, None], (g, n))

    def b_blockdiag(b):
        b4 = b.astype(f32).reshape(nlb, gb, n, p)
        return jnp.einsum("lhnq,gh->lgqhn", b4, eye).reshape(nlb, gb * p, nst)

    def c_blockdiag(c):
        c4 = c.astype(f32).reshape(nlb, gb, p, n)
        return jnp.einsum("lhpn,gh->lgphn", c4, eye).reshape(nlb, gb * p, nst)

    tc = S5_CHUNK
    vec_r = pl.BlockSpec((None, 1, nst), lambda l: (l, 0, 0))
    mat = pl.BlockSpec((None, LANES, nst), lambda l: (l, 0, 0))
    return pl.pallas_call(
        _s5_param_kernel,
        grid=(nlb,),
        in_specs=[vec_r, vec_r, vec_r, mat, mat, mat, mat],
        out_specs=[pl.BlockSpec((None, (tc + 1) * LANES, LANES), lambda l: (l, 0, 0)),
                   pl.BlockSpec((None, tc * LANES, 2 * nst), lambda l: (l, 0, 0)),
                   pl.BlockSpec((None, tc, LANES, 2 * nst), lambda l: (l, 0, 0, 0)),
                   pl.BlockSpec((None, 1, 2 * nst), lambda l: (l, 0, 0))],
        out_shape=[jax.ShapeDtypeStruct((nlb, (tc + 1) * LANES, LANES), jnp.bfloat16),
                   jax.ShapeDtypeStruct((nlb, tc * LANES, 2 * nst), jnp.bfloat16),
                   jax.ShapeDtypeStruct((nlb, tc, LANES, 2 * nst), jnp.bfloat16),
                   jax.ShapeDtypeStruct((nlb, 1, 2 * nst), f32)],
        compiler_params=_params(("arbitrary",)),
        name="s5_params",
    )(row(lam_re), row(lam_im), row(ldt),
      b_blockdiag(b_re), b_blockdiag(b_im), c_blockdiag(c_re), c_blockdiag(c_im))


def _load_ucat(u_ref, uf_scr, ucat_scr, nc):
    uf_scr[...] = u_ref[...].astype(jnp.float32)
    for s in range(S5_CHUNK):
        ucat_scr[:, s * LANES:(s + 1) * LANES] = (
            uf_scr[pl.ds(s, nc, stride=S5_CHUNK), :].astype(ucat_scr.dtype))


def _s5_state_in_kernel(u_ref, pblk_ref, b_ref, uf_scr, ucat_scr, *, nc):
    _load_ucat(u_ref, uf_scr, ucat_scr, nc)
    b_ref[...] = jnp.dot(ucat_scr[...], pblk_ref[...], preferred_element_type=jnp.float32)


def _s5_scan_kernel(b_ref, at_ref, h_ref, *, nc, nlb):
    nst = STATE_PER_BLOCK
    a_re = at_ref[:, 0:nst]
    a_im = at_ref[:, nst:2 * nst]

    def body(c, carry):
        hr, hi = carry
        h_ref[c, :, 0:nst] = hr
        h_ref[c, :, nst:2 * nst] = hi
        bc = b_ref[c]
        return (a_re * hr - a_im * hi + bc[:, 0:nst], a_re * hi + a_im * hr + bc[:, nst:2 * nst])

    zero = jnp.zeros((nlb, nst), jnp.float32)
    lax.fori_loop(0, nc, body, (zero, zero))


def _s5_out_kernel(u_ref, h_ref, krev_ref, qblk_ref, dsk_ref, y_ref, uf_scr, ucat_scr, y_scr, *, nc):
    _load_ucat(u_ref, uf_scr, ucat_scr, nc)
    hb = h_ref[...].astype(jnp.bfloat16)
    dsk = dsk_ref[...]
    tc = S5_CHUNK
    nt = (((1,), (1,)), ((), ()))
    for t in range(0, tc, 2):
        kd = (t + 2) * LANES
        r_t = (tc - 1 - t) * LANES
        taps = jnp.concatenate([krev_ref[r_t:r_t + kd, :], krev_ref[r_t - LANES:r_t - LANES + kd, :]], axis=1)
        y2 = jnp.dot(ucat_scr[:, 0:kd], taps, preferred_element_type=jnp.float32)
        carry_in = qblk_ref[t:t + 2].reshape(2 * LANES, qblk_ref.shape[2])
        y2 = y2 + lax.dot_general(hb, carry_in, nt, preferred_element_type=jnp.float32)
        for s in range(2):
            y = y2[:, s * LANES:(s + 1) * LANES] + dsk * uf_scr[pl.ds(t + s, nc, stride=tc), :]
            y = 0.5 * y * (1.0 + lax.erf(y * (2.0 ** -0.5)))
            y_scr[pl.ds(t + s, nc, stride=tc), :] = y
    y_ref[...] = y_scr[...].astype(y_ref.dtype)


def _s5(z, lam_re, lam_im, log_dt, b_re, b_im, c_re, c_im, d_skip):
    s = z.shape[0]
    nlb = lam_re.shape[0] // GROUPS_PER_BLOCK
    nc = s // S5_CHUNK
    nst2 = 2 * STATE_PER_BLOCK
    u_col0 = (3 * ATTN_WIDTH) // LANES
    krev, pblk, qblk, at = _s5_params(lam_re, lam_im, log_dt, b_re, b_im, c_re, c_im)
    u_spec = pl.BlockSpec((s, LANES), lambda l: (0, u_col0 + l))
    b2 = pl.pallas_call(
        functools.partial(_s5_state_in_kernel, nc=nc),
        grid=(nlb,),
        in_specs=[u_spec, pl.BlockSpec((None, S5_CHUNK * LANES, nst2), lambda l: (l, 0, 0))],
        out_specs=pl.BlockSpec((nc, nst2), lambda l: (0, l)),
        out_shape=jax.ShapeDtypeStruct((nc, nlb * nst2), jnp.float32),
        scratch_shapes=[pltpu.VMEM((s, LANES), jnp.float32),
                        pltpu.VMEM((nc, S5_CHUNK * LANES), jnp.bfloat16)],
        compiler_params=_params(("arbitrary",)),
        name="s5_state_in",
    )(z, pblk)
    h3 = pl.pallas_call(
        functools.partial(_s5_scan_kernel, nc=nc, nlb=nlb),
        out_shape=jax.ShapeDtypeStruct((nc, nlb, nst2), jnp.float32),
        compiler_params=pltpu.CompilerParams(vmem_limit_bytes=VMEM_LIMIT_BYTES),
        name="s5_scan",
    )(b2.reshape(nc, nlb, nst2), at.reshape(nlb, nst2))
    return pl.pallas_call(
        functools.partial(_s5_out_kernel, nc=nc),
        grid=(nlb,),
        in_specs=[u_spec,
                  pl.BlockSpec((nc, nst2), lambda l: (0, l)),
                  pl.BlockSpec((None, (S5_CHUNK + 1) * LANES, LANES), lambda l: (l, 0, 0)),
                  pl.BlockSpec((None, S5_CHUNK, LANES, nst2), lambda l: (l, 0, 0, 0)),
                  pl.BlockSpec((1, LANES), lambda l: (0, l))],
        out_specs=pl.BlockSpec((s, LANES), lambda l: (0, l)),
        out_shape=jax.ShapeDtypeStruct((s, nlb * LANES), jnp.bfloat16),
        scratch_shapes=[pltpu.VMEM((s, LANES), jnp.float32),
                        pltpu.VMEM((nc, S5_CHUNK * LANES), jnp.bfloat16),
                        pltpu.VMEM((s, LANES), jnp.float32)],
        compiler_params=_params(("arbitrary",)),
        name="s5_out",
    )(z, h3.reshape(nc, nlb * nst2), krev, qblk, d_skip.astype(jnp.float32).reshape(1, -1))


def _mixout_kernel(*refs, alpha, n_pat):
    o_refs = refs[:n_pat]
    lse_refs = refs[n_pat:2 * n_pat]
    (yg_ref, x_ref, wglu_ref, bglu_ref, ga_ref, gs_ref, wout_ref,
     gt_ref, lg_ref, lb_ref, sc_ref, sh_ref, x1_ref, h2_ref, o_scr, lse_scr) = refs[2 * n_pat:]
    planes, tm, _ = o_scr.shape
    aw = planes * LANES

    def token_major(ref, scr):
        dil = ref.shape[0]
        if dil == 1:
            return ref[0].astype(jnp.float32)
        n_planes = ref.shape[2] // LANES
        for r in range(dil):
            for cc in range(n_planes):
                scr[cc, pl.ds(r, tm // dil, stride=dil), :] = (
                    ref[r, :, cc * LANES:(cc + 1) * LANES].astype(jnp.float32))
        return jnp.concatenate([scr[cc] for cc in range(n_planes)], axis=1)

    lses = [token_major(ref, lse_scr) for ref in lse_refs]
    top = functools.reduce(jnp.maximum, lses)
    ws = [jnp.exp(v - top) for v in lses]
    inv = 1.0 / functools.reduce(lambda u, v: u + v, ws)
    head_of_lane = lax.broadcasted_iota(jnp.int32, (LANES, aw), 1) // HEAD_DIM
    spread = (lax.broadcasted_iota(jnp.int32, (LANES, aw), 0) == head_of_lane).astype(jnp.bfloat16)
    spread2 = jnp.concatenate([spread, spread], axis=0)
    a = jnp.zeros((tm, aw), jnp.float32)
    for w, o_ref in zip(ws, o_refs):
        w = w * inv
        w_hi = w.astype(jnp.bfloat16)
        w_lo = (w - w_hi.astype(jnp.float32)).astype(jnp.bfloat16)
        wide = jnp.dot(jnp.concatenate([w_hi, w_lo], axis=1), spread2, preferred_element_type=jnp.float32)
        a = a + wide * token_major(o_ref, o_scr)
    ra = a * lax.rsqrt(jnp.mean(a * a, axis=-1, keepdims=True) + LN_EPS) * ga_ref[...]
    yg = yg_ref[...]
    y = yg.astype(jnp.float32)
    gate = jax.nn.sigmoid(jnp.dot(yg, wglu_ref[...], preferred_element_type=jnp.float32) + bglu_ref[...])
    y = y * gate
    ry = y * lax.rsqrt(jnp.mean(y * y, axis=-1, keepdims=True) + LN_EPS) * gs_ref[...]
    mix = (jnp.dot(ra.astype(jnp.bfloat16), wout_ref[0:aw, :], preferred_element_type=jnp.float32)
           + jnp.dot(ry.astype(jnp.bfloat16), wout_ref[aw:, :], preferred_element_type=jnp.float32))
    t = alpha * x_ref[...] + (1.0 + gt_ref[...]) * mix
    x1 = _ln_rows(t) * lg_ref[...] + lb_ref[...]
    x1_ref[...] = x1
    h2_ref[...] = (_ln_rows(x1) * (1.0 + sc_ref[...]) + sh_ref[...]).astype(h2_ref.dtype)


def _mixout(attn_parts, yg, x2, w_glu_bf, b_glu, g_attn, g_ssm, w_out_bf, gt1, ln_g, ln_b, sc2, sh2, alpha):
    s, d = x2.shape
    aw = attn_parts[0][0].shape[-1]
    sw = yg.shape[1]
    n_pat = len(attn_parts)
    tm = min(512, s)
    row = lambda i: (i, 0)
    fix = lambda i: (0, 0)
    vec = lambda n: pl.BlockSpec((1, n), fix)
    res = lambda a: pl.BlockSpec((a.shape[0], tm // a.shape[0], a.shape[2]), lambda i: (0, i, 0))
    return pl.pallas_call(
        functools.partial(_mixout_kernel, alpha=alpha, n_pat=n_pat),
        grid=(s // tm,),
        in_specs=([res(o) for o, _ in attn_parts] + [res(v) for _, v in attn_parts]
                  + [pl.BlockSpec((tm, sw), row), pl.BlockSpec((tm, d), row),
                     pl.BlockSpec((sw, sw), fix), vec(sw), vec(aw), vec(sw),
                     pl.BlockSpec((aw + sw, d), fix), vec(d), vec(d), vec(d), vec(d), vec(d)]),
        out_specs=[pl.BlockSpec((tm, d), row), pl.BlockSpec((tm, d), row)],
        out_shape=[jax.ShapeDtypeStruct((s, d), jnp.float32), jax.ShapeDtypeStruct((s, d), jnp.bfloat16)],
        scratch_shapes=[pltpu.VMEM((aw // LANES, tm, LANES), jnp.float32),
                        pltpu.VMEM((1, tm, LANES), jnp.float32)],
        compiler_params=_params(("arbitrary",)),
        name="mixout",
    )(*[o for o, _ in attn_parts], *[l for _, l in attn_parts], yg, x2, w_glu_bf, b_glu.reshape(1, -1),
      g_attn.reshape(1, -1), g_ssm.reshape(1, -1), w_out_bf, gt1, ln_g.reshape(1, -1), ln_b.reshape(1, -1),
      sc2, sh2)


def _top_values(sc, k, want_rank=False):
    vals = []
    rank = jnp.full(sc.shape, float(k), jnp.float32) if want_rank else None
    for j in range(k):
        m = jnp.max(sc, axis=0, keepdims=True)
        vals.append(m)
        hit = sc == m
        if want_rank:
            rank = jnp.where(hit, float(j), rank)
        sc = jnp.where(hit, BIG_NEG, sc)
    return vals, rank


def _peerq_kernel(h2_ref, wqp_ref, keys_ref, c0_ref, r1_ref, e0_ref, e1_ref, sc_scr):
    k = PEER_TOPK
    tm = h2_ref.shape[0]
    qp = jnp.dot(h2_ref[...], wqp_ref[...], preferred_element_type=jnp.float32).astype(jnp.bfloat16)
    nt = (((1,), (1,)), ((), ()))
    for hi in range(2 * PEER_HEADS):
        col = hi * PEER_SUB_DIM
        sc_scr[hi] = lax.dot_general(keys_ref[hi % 2], qp[:, col:col + PEER_SUB_DIM], nt,
                                     preferred_element_type=jnp.float32)
    def body(h, carry):
        s0 = sc_scr[2 * h]
        s1 = sc_scr[2 * h + 1]
        top0, _ = _top_values(s0, k)
        top1, rank1 = _top_values(s1, k, want_rank=True)
        cands = [top0[i] + top1[j] for i in range(k) for j in range(k) if (i + 1) * (j + 1) <= k]
        pad = (-len(cands)) % SUBLANES
        cmat = jnp.concatenate(cands + [jnp.full_like(cands[0], BIG_NEG)] * pad, axis=0)
        best, _ = _top_values(cmat, k)
        thr = best[k - 1]
        m0, m1 = top0[0], top1[0]
        zsum = jnp.sum(jnp.where(cmat >= thr, jnp.exp(cmat - (m0 + m1)), 0.0), axis=0, keepdims=True)
        count0 = jnp.zeros_like(s0)
        for j in range(k):
            count0 = jnp.where(s0 + top1[j] >= thr, float(j + 1), count0)
        c0_ref[h] = count0
        r1_ref[h] = rank1.astype(r1_ref.dtype)
        e0_ref[h] = jnp.exp(s0 - m0) / zsum
        e1_ref[h] = jnp.exp(s1 - m1).astype(e1_ref.dtype)
        return carry

    lax.fori_loop(0, PEER_HEADS, body, 0)


def _peerq(h2, w_qp_bf, keys_bf):
    s, d = h2.shape
    tm = min(512, s)
    nk = keys_bf.shape[1]
    stat = pl.BlockSpec((PEER_HEADS, nk, tm), lambda i: (0, 0, i))
    words = jax.ShapeDtypeStruct((PEER_HEADS, nk, s), jnp.float32)
    halfs = jax.ShapeDtypeStruct((PEER_HEADS, nk, s), jnp.bfloat16)
    return pl.pallas_call(
        _peerq_kernel,
        grid=(s // tm,),
        in_specs=[pl.BlockSpec((tm, d), lambda i: (i, 0)),
                  pl.BlockSpec(w_qp_bf.shape, lambda i: (0, 0)),
                  pl.BlockSpec(keys_bf.shape, lambda i: (0, 0, 0))],
        out_specs=[stat, stat, stat, stat],
        out_shape=[words, halfs, words, halfs],
        scratch_shapes=[pltpu.VMEM((2 * PEER_HEADS, nk, tm), jnp.float32)],
        compiler_params=_params(("arbitrary",)),
        name="peerq",
    )(h2, w_qp_bf, keys_bf)


def _peer_kernel(ua_ref, ub_ref, vt_ref, h2t_ref, c0_ref, e0_ref, r1_ref, e1_ref, o_ref,
                 a_s0, a_s1, w_s0, w_s1, *, n_pairs, lane_chunk):
    m = pl.program_id(1)
    nk = PEER_N_KEYS
    te, tm = a_s0.shape
    rows_per_tile = te // nk
    packed = 2 * SUBLANES
    bf = jnp.bfloat16

    def stage_a(u_ref, dst):
        dst[...] = jnp.dot(u_ref[...], h2t_ref[...], preferred_element_type=jnp.float32)

    def row_tile(ref, h, i0, ls):
        return jnp.broadcast_to(ref[h, i0:i0 + 1, ls], (packed, lane_chunk)).astype(bf)

    def stage_b(src, dst, row_off, ls):
        for rr in range(rows_per_tile):
            rows = slice(rr * nk, (rr + 1) * nk)
            i0 = row_off + rr
            a = src[rows, ls]
            act = 0.5 * a * (1.0 + lax.erf(a * (2.0 ** -0.5)))
            gate = jnp.zeros((nk // packed, packed, lane_chunk), bf)
            for h in range(PEER_HEADS):
                c0 = row_tile(c0_ref, h, i0, ls)
                e0 = row_tile(e0_ref, h, i0, ls)
                r1 = r1_ref[h, :, ls].reshape(nk // packed, packed, lane_chunk)
                e1 = e1_ref[h, :, ls].reshape(nk // packed, packed, lane_chunk)
                gate = gate + jnp.where(r1 < c0, e0 * e1, jnp.zeros_like(e1))
            w = gate * act.astype(bf).reshape(nk // packed, packed, lane_chunk)
            dst[rows, ls] = w.reshape(nk, lane_chunk)

    def tile(src, dst, half, row_off):
        for c in range(tm // lane_chunk):
            ls = slice(c * lane_chunk, (c + 1) * lane_chunk)
            stage_b(src, dst, row_off, ls)
            o_ref[:, ls] += jnp.dot(vt_ref[:, half * te:(half + 1) * te], dst[:, ls],
                                    preferred_element_type=jnp.float32)

    @pl.when(m == 0)
    def _():
        o_ref[...] = jnp.zeros_like(o_ref)
        stage_a(ub_ref, a_s0)

    @pl.when(jnp.logical_and(m > 0, m < n_pairs))
    def _():
        stage_a(ua_ref, a_s1)
        tile(a_s0, w_s0, 0, 0)
        stage_a(ub_ref, a_s0)
        tile(a_s1, w_s1, 1, rows_per_tile)

    @pl.when(m == n_pairs)
    def _():
        stage_a(ua_ref, a_s1)
        tile(a_s0, w_s0, 0, 0)
        tile(a_s1, w_s1, 1, rows_per_tile)


def _peer(h2t, u_bf, vt_bf, a0, n1, e0, e1):
    d, s = h2t.shape
    ne = u_bf.shape[0]
    nk = PEER_N_KEYS
    tm = min(512, s)
    te = (SUBLANES // 2) * nk
    n_tiles = ne // te
    n_pairs = n_tiles // 2
    rows = pl.BlockSpec((PEER_HEADS, SUBLANES, tm), lambda i, m: (0, jnp.maximum(m - 1, 0), i))
    full_stat = pl.BlockSpec((PEER_HEADS, nk, tm), lambda i, m: (0, 0, i))
    return pl.pallas_call(
        functools.partial(_peer_kernel, n_pairs=n_pairs, lane_chunk=min(256, tm)),
        grid=(s // tm, n_pairs + 1),
        in_specs=[pl.BlockSpec((te, d), lambda i, m: (jnp.maximum(2 * m - 1, 0), 0)),
                  pl.BlockSpec((te, d), lambda i, m: (jnp.minimum(2 * m, n_tiles - 1), 0)),
                  pl.BlockSpec((d, 2 * te), lambda i, m: (0, jnp.maximum(m - 1, 0))),
                  pl.BlockSpec((d, tm), lambda i, m: (0, i)),
                  rows, rows, full_stat, full_stat],
        out_specs=pl.BlockSpec((d, tm), lambda i, m: (0, i)),
        out_shape=jax.ShapeDtypeStruct((d, s), jnp.float32),
        scratch_shapes=[pltpu.VMEM((te, tm), jnp.float32), pltpu.VMEM((te, tm), jnp.float32),
                        pltpu.VMEM((te, tm), jnp.bfloat16), pltpu.VMEM((te, tm), jnp.bfloat16)],
        compiler_params=_params(("arbitrary", "arbitrary")),
        name="peer",
    )(u_bf, u_bf, vt_bf, h2t, a0, e0, n1, e1)


def _transpose_cast_kernel(v_ref, o_ref):
    o_ref[...] = v_ref[...].T.astype(o_ref.dtype)


def _transpose_cast(v, dtype):
    ne, d = v.shape
    te = min(1024, ne)
    return pl.pallas_call(
        _transpose_cast_kernel,
        grid=(ne // te,),
        in_specs=[pl.BlockSpec((te, d), lambda j: (j, 0))],
        out_specs=pl.BlockSpec((d, te), lambda j: (0, j)),
        out_shape=jax.ShapeDtypeStruct((d, ne), dtype),
        compiler_params=_params(("arbitrary",)),
        name="expert_v_t",
    )(v)


def _final_kernel(ft_ref, x1_ref, gt_ref, lg_ref, lb_ref, o_ref, *, alpha):
    ffn = ft_ref[...].T
    t = alpha * x1_ref[...] + (1.0 + gt_ref[...]) * ffn
    o_ref[...] = _ln_rows(t) * lg_ref[...] + lb_ref[...]


def _final(ffn_t, x1, gt2, ln_g, ln_b, alpha):
    s, d = x1.shape
    tm = min(512, s)
    vec = pl.BlockSpec((1, d), lambda i: (0, 0))
    return pl.pallas_call(
        functools.partial(_final_kernel, alpha=alpha),
        grid=(s // tm,),
        in_specs=[pl.BlockSpec((d, tm), lambda i: (0, i)), pl.BlockSpec((tm, d), lambda i: (i, 0)),
                  vec, vec, vec],
        out_specs=pl.BlockSpec((tm, d), lambda i: (i, 0)),
        out_shape=jax.ShapeDtypeStruct((s, d), jnp.float32),
        compiler_params=_params(("arbitrary",)),
        name="final",
    )(ffn_t, x1, gt2, ln_g.reshape(1, -1), ln_b.reshape(1, -1))


def kernel(x, c, positions, w_ada, b_ada, w_in, lam_re, lam_im, log_dt, ssm_b_re, ssm_b_im, ssm_c_re,
           ssm_c_im, ssm_d, w_glu, b_glu, g_attn, g_ssm, w_out, ln1_g, ln1_b, w_qp, sub_keys, expert_u,
           expert_v, ln2_g, ln2_b):
    b, s, d = x.shape
    assert b == 1, "one sequence per call"
    depth = w_ada.shape[0]
    alpha = (2.0 * depth) ** 0.25
    bf = jnp.bfloat16
    x2 = x.reshape(s, d)
    pos = positions.reshape(s, 1)
    dilations = [dil for _, dil in DILATED_PATTERNS if dil > 1]
    for l in range(depth):
        mod = _ada(c, w_ada[l], b_ada[l])
        sh1, sc1, gt1, sh2, sc2, gt2 = [mod[:, k * d:(k + 1) * d] for k in range(ADA_CHUNKS)]
        z, *z_res = _inproj(x2, pos, sc1, sh1, w_in[l].astype(bf), dilations)
        z_dil = [z.reshape(1, s, -1) if dil == 1 else z_res[dilations.index(dil)]
                 for _, dil in DILATED_PATTERNS]
        attn = _attention(z, z_dil)
        yg = _s5(z, lam_re[l], lam_im[l], log_dt[l], ssm_b_re[l], ssm_b_im[l],
                 ssm_c_re[l], ssm_c_im[l], ssm_d[l])
        x1, h2 = _mixout(attn, yg, x2, w_glu[l].astype(bf), b_glu[l], g_attn[l], g_ssm[l],
                         w_out[l].astype(bf), gt1, ln1_g[l], ln1_b[l], sc2, sh2, alpha)
        a0, n1, e0, e1 = _peerq(h2, w_qp[l].astype(bf), sub_keys[l].astype(bf))
        ffn_t = _peer(h2.T, expert_u[l].astype(bf), _transpose_cast(expert_v[l], bf), a0, n1, e0, e1)
        x2 = _final(ffn_t, x1, gt2, ln2_g[l], ln2_b[l], alpha)
    return x2.reshape(b, s, d)
```

```python
import functools
import math

import jax
import jax.numpy as jnp
from jax import lax
from jax.experimental import pallas as pl
from jax.experimental.pallas import tpu as pltpu

ATTN_HEADS = 16
HEAD_DIM = 64
ATTN_WIDTH = ATTN_HEADS * HEAD_DIM
SSM_GROUP = 16
SSM_STATE = 64
DILATED_PATTERNS = ((128, 1), (512, 4), (2048, 16))
ATTN_BLOCK = 128
ROPE_THETA = 10000.0
NEG_INF = -1e30
PEER_HEADS = 8
PEER_SUB_DIM = 128
PEER_N_KEYS = 128
PEER_TOPK = 16
LN_EPS = 1e-5
ADA_CHUNKS = 6

LANES = 128
SUBLANES = 8
VMEM_LIMIT_BYTES = 56 * 1024 * 1024

S5_CHUNK = 16
GROUPS_PER_BLOCK = LANES // SSM_GROUP
STATE_PER_BLOCK = GROUPS_PER_BLOCK * SSM_STATE
BIG_NEG = -3.0e38
RANK_STEP = 1.0e36
QK_LOG2_SCALE = HEAD_DIM ** -0.5 * math.log2(math.e)


def _params(sem):
    return pltpu.CompilerParams(dimension_semantics=sem, vmem_limit_bytes=VMEM_LIMIT_BYTES)


def _ln_rows(t):
    mu = jnp.mean(t, axis=-1, keepdims=True)
    d = t - mu
    var = jnp.mean(d * d, axis=-1, keepdims=True)
    return d * lax.rsqrt(var + LN_EPS)


def _ada_kernel(c_ref, w_ref, b_ref, o_ref):
    c = c_ref[...]
    sc = c * jax.nn.sigmoid(c)
    o_ref[...] = jnp.sum(sc * w_ref[...], axis=0, keepdims=True) + b_ref[...]


def _ada(c, w_ada, b_ada):
    d = c.shape[-1]
    n = w_ada.shape[-1]
    tn = n // 8
    return pl.pallas_call(
        _ada_kernel,
        grid=(n // tn,),
        in_specs=[pl.BlockSpec((d, 1), lambda j: (0, 0)),
                  pl.BlockSpec((d, tn), lambda j: (0, j)),
                  pl.BlockSpec((1, tn), lambda j: (0, j))],
        out_specs=pl.BlockSpec((1, tn), lambda j: (0, j)),
        out_shape=jax.ShapeDtypeStruct((1, n), jnp.float32),
        compiler_params=_params(("arbitrary",)),
        name="ada",
    )(c.reshape(d, 1), w_ada, b_ada.reshape(1, n))


def _inproj_kernel(x_ref, pos_ref, sc_ref, sh_ref, w_ref, z_ref, *rest, n_rope, dilations):
    dil_refs = rest[:len(dilations)]
    h_scr, cos_scr, sin_scr, z_scr, stage_scr = rest[len(dilations):]
    j = pl.program_id(1)
    tm, tn = z_ref.shape

    @pl.when(j == 0)
    def _():
        h = _ln_rows(x_ref[...]) * (1.0 + sc_ref[...]) + sh_ref[...]
        h_scr[...] = h.astype(jnp.bfloat16)
        lane = lax.broadcasted_iota(jnp.int32, (1, LANES), 1)
        fi = ((lane % HEAD_DIM) % (HEAD_DIM // 2)).astype(jnp.float32)
        inv_freq = jnp.exp(fi * (-math.log(ROPE_THETA) / (HEAD_DIM // 2)))
        ang = pos_ref[...].astype(jnp.float32) * inv_freq
        first_half = (lane % HEAD_DIM) < (HEAD_DIM // 2)
        cos_scr[...] = jnp.cos(ang)
        sin_scr[...] = jnp.where(first_half, -jnp.sin(ang), jnp.sin(ang))

    z = jnp.dot(h_scr[...], w_ref[...], preferred_element_type=jnp.float32)

    @pl.when(j < n_rope)
    def _():
        lane = lax.broadcasted_iota(jnp.int32, (1, LANES), 1)
        first_half = (lane % HEAD_DIM) < (HEAD_DIM // 2)
        q_mult = jnp.where(j < n_rope // 2, QK_LOG2_SCALE, 1.0)
        cos = cos_scr[...] * q_mult
        sin = sin_scr[...] * q_mult
        for cc in range(tn // LANES):
            zc = z[:, cc * LANES:(cc + 1) * LANES]
            partner = jnp.where(first_half, pltpu.roll(zc, LANES - HEAD_DIM // 2, 1),
                                pltpu.roll(zc, HEAD_DIM // 2, 1))
            z_scr[cc] = zc * cos + partner * sin

    @pl.when(j >= n_rope)
    def _():
        for cc in range(tn // LANES):
            z_scr[cc] = z[:, cc * LANES:(cc + 1) * LANES]

    for cc in range(tn // LANES):
        z_ref[:, cc * LANES:(cc + 1) * LANES] = z_scr[cc].astype(z_ref.dtype)

    @pl.when(j < (3 * ATTN_WIDTH) // tn)
    def _():
        base = 1
        for dil, ref in zip(dilations, dil_refs):
            step = dil // base if dil % base == 0 else dil
            for r in range(dil):
                for cc in range(tn // LANES):
                    if step == dil:
                        rows = z_scr[cc, pl.ds(r, tm // dil, stride=dil), :]
                    else:
                        rows = stage_scr[cc, pl.ds((r % base) * (tm // base) + r // base, tm // dil, stride=step), :]
                    ref[r, :, cc * LANES:(cc + 1) * LANES] = rows.astype(ref.dtype)
                    if dil != dilations[-1]:
                        stage_scr[cc, r * (tm // dil):(r + 1) * (tm // dil), :] = rows
            base = dil


def _inproj(x2, pos, sc1, sh1, w_in_bf, dilations):
    s, d = x2.shape
    n = w_in_bf.shape[1]
    tm = min(1024, s)
    tn = 512
    n_rope = (2 * ATTN_WIDTH) // tn
    n_qkv = (3 * ATTN_WIDTH) // tn
    bf = jnp.bfloat16
    dil_specs = [pl.BlockSpec((dil, tm // dil, tn), lambda i, j: (0, i, jnp.minimum(j, n_qkv - 1)))
                 for dil in dilations]
    dil_shapes = [jax.ShapeDtypeStruct((dil, s // dil, 3 * ATTN_WIDTH), bf) for dil in dilations]
    return pl.pallas_call(
        functools.partial(_inproj_kernel, n_rope=n_rope, dilations=tuple(dilations)),
        grid=(s // tm, n // tn),
        in_specs=[pl.BlockSpec((tm, d), lambda i, j: (i, 0)),
                  pl.BlockSpec((tm, 1), lambda i, j: (i, 0)),
                  pl.BlockSpec((1, d), lambda i, j: (0, 0)),
                  pl.BlockSpec((1, d), lambda i, j: (0, 0)),
                  pl.BlockSpec((d, tn), lambda i, j: (0, j))],
        out_specs=[pl.BlockSpec((tm, tn), lambda i, j: (i, j))] + dil_specs,
        out_shape=[jax.ShapeDtypeStruct((s, n), bf)] + dil_shapes,
        scratch_shapes=[pltpu.VMEM((tm, d), bf),
                        pltpu.VMEM((tm, LANES), jnp.float32),
                        pltpu.VMEM((tm, LANES), jnp.float32),
                        pltpu.VMEM((tn // LANES, tm, LANES), jnp.float32),
                        pltpu.VMEM((tn // LANES, tm, LANES), jnp.float32)],
        compiler_params=_params(("arbitrary", "arbitrary")),
        name="inproj",
    )(x2, pos, sc1, sh1, w_in_bf)


def _attn_kernel(q_ref, kc_ref, vc_ref, kp_ref, vp_ref, o_ref, lse_ref, kwin, vt_win, *, nblk):
    i = pl.program_id(1)
    blk = ATTN_BLOCK
    kwin[0:blk, :] = kp_ref[...]
    kwin[blk:, :] = kc_ref[...]
    vt_win[:, 0:blk] = vp_ref[...].T
    vt_win[:, blk:] = vc_ref[...].T

    key = lax.broadcasted_iota(jnp.int32, (2 * blk, blk), 0)
    qry = lax.broadcasted_iota(jnp.int32, (2 * blk, blk), 1)
    lane = lax.broadcasted_iota(jnp.int32, (blk, LANES), 1)
    row = lax.broadcasted_iota(jnp.int32, (blk, LANES), 0)
    nt = (((1,), (1,)), ((), ()))

    def body(j, carry):
        r0 = pl.multiple_of(j * blk, blk)
        first_key = jnp.where((i * nblk + j) > 0, 0, blk)
        mask = jnp.logical_and(key >= jnp.maximum(qry, first_key), key <= qry + blk)
        lse_rows = []
        for hp in range(ATTN_HEADS // 2):
            cs = slice(hp * LANES, (hp + 1) * LANES)
            q2 = q_ref[pl.ds(r0, blk), cs]
            k2 = kwin[pl.ds(r0, 2 * blk), cs]
            vt2 = vt_win[cs, pl.ds(r0, 2 * blk)]
            halves = []
            for hd in range(2):
                in_head = (lane < HEAD_DIM) if hd == 0 else (lane >= HEAD_DIM)
                qm = jnp.where(in_head, q2, jnp.zeros_like(q2))
                st = lax.dot_general(k2, qm, nt, preferred_element_type=jnp.float32)
                st = jnp.where(mask, st, NEG_INF)
                m = jnp.max(st, axis=0, keepdims=True)
                p = jnp.exp2(st - m)
                l = jnp.sum(p, axis=0, keepdims=True)
                ot = jnp.dot(vt2, p.astype(vt2.dtype), preferred_element_type=jnp.float32)
                halves.append(ot / l)
                lse_rows.append(m * math.log(2.0) + jnp.log(l))
            ot2 = jnp.where(row < HEAD_DIM, halves[0], halves[1])
            o_ref[pl.ds(r0, blk), cs] = ot2.T.astype(o_ref.dtype)
        lse_mat = jnp.zeros((LANES, blk), jnp.float32)
        for hidx, lse_h in enumerate(lse_rows):
            lse_mat = jnp.where(row == hidx, lse_h, lse_mat)
        lse_ref[pl.ds(r0, blk), :] = lse_mat.T
        return carry

    lax.fori_loop(0, nblk, body, 0)


def _attn_pattern(zd, dilation):
    _, length, zw = zd.shape
    aw = ATTN_WIDTH
    rows = min(1024, length)
    nblk = rows // ATTN_BLOCK
    cur = lambda off: pl.BlockSpec((None, rows, aw), lambda r, i: (r, i, off))
    prev = lambda off: pl.BlockSpec((None, ATTN_BLOCK, aw), lambda r, i: (r, jnp.maximum(i * nblk - 1, 0), off))
    return pl.pallas_call(
        functools.partial(_attn_kernel, nblk=nblk),
        grid=(dilation, length // rows),
        in_specs=[cur(0), cur(1), cur(2), prev(1), prev(2)],
        out_specs=[pl.BlockSpec((None, rows, aw), lambda r, i: (r, i, 0)),
                   pl.BlockSpec((None, rows, LANES), lambda r, i: (r, i, 0))],
        out_shape=[jax.ShapeDtypeStruct((dilation, length, aw), jnp.bfloat16),
                   jax.ShapeDtypeStruct((dilation, length, LANES), jnp.float32)],
        scratch_shapes=[pltpu.VMEM((rows + ATTN_BLOCK, aw), jnp.bfloat16),
                        pltpu.VMEM((aw, rows + ATTN_BLOCK), jnp.bfloat16)],
        compiler_params=_params(("arbitrary", "arbitrary")),
        name=f"attn_d{dilation}",
    )(zd, zd, zd, zd, zd)


def _attention(z, z_dil):
    outs = []
    for (window, dilation), zd in zip(DILATED_PATTERNS, z_dil):
        assert window // dilation == ATTN_BLOCK
        outs.append(_attn_pattern(zd, dilation))
    return outs


def _s5_param_kernel(lr_r, li_r, ld_r, bre_ref, bim_ref, cre_ref, cim_ref,
                     krev_ref, pblk_ref, qblk_ref, at_ref):
    t_chunk = S5_CHUNK
    nst = STATE_PER_BLOCK
    lr, li, ld = lr_r[...], li_r[...], ld_r[...]
    dt = jnp.exp(ld)
    mag = jnp.exp(lr * dt)
    ar, ai = mag * jnp.cos(li * dt), mag * jnp.sin(li * dt)
    den = lr * lr + li * li
    cr = ((ar - 1.0) * lr + ai * li) / den
    ci = (ai * lr - (ar - 1.0) * li) / den
    bre, bim = bre_ref[...], bim_ref[...]
    bbr = cr * bre - ci * bim
    bbi = cr * bim + ci * bre
    cre, cim = cre_ref[...], cim_ref[...]
    cre_b, cim_b = cre.astype(pblk_ref.dtype), cim.astype(pblk_ref.dtype)
    nt = (((1,), (1,)), ((), ()))
    er, ei = jnp.ones_like(ar), jnp.zeros_like(ai)
    for tau in range(t_chunk):
        pr = er * bbr - ei * bbi
        pi = er * bbi + ei * bbr
        blk = t_chunk - 1 - tau
        rows = slice(blk * LANES, (blk + 1) * LANES)
        prb, pib = pr.astype(pblk_ref.dtype), pi.astype(pblk_ref.dtype)
        pblk_ref[rows, 0:nst] = prb
        pblk_ref[rows, nst:2 * nst] = pib
        k_tau = (lax.dot_general(prb, cre_b, nt, preferred_element_type=jnp.float32)
                 - lax.dot_general(pib, cim_b, nt, preferred_element_type=jnp.float32))
        krev_ref[rows, :] = k_tau.astype(krev_ref.dtype)
        er, ei = er * ar - ei * ai, er * ai + ei * ar
        qblk_ref[tau, :, 0:nst] = (cre * er - cim * ei).astype(qblk_ref.dtype)
        qblk_ref[tau, :, nst:2 * nst] = (-(cre * ei + cim * er)).astype(qblk_ref.dtype)
    krev_ref[t_chunk * LANES:, :] = jnp.zeros((LANES, LANES), krev_ref.dtype)
    at_ref[:, 0:nst] = er
    at_ref[:, nst:2 * nst] = ei


def _s5_params(lam_re, lam_im, log_dt, b_re, b_im, c_re, c_im):
    g, n = lam_re.shape
    p = b_re.shape[-1]
    gb = GROUPS_PER_BLOCK
    nlb = g // gb
    nst = gb * n
    f32 = jnp.float32
    eye = jnp.eye(gb, dtype=f32)

    def row(a):
        return a.astype(f32).reshape(nlb, 1, nst)

    ldt = jnp.broadcast_to(log_dt.astype(f32)[:, None], (g, n))

    def b_blockdiag(b):
        b4 = b.astype(f32).reshape(nlb, gb, n, p)
        return jnp.einsum("lhnq,gh->lgqhn", b4, eye).reshape(nlb, gb * p, nst)

    def c_blockdiag(c):
        c4 = c.astype(f32).reshape(nlb, gb, p, n)
        return jnp.einsum("lhpn,gh->lgphn", c4, eye).reshape(nlb, gb * p, nst)

    tc = S5_CHUNK
    vec_r = pl.BlockSpec((None, 1, nst), lambda l: (l, 0, 0))
    mat = pl.BlockSpec((None, LANES, nst), lambda l: (l, 0, 0))
    return pl.pallas_call(
        _s5_param_kernel,
        grid=(nlb,),
        in_specs=[vec_r, vec_r, vec_r, mat, mat, mat, mat],
        out_specs=[pl.BlockSpec((None, (tc + 1) * LANES, LANES), lambda l: (l, 0, 0)),
                   pl.BlockSpec((None, tc * LANES, 2 * nst), lambda l: (l, 0, 0)),
                   pl.BlockSpec((None, tc, LANES, 2 * nst), lambda l: (l, 0, 0, 0)),
                   pl.BlockSpec((None, 1, 2 * nst), lambda l: (l, 0, 0))],
        out_shape=[jax.ShapeDtypeStruct((nlb, (tc + 1) * LANES, LANES), jnp.bfloat16),
                   jax.ShapeDtypeStruct((nlb, tc * LANES, 2 * nst), jnp.bfloat16),
                   jax.ShapeDtypeStruct((nlb, tc, LANES, 2 * nst), jnp.bfloat16),
                   jax.ShapeDtypeStruct((nlb, 1, 2 * nst), f32)],
        compiler_params=_params(("arbitrary",)),
        name="s5_params",
    )(row(lam_re), row(lam_im), row(ldt),
      b_blockdiag(b_re), b_blockdiag(b_im), c_blockdiag(c_re), c_blockdiag(c_im))


def _load_ucat(u_ref, uf_scr, ucat_scr, nc):
    uf_scr[...] = u_ref[...].astype(jnp.float32)
    for s in range(S5_CHUNK):
        ucat_scr[:, s * LANES:(s + 1) * LANES] = (
            uf_scr[pl.ds(s, nc, stride=S5_CHUNK), :].astype(ucat_scr.dtype))


def _s5_state_in_kernel(u_ref, pblk_ref, b_ref, uf_scr, ucat_scr, *, nc):
    _load_ucat(u_ref, uf_scr, ucat_scr, nc)
    b_ref[...] = jnp.dot(ucat_scr[...], pblk_ref[...], preferred_element_type=jnp.float32)


def _s5_scan_kernel(b_ref, at_ref, h_ref, *, nc, nlb):
    nst = STATE_PER_BLOCK
    a_re = at_ref[:, 0:nst]
    a_im = at_ref[:, nst:2 * nst]

    def body(c, carry):
        hr, hi = carry
        h_ref[c, :, 0:nst] = hr
        h_ref[c, :, nst:2 * nst] = hi
        bc = b_ref[c]
        return (a_re * hr - a_im * hi + bc[:, 0:nst], a_re * hi + a_im * hr + bc[:, nst:2 * nst])

    zero = jnp.zeros((nlb, nst), jnp.float32)
    lax.fori_loop(0, nc, body, (zero, zero))


def _s5_out_kernel(u_ref, h_ref, krev_ref, qblk_ref, dsk_ref, y_ref, uf_scr, ucat_scr, y_scr, *, nc):
    _load_ucat(u_ref, uf_scr, ucat_scr, nc)
    hb = h_ref[...].astype(jnp.bfloat16)
    dsk = dsk_ref[...]
    tc = S5_CHUNK
    nt = (((1,), (1,)), ((), ()))
    for t in range(0, tc, 2):
        kd = (t + 2) * LANES
        r_t = (tc - 1 - t) * LANES
        taps = jnp.concatenate([krev_ref[r_t:r_t + kd, :], krev_ref[r_t - LANES:r_t - LANES + kd, :]], axis=1)
        y2 = jnp.dot(ucat_scr[:, 0:kd], taps, preferred_element_type=jnp.float32)
        carry_in = qblk_ref[t:t + 2].reshape(2 * LANES, qblk_ref.shape[2])
        y2 = y2 + lax.dot_general(hb, carry_in, nt, preferred_element_type=jnp.float32)
        for s in range(2):
            y = y2[:, s * LANES:(s + 1) * LANES] + dsk * uf_scr[pl.ds(t + s, nc, stride=tc), :]
            y = 0.5 * y * (1.0 + lax.erf(y * (2.0 ** -0.5)))
            y_scr[pl.ds(t + s, nc, stride=tc), :] = y
    y_ref[...] = y_scr[...].astype(y_ref.dtype)


def _s5(z, lam_re, lam_im, log_dt, b_re, b_im, c_re, c_im, d_skip):
    s = z.shape[0]
    nlb = lam_re.shape[0] // GROUPS_PER_BLOCK
    nc = s // S5_CHUNK
    nst2 = 2 * STATE_PER_BLOCK
    u_col0 = (3 * ATTN_WIDTH) // LANES
    krev, pblk, qblk, at = _s5_params(lam_re, lam_im, log_dt, b_re, b_im, c_re, c_im)
    u_spec = pl.BlockSpec((s, LANES), lambda l: (0, u_col0 + l))
    b2 = pl.pallas_call(
        functools.partial(_s5_state_in_kernel, nc=nc),
        grid=(nlb,),
        in_specs=[u_spec, pl.BlockSpec((None, S5_CHUNK * LANES, nst2), lambda l: (l, 0, 0))],
        out_specs=pl.BlockSpec((nc, nst2), lambda l: (0, l)),
        out_shape=jax.ShapeDtypeStruct((nc, nlb * nst2), jnp.float32),
        scratch_shapes=[pltpu.VMEM((s, LANES), jnp.float32),
                        pltpu.VMEM((nc, S5_CHUNK * LANES), jnp.bfloat16)],
        compiler_params=_params(("arbitrary",)),
        name="s5_state_in",
    )(z, pblk)
    h3 = pl.pallas_call(
        functools.partial(_s5_scan_kernel, nc=nc, nlb=nlb),
        out_shape=jax.ShapeDtypeStruct((nc, nlb, nst2), jnp.float32),
        compiler_params=pltpu.CompilerParams(vmem_limit_bytes=VMEM_LIMIT_BYTES),
        name="s5_scan",
    )(b2.reshape(nc, nlb, nst2), at.reshape(nlb, nst2))
    return pl.pallas_call(
        functools.partial(_s5_out_kernel, nc=nc),
        grid=(nlb,),
        in_specs=[u_spec,
                  pl.BlockSpec((nc, nst2), lambda l: (0, l)),
                  pl.BlockSpec((None, (S5_CHUNK + 1) * LANES, LANES), lambda l: (l, 0, 0)),
                  pl.BlockSpec((None, S5_CHUNK, LANES, nst2), lambda l: (l, 0, 0, 0)),
                  pl.BlockSpec((1, LANES), lambda l: (0, l))],
        out_specs=pl.BlockSpec((s, LANES), lambda l: (0, l)),
        out_shape=jax.ShapeDtypeStruct((s, nlb * LANES), jnp.bfloat16),
        scratch_shapes=[pltpu.VMEM((s, LANES), jnp.float32),
                        pltpu.VMEM((nc, S5_CHUNK * LANES), jnp.bfloat16),
                        pltpu.VMEM((s, LANES), jnp.float32)],
        compiler_params=_params(("arbitrary",)),
        name="s5_out",
    )(z, h3.reshape(nc, nlb * nst2), krev, qblk, d_skip.astype(jnp.float32).reshape(1, -1))


def _mixout_kernel(*refs, alpha, n_pat):
    o_refs = refs[:n_pat]
    lse_refs = refs[n_pat:2 * n_pat]
    (yg_ref, x_ref, wglu_ref, bglu_ref, ga_ref, gs_ref, wout_ref,
     gt_ref, lg_ref, lb_ref, sc_ref, sh_ref, x1_ref, h2_ref, o_scr, lse_scr) = refs[2 * n_pat:]
    planes, tm, _ = o_scr.shape
    aw = planes * LANES

    def token_major(ref, scr):
        dil = ref.shape[0]
        if dil == 1:
            return ref[0].astype(jnp.float32)
        n_planes = ref.shape[2] // LANES
        for r in range(dil):
            for cc in range(n_planes):
                scr[cc, pl.ds(r, tm // dil, stride=dil), :] = (
                    ref[r, :, cc * LANES:(cc + 1) * LANES].astype(jnp.float32))
        return jnp.concatenate([scr[cc] for cc in range(n_planes)], axis=1)

    lses = [token_major(ref, lse_scr) for ref in lse_refs]
    top = functools.reduce(jnp.maximum, lses)
    ws = [jnp.exp(v - top) for v in lses]
    inv = 1.0 / functools.reduce(lambda u, v: u + v, ws)
    head_of_lane = lax.broadcasted_iota(jnp.int32, (LANES, aw), 1) // HEAD_DIM
    spread = (lax.broadcasted_iota(jnp.int32, (LANES, aw), 0) == head_of_lane).astype(jnp.bfloat16)
    spread2 = jnp.concatenate([spread, spread], axis=0)
    a = jnp.zeros((tm, aw), jnp.float32)
    for w, o_ref in zip(ws, o_refs):
        w = w * inv
        w_hi = w.astype(jnp.bfloat16)
        w_lo = (w - w_hi.astype(jnp.float32)).astype(jnp.bfloat16)
        wide = jnp.dot(jnp.concatenate([w_hi, w_lo], axis=1), spread2, preferred_element_type=jnp.float32)
        a = a + wide * token_major(o_ref, o_scr)
    ra = a * lax.rsqrt(jnp.mean(a * a, axis=-1, keepdims=True) + LN_EPS) * ga_ref[...]
    yg = yg_ref[...]
    y = yg.astype(jnp.float32)
    gate = jax.nn.sigmoid(jnp.dot(yg, wglu_ref[...], preferred_element_type=jnp.float32) + bglu_ref[...])
    y = y * gate
    ry = y * lax.rsqrt(jnp.mean(y * y, axis=-1, keepdims=True) + LN_EPS) * gs_ref[...]
    mix = (jnp.dot(ra.astype(jnp.bfloat16), wout_ref[0:aw, :], preferred_element_type=jnp.float32)
           + jnp.dot(ry.astype(jnp.bfloat16), wout_ref[aw:, :], preferred_element_type=jnp.float32))
    t = alpha * x_ref[...] + (1.0 + gt_ref[...]) * mix
    x1 = _ln_rows(t) * lg_ref[...] + lb_ref[...]
    x1_ref[...] = x1
    h2_ref[...] = (_ln_rows(x1) * (1.0 + sc_ref[...]) + sh_ref[...]).astype(h2_ref.dtype)


def _mixout(attn_parts, yg, x2, w_glu_bf, b_glu, g_attn, g_ssm, w_out_bf, gt1, ln_g, ln_b, sc2, sh2, alpha):
    s, d = x2.shape
    aw = attn_parts[0][0].shape[-1]
    sw = yg.shape[1]
    n_pat = len(attn_parts)
    tm = min(512, s)
    row = lambda i: (i, 0)
    fix = lambda i: (0, 0)
    vec = lambda n: pl.BlockSpec((1, n), fix)
    res = lambda a: pl.BlockSpec((a.shape[0], tm // a.shape[0], a.shape[2]), lambda i: (0, i, 0))
    return pl.pallas_call(
        functools.partial(_mixout_kernel, alpha=alpha, n_pat=n_pat),
        grid=(s // tm,),
        in_specs=([res(o) for o, _ in attn_parts] + [res(v) for _, v in attn_parts]
                  + [pl.BlockSpec((tm, sw), row), pl.BlockSpec((tm, d), row),
                     pl.BlockSpec((sw, sw), fix), vec(sw), vec(aw), vec(sw),
                     pl.BlockSpec((aw + sw, d), fix), vec(d), vec(d), vec(d), vec(d), vec(d)]),
        out_specs=[pl.BlockSpec((tm, d), row), pl.BlockSpec((tm, d), row)],
        out_shape=[jax.ShapeDtypeStruct((s, d), jnp.float32), jax.ShapeDtypeStruct((s, d), jnp.bfloat16)],
        scratch_shapes=[pltpu.VMEM((aw // LANES, tm, LANES), jnp.float32),
                        pltpu.VMEM((1, tm, LANES), jnp.float32)],
        compiler_params=_params(("arbitrary",)),
        name="mixout",
    )(*[o for o, _ in attn_parts], *[l for _, l in attn_parts], yg, x2, w_glu_bf, b_glu.reshape(1, -1),
      g_attn.reshape(1, -1), g_ssm.reshape(1, -1), w_out_bf, gt1, ln_g.reshape(1, -1), ln_b.reshape(1, -1),
      sc2, sh2)


def _top_values(sc, k, want_rank=False):
    vals = []
    for j in range(k):
        m = jnp.max(sc, axis=0, keepdims=True)
        vals.append(m)
        sc = jnp.where(sc == m, BIG_NEG + j * RANK_STEP, sc)
    rank = None
    if want_rank:
        rank = jnp.where(sc < BIG_NEG + k * RANK_STEP, jnp.round((sc - BIG_NEG) * (1.0 / RANK_STEP)), float(k))
    return vals, rank


def _peerq_kernel(h2_ref, wqp_ref, keys_ref, c0_ref, r1_ref, e0_ref, e1_ref, sc_scr):
    k = PEER_TOPK
    tm = h2_ref.shape[0]
    qp = jnp.dot(h2_ref[...], wqp_ref[...], preferred_element_type=jnp.float32).astype(jnp.bfloat16)
    nt = (((1,), (1,)), ((), ()))
    for hi in range(2 * PEER_HEADS):
        col = hi * PEER_SUB_DIM
        sc_scr[hi] = lax.dot_general(keys_ref[hi % 2], qp[:, col:col + PEER_SUB_DIM], nt,
                                     preferred_element_type=jnp.float32)
    def body(h, carry):
        s0 = sc_scr[2 * h]
        s1 = sc_scr[2 * h + 1]
        top0, _ = _top_values(s0, k)
        top1, rank1 = _top_values(s1, k, want_rank=True)
        cands = [top0[i] + top1[j] for i in range(k) for j in range(k) if (i + 1) * (j + 1) <= k]
        pad = (-len(cands)) % SUBLANES
        cmat = jnp.concatenate(cands + [jnp.full_like(cands[0], BIG_NEG)] * pad, axis=0)
        best, _ = _top_values(cmat, k)
        thr = best[k - 1]
        m0, m1 = top0[0], top1[0]
        zsum = jnp.sum(jnp.where(cmat >= thr, jnp.exp(cmat - (m0 + m1)), 0.0), axis=0, keepdims=True)
        count0 = jnp.zeros_like(s0)
        for j in range(k):
            count0 = jnp.where(s0 + top1[j] >= thr, float(j + 1), count0)
        c0_ref[h] = count0
        r1_ref[h] = rank1.astype(r1_ref.dtype)
        e0_ref[h] = jnp.exp(s0 - m0) / zsum
        e1_ref[h] = jnp.exp(s1 - m1).astype(e1_ref.dtype)
        return carry

    lax.fori_loop(0, PEER_HEADS, body, 0)


def _peerq(h2, w_qp_bf, keys_bf):
    s, d = h2.shape
    tm = min(512, s)
    nk = keys_bf.shape[1]
    stat = pl.BlockSpec((PEER_HEADS, nk, tm), lambda i: (0, 0, i))
    words = jax.ShapeDtypeStruct((PEER_HEADS, nk, s), jnp.float32)
    halfs = jax.ShapeDtypeStruct((PEER_HEADS, nk, s), jnp.bfloat16)
    return pl.pallas_call(
        _peerq_kernel,
        grid=(s // tm,),
        in_specs=[pl.BlockSpec((tm, d), lambda i: (i, 0)),
                  pl.BlockSpec(w_qp_bf.shape, lambda i: (0, 0)),
                  pl.BlockSpec(keys_bf.shape, lambda i: (0, 0, 0))],
        out_specs=[stat, stat, stat, stat],
        out_shape=[words, halfs, words, halfs],
        scratch_shapes=[pltpu.VMEM((2 * PEER_HEADS, nk, tm), jnp.float32)],
        compiler_params=_params(("arbitrary",)),
        name="peerq",
    )(h2, w_qp_bf, keys_bf)


def _peer_kernel(ua_ref, ub_ref, vt_ref, h2t_ref, c0_ref, e0_ref, r1_ref, e1_ref, o_ref,
                 a_s0, a_s1, w_s0, w_s1, *, n_pairs, lane_chunk):
    m = pl.program_id(1)
    nk = PEER_N_KEYS
    te, tm = a_s0.shape
    rows_per_tile = te // nk
    packed = 2 * SUBLANES
    bf = jnp.bfloat16

    def stage_a(u_ref, dst):
        dst[...] = jnp.dot(u_ref[...], h2t_ref[...], preferred_element_type=jnp.float32)

    def row_tile(ref, h, i0, ls):
        return jnp.broadcast_to(ref[h, i0:i0 + 1, ls], (packed, lane_chunk)).astype(bf)

    def stage_b(src, dst, row_off, ls):
        for rr in range(rows_per_tile):
            rows = slice(rr * nk, (rr + 1) * nk)
            i0 = row_off + rr
            a = src[rows, ls]
            act = 0.5 * a * (1.0 + lax.erf(a * (2.0 ** -0.5)))
            gate = jnp.zeros((nk // packed, packed, lane_chunk), bf)
            for h in range(PEER_HEADS):
                c0 = row_tile(c0_ref, h, i0, ls)
                e0 = row_tile(e0_ref, h, i0, ls)
                r1 = r1_ref[h, :, ls].reshape(nk // packed, packed, lane_chunk)
                e1 = e1_ref[h, :, ls].reshape(nk // packed, packed, lane_chunk)
                gate = gate + jnp.where(r1 < c0, e0 * e1, jnp.zeros_like(e1))
            w = gate * act.astype(bf).reshape(nk // packed, packed, lane_chunk)
            dst[rows, ls] = w.reshape(nk, lane_chunk)

    def tile(src, dst, half, row_off):
        for c in range(tm // lane_chunk):
            ls = slice(c * lane_chunk, (c + 1) * lane_chunk)
            stage_b(src, dst, row_off, ls)
            o_ref[:, ls] += jnp.dot(vt_ref[:, half * te:(half + 1) * te], dst[:, ls],
                                    preferred_element_type=jnp.float32)

    @pl.when(m == 0)
    def _():
        o_ref[...] = jnp.zeros_like(o_ref)
        stage_a(ub_ref, a_s0)

    @pl.when(jnp.logical_and(m > 0, m < n_pairs))
    def _():
        stage_a(ua_ref, a_s1)
        tile(a_s0, w_s0, 0, 0)
        stage_a(ub_ref, a_s0)
        tile(a_s1, w_s1, 1, rows_per_tile)

    @pl.when(m == n_pairs)
    def _():
        stage_a(ua_ref, a_s1)
        tile(a_s0, w_s0, 0, 0)
        tile(a_s1, w_s1, 1, rows_per_tile)


def _peer(h2t, u_bf, vt_bf, a0, n1, e0, e1):
    d, s = h2t.shape
    ne = u_bf.shape[0]
    nk = PEER_N_KEYS
    tm = min(512, s)
    te = (SUBLANES // 2) * nk
    n_tiles = ne // te
    n_pairs = n_tiles // 2
    rows = pl.BlockSpec((PEER_HEADS, SUBLANES, tm), lambda i, m: (0, jnp.maximum(m - 1, 0), i))
    full_stat = pl.BlockSpec((PEER_HEADS, nk, tm), lambda i, m: (0, 0, i))
    return pl.pallas_call(
        functools.partial(_peer_kernel, n_pairs=n_pairs, lane_chunk=min(256, tm)),
        grid=(s // tm, n_pairs + 1),
        in_specs=[pl.BlockSpec((te, d), lambda i, m: (jnp.maximum(2 * m - 1, 0), 0)),
                  pl.BlockSpec((te, d), lambda i, m: (jnp.minimum(2 * m, n_tiles - 1), 0)),
                  pl.BlockSpec((d, 2 * te), lambda i, m: (0, jnp.maximum(m - 1, 0))),
                  pl.BlockSpec((d, tm), lambda i, m: (0, i)),
                  rows, rows, full_stat, full_stat],
        out_specs=pl.BlockSpec((d, tm), lambda i, m: (0, i)),
        out_shape=jax.ShapeDtypeStruct((d, s), jnp.float32),
        scratch_shapes=[pltpu.VMEM((te, tm), jnp.float32), pltpu.VMEM((te, tm), jnp.float32),
                        pltpu.VMEM((te, tm), jnp.bfloat16), pltpu.VMEM((te, tm), jnp.bfloat16)],
        compiler_params=_params(("arbitrary", "arbitrary")),
        name="peer",
    )(u_bf, u_bf, vt_bf, h2t, a0, e0, n1, e1)


def _transpose_cast_kernel(v_ref, o_ref):
    o_ref[...] = v_ref[...].T.astype(o_ref.dtype)


def _transpose_cast(v, dtype):
    ne, d = v.shape
    te = min(1024, ne)
    return pl.pallas_call(
        _transpose_cast_kernel,
        grid=(ne // te,),
        in_specs=[pl.BlockSpec((te, d), lambda j: (j, 0))],
        out_specs=pl.BlockSpec((d, te), lambda j: (0, j)),
        out_shape=jax.ShapeDtypeStruct((d, ne), dtype),
        compiler_params=_params(("arbitrary",)),
        name="expert_v_t",
    )(v)


def _final_kernel(ft_ref, x1_ref, gt_ref, lg_ref, lb_ref, o_ref, *, alpha):
    ffn = ft_ref[...].T
    t = alpha * x1_ref[...] + (1.0 + gt_ref[...]) * ffn
    o_ref[...] = _ln_rows(t) * lg_ref[...] + lb_ref[...]


def _final(ffn_t, x1, gt2, ln_g, ln_b, alpha):
    s, d = x1.shape
    tm = min(512, s)
    vec = pl.BlockSpec((1, d), lambda i: (0, 0))
    return pl.pallas_call(
        functools.partial(_final_kernel, alpha=alpha),
        grid=(s // tm,),
        in_specs=[pl.BlockSpec((d, tm), lambda i: (0, i)), pl.BlockSpec((tm, d), lambda i: (i, 0)),
                  vec, vec, vec],
        out_specs=pl.BlockSpec((tm, d), lambda i: (i, 0)),
        out_shape=jax.ShapeDtypeStruct((s, d), jnp.float32),
        compiler_params=_params(("arbitrary",)),
        name="final",
    )(ffn_t, x1, gt2, ln_g.reshape(1, -1), ln_b.reshape(1, -1))


def kernel(x, c, positions, w_ada, b_ada, w_in, lam_re, lam_im, log_dt, ssm_b_re, ssm_b_im, ssm_c_re,
           ssm_c_im, ssm_d, w_glu, b_glu, g_attn, g_ssm, w_out, ln1_g, ln1_b, w_qp, sub_keys, expert_u,
           expert_v, ln2_g, ln2_b):
    b, s, d = x.shape
    assert b == 1, "one sequence per call"
    depth = w_ada.shape[0]
    alpha = (2.0 * depth) ** 0.25
    bf = jnp.bfloat16
    x2 = x.reshape(s, d)
    pos = positions.reshape(s, 1)
    dilations = [dil for _, dil in DILATED_PATTERNS if dil > 1]
    for l in range(depth):
        mod = _ada(c, w_ada[l], b_ada[l])
        sh1, sc1, gt1, sh2, sc2, gt2 = [mod[:, k * d:(k + 1) * d] for k in range(ADA_CHUNKS)]
        z, *z_res = _inproj(x2, pos, sc1, sh1, w_in[l].astype(bf), dilations)
        z_dil = [z.reshape(1, s, -1) if dil == 1 else z_res[dilations.index(dil)]
                 for _, dil in DILATED_PATTERNS]
        attn = _attention(z, z_dil)
        yg = _s5(z, lam_re[l], lam_im[l], log_dt[l], ssm_b_re[l], ssm_b_im[l],
                 ssm_c_re[l], ssm_c_im[l], ssm_d[l])
        x1, h2 = _mixout(attn, yg, x2, w_glu[l].astype(bf), b_glu[l], g_attn[l], g_ssm[l],
                         w_out[l].astype(bf), gt1, ln1_g[l], ln1_b[l], sc2, sh2, alpha)
        a0, n1, e0, e1 = _peerq(h2, w_qp[l].astype(bf), sub_keys[l].astype(bf))
        ffn_t = _peer(h2.T, expert_u[l].astype(bf), _transpose_cast(expert_v[l], bf), a0, n1, e0, e1)
        x2 = _final(ffn_t, x1, gt2, ln2_g[l], ln2_b[l], alpha)
    return x2.reshape(b, s, d)
```

```python
import functools
import math

import jax
import jax.numpy as jnp
from jax import lax
from jax.experimental import pallas as pl
from jax.experimental.pallas import tpu as pltpu

ATTN_HEADS = 16
HEAD_DIM = 64
ATTN_WIDTH = ATTN_HEADS * HEAD_DIM
SSM_GROUP = 16
SSM_STATE = 64
DILATED_PATTERNS = ((128, 1), (512, 4), (2048, 16))
ATTN_BLOCK = 128
ROPE_THETA = 10000.0
NEG_INF = -1e30
PEER_HEADS = 8
PEER_SUB_DIM = 128
PEER_N_KEYS = 128
PEER_TOPK = 16
LN_EPS = 1e-5
ADA_CHUNKS = 6

LANES = 128
SUBLANES = 8
VMEM_LIMIT_BYTES = 56 * 1024 * 1024

S5_CHUNK = 16
GROUPS_PER_BLOCK = LANES // SSM_GROUP
STATE_PER_BLOCK = GROUPS_PER_BLOCK * SSM_STATE
BIG_NEG = -3.0e38
RANK_STEP = 1.0e36
QK_LOG2_SCALE = HEAD_DIM ** -0.5 * math.log2(math.e)


def _params(sem, fuse_inputs=None):
    return pltpu.CompilerParams(dimension_semantics=sem, vmem_limit_bytes=VMEM_LIMIT_BYTES,
                                allow_input_fusion=fuse_inputs)


def _ln_rows(t):
    mu = jnp.mean(t, axis=-1, keepdims=True)
    d = t - mu
    var = jnp.mean(d * d, axis=-1, keepdims=True)
    return d * lax.rsqrt(var + LN_EPS)


def _ada_kernel(c_ref, w_ref, b_ref, o_ref):
    c = c_ref[...]
    sc = c * jax.nn.sigmoid(c)
    o_ref[...] = jnp.sum(sc * w_ref[...], axis=0, keepdims=True) + b_ref[...]


def _ada(c, w_ada, b_ada):
    d = c.shape[-1]
    n = w_ada.shape[-1]
    tn = n // 8
    return pl.pallas_call(
        _ada_kernel,
        grid=(n // tn,),
        in_specs=[pl.BlockSpec((d, 1), lambda j: (0, 0)),
                  pl.BlockSpec((d, tn), lambda j: (0, j)),
                  pl.BlockSpec((1, tn), lambda j: (0, j))],
        out_specs=pl.BlockSpec((1, tn), lambda j: (0, j)),
        out_shape=jax.ShapeDtypeStruct((1, n), jnp.float32),
        compiler_params=_params(("arbitrary",)),
        name="ada",
    )(c.reshape(d, 1), w_ada, b_ada.reshape(1, n))


def _inproj_kernel(x_ref, pos_ref, sc_ref, sh_ref, w_ref, z_ref, *rest, n_rope, dilations):
    dil_refs = rest[:len(dilations)]
    h_scr, cos_scr, sin_scr, z_scr, stage_scr = rest[len(dilations):]
    j = pl.program_id(1)
    tm, tn = z_ref.shape

    @pl.when(j == 0)
    def _():
        h = _ln_rows(x_ref[...]) * (1.0 + sc_ref[...]) + sh_ref[...]
        h_scr[...] = h.astype(jnp.bfloat16)
        lane = lax.broadcasted_iota(jnp.int32, (1, LANES), 1)
        fi = ((lane % HEAD_DIM) % (HEAD_DIM // 2)).astype(jnp.float32)
        inv_freq = jnp.exp(fi * (-math.log(ROPE_THETA) / (HEAD_DIM // 2)))
        ang = pos_ref[...].astype(jnp.float32) * inv_freq
        first_half = (lane % HEAD_DIM) < (HEAD_DIM // 2)
        cos_scr[...] = jnp.cos(ang)
        sin_scr[...] = jnp.where(first_half, -jnp.sin(ang), jnp.sin(ang))

    z = jnp.dot(h_scr[...], w_ref[...], preferred_element_type=jnp.float32)

    @pl.when(j < n_rope)
    def _():
        lane = lax.broadcasted_iota(jnp.int32, (1, LANES), 1)
        first_half = (lane % HEAD_DIM) < (HEAD_DIM // 2)
        q_mult = jnp.where(j < n_rope // 2, QK_LOG2_SCALE, 1.0)
        cos = cos_scr[...] * q_mult
        sin = sin_scr[...] * q_mult
        for cc in range(tn // LANES):
            zc = z[:, cc * LANES:(cc + 1) * LANES]
            partner = jnp.where(first_half, pltpu.roll(zc, LANES - HEAD_DIM // 2, 1),
                                pltpu.roll(zc, HEAD_DIM // 2, 1))
            z_scr[cc] = zc * cos + partner * sin

    @pl.when(j >= n_rope)
    def _():
        for cc in range(tn // LANES):
            z_scr[cc] = z[:, cc * LANES:(cc + 1) * LANES]

    for cc in range(tn // LANES):
        z_ref[:, cc * LANES:(cc + 1) * LANES] = z_scr[cc].astype(z_ref.dtype)

    @pl.when(j < (3 * ATTN_WIDTH) // tn)
    def _():
        base = 1
        for dil, ref in zip(dilations, dil_refs):
            step = dil // base if dil % base == 0 else dil
            for r in range(dil):
                for cc in range(tn // LANES):
                    if step == dil:
                        rows = z_scr[cc, pl.ds(r, tm // dil, stride=dil), :]
                    else:
                        rows = stage_scr[cc, pl.ds((r % base) * (tm // base) + r // base, tm // dil, stride=step), :]
                    ref[r, :, cc * LANES:(cc + 1) * LANES] = rows.astype(ref.dtype)
                    if dil != dilations[-1]:
                        stage_scr[cc, r * (tm // dil):(r + 1) * (tm // dil), :] = rows
            base = dil


def _inproj(x2, pos, sc1, sh1, w_in_bf, dilations):
    s, d = x2.shape
    n = w_in_bf.shape[1]
    tm = min(1024, s)
    tn = 512
    n_rope = (2 * ATTN_WIDTH) // tn
    n_qkv = (3 * ATTN_WIDTH) // tn
    bf = jnp.bfloat16
    dil_specs = [pl.BlockSpec((dil, tm // dil, tn), lambda i, j: (0, i, jnp.minimum(j, n_qkv - 1)))
                 for dil in dilations]
    dil_shapes = [jax.ShapeDtypeStruct((dil, s // dil, 3 * ATTN_WIDTH), bf) for dil in dilations]
    return pl.pallas_call(
        functools.partial(_inproj_kernel, n_rope=n_rope, dilations=tuple(dilations)),
        grid=(s // tm, n // tn),
        in_specs=[pl.BlockSpec((tm, d), lambda i, j: (i, 0)),
                  pl.BlockSpec((tm, 1), lambda i, j: (i, 0)),
                  pl.BlockSpec((1, d), lambda i, j: (0, 0)),
                  pl.BlockSpec((1, d), lambda i, j: (0, 0)),
                  pl.BlockSpec((d, tn), lambda i, j: (0, j))],
        out_specs=[pl.BlockSpec((tm, tn), lambda i, j: (i, j))] + dil_specs,
        out_shape=[jax.ShapeDtypeStruct((s, n), bf)] + dil_shapes,
        scratch_shapes=[pltpu.VMEM((tm, d), bf),
                        pltpu.VMEM((tm, LANES), jnp.float32),
                        pltpu.VMEM((tm, LANES), jnp.float32),
                        pltpu.VMEM((tn // LANES, tm, LANES), jnp.float32),
                        pltpu.VMEM((tn // LANES, tm, LANES), jnp.float32)],
        compiler_params=_params(("arbitrary", "arbitrary")),
        name="inproj",
    )(x2, pos, sc1, sh1, w_in_bf)


def _attn_kernel(q_ref, kc_ref, vc_ref, kp_ref, vp_ref, o_ref, lse_ref, kwin, vt_win, *, nblk):
    i = pl.program_id(1)
    blk = ATTN_BLOCK
    kwin[0:blk, :] = kp_ref[...]
    kwin[blk:, :] = kc_ref[...]
    vt_win[:, 0:blk] = vp_ref[...].T
    vt_win[:, blk:] = vc_ref[...].T

    key = lax.broadcasted_iota(jnp.int32, (2 * blk, blk), 0)
    qry = lax.broadcasted_iota(jnp.int32, (2 * blk, blk), 1)
    lane = lax.broadcasted_iota(jnp.int32, (blk, LANES), 1)
    row = lax.broadcasted_iota(jnp.int32, (blk, LANES), 0)
    nt = (((1,), (1,)), ((), ()))

    def body(j, carry):
        r0 = pl.multiple_of(j * blk, blk)
        first_key = jnp.where((i * nblk + j) > 0, 0, blk)
        mask = jnp.logical_and(key >= jnp.maximum(qry, first_key), key <= qry + blk)
        lse_rows = []
        for hp in range(ATTN_HEADS // 2):
            cs = slice(hp * LANES, (hp + 1) * LANES)
            q2 = q_ref[pl.ds(r0, blk), cs]
            k2 = kwin[pl.ds(r0, 2 * blk), cs]
            vt2 = vt_win[cs, pl.ds(r0, 2 * blk)]
            halves = []
            for hd in range(2):
                in_head = (lane < HEAD_DIM) if hd == 0 else (lane >= HEAD_DIM)
                qm = jnp.where(in_head, q2, jnp.zeros_like(q2))
                st = lax.dot_general(k2, qm, nt, preferred_element_type=jnp.float32)
                st = jnp.where(mask, st, NEG_INF)
                m = jnp.max(st, axis=0, keepdims=True)
                p = jnp.exp2(st - m)
                l = jnp.sum(p, axis=0, keepdims=True)
                ot = jnp.dot(vt2, p.astype(vt2.dtype), preferred_element_type=jnp.float32)
                halves.append(ot / l)
                lse_rows.append(m * math.log(2.0) + jnp.log(l))
            ot2 = jnp.where(row < HEAD_DIM, halves[0], halves[1])
            o_ref[pl.ds(r0, blk), cs] = ot2.T.astype(o_ref.dtype)
        lse_mat = jnp.zeros((LANES, blk), jnp.float32)
        for hidx, lse_h in enumerate(lse_rows):
            lse_mat = jnp.where(row == hidx, lse_h, lse_mat)
        lse_ref[pl.ds(r0, blk), :] = lse_mat.T
        return carry

    lax.fori_loop(0, nblk, body, 0)


def _attn_pattern(zd, dilation):
    _, length, zw = zd.shape
    aw = ATTN_WIDTH
    rows = min(1024, length)
    nblk = rows // ATTN_BLOCK
    cur = lambda off: pl.BlockSpec((None, rows, aw), lambda r, i: (r, i, off))
    prev = lambda off: pl.BlockSpec((None, ATTN_BLOCK, aw), lambda r, i: (r, jnp.maximum(i * nblk - 1, 0), off))
    return pl.pallas_call(
        functools.partial(_attn_kernel, nblk=nblk),
        grid=(dilation, length // rows),
        in_specs=[cur(0), cur(1), cur(2), prev(1), prev(2)],
        out_specs=[pl.BlockSpec((None, rows, aw), lambda r, i: (r, i, 0)),
                   pl.BlockSpec((None, rows, LANES), lambda r, i: (r, i, 0))],
        out_shape=[jax.ShapeDtypeStruct((dilation, length, aw), jnp.bfloat16),
                   jax.ShapeDtypeStruct((dilation, length, LANES), jnp.float32)],
        scratch_shapes=[pltpu.VMEM((rows + ATTN_BLOCK, aw), jnp.bfloat16),
                        pltpu.VMEM((aw, rows + ATTN_BLOCK), jnp.bfloat16)],
        compiler_params=_params(("arbitrary", "arbitrary")),
        name=f"attn_d{dilation}",
    )(zd, zd, zd, zd, zd)


def _attention(z, z_dil):
    outs = []
    for (window, dilation), zd in zip(DILATED_PATTERNS, z_dil):
        assert window // dilation == ATTN_BLOCK
        outs.append(_attn_pattern(zd, dilation))
    return outs


def _s5_param_kernel(lr_r, li_r, ld_r, bre_ref, bim_ref, cre_ref, cim_ref,
                     krev_ref, pblk_ref, qblk_ref, at_ref):
    t_chunk = S5_CHUNK
    nst = STATE_PER_BLOCK
    lr, li, ld = lr_r[...], li_r[...], ld_r[...]
    dt = jnp.exp(ld)
    mag = jnp.exp(lr * dt)
    ar, ai = mag * jnp.cos(li * dt), mag * jnp.sin(li * dt)
    den = lr * lr + li * li
    cr = ((ar - 1.0) * lr + ai * li) / den
    ci = (ai * lr - (ar - 1.0) * li) / den
    bre, bim = bre_ref[...], bim_ref[...]
    bbr = cr * bre - ci * bim
    bbi = cr * bim + ci * bre
    cre, cim = cre_ref[...], cim_ref[...]
    cre_b, cim_b = cre.astype(pblk_ref.dtype), cim.astype(pblk_ref.dtype)
    nt = (((1,), (1,)), ((), ()))
    er, ei = jnp.ones_like(ar), jnp.zeros_like(ai)
    for tau in range(t_chunk):
        pr = er * bbr - ei * bbi
        pi = er * bbi + ei * bbr
        blk = t_chunk - 1 - tau
        rows = slice(blk * LANES, (blk + 1) * LANES)
        prb, pib = pr.astype(pblk_ref.dtype), pi.astype(pblk_ref.dtype)
        pblk_ref[rows, 0:nst] = prb
        pblk_ref[rows, nst:2 * nst] = pib
        k_tau = (lax.dot_general(prb, cre_b, nt, preferred_element_type=jnp.float32)
                 - lax.dot_general(pib, cim_b, nt, preferred_element_type=jnp.float32))
        krev_ref[rows, :] = k_tau.astype(krev_ref.dtype)
        er, ei = er * ar - ei * ai, er * ai + ei * ar
        qblk_ref[tau, :, 0:nst] = (cre * er - cim * ei).astype(qblk_ref.dtype)
        qblk_ref[tau, :, nst:2 * nst] = (-(cre * ei + cim * er)).astype(qblk_ref.dtype)
    krev_ref[t_chunk * LANES:, :] = jnp.zeros((LANES, LANES), krev_ref.dtype)
    at_ref[:, 0:nst] = er
    at_ref[:, nst:2 * nst] = ei


def _s5_params(lam_re, lam_im, log_dt, b_re, b_im, c_re, c_im):
    g, n = lam_re.shape
    p = b_re.shape[-1]
    gb = GROUPS_PER_BLOCK
    nlb = g // gb
    nst = gb * n
    f32 = jnp.float32
    eye = jnp.eye(gb, dtype=f32)

    def row(a):
        return a.astype(f32).reshape(nlb, 1, nst)

    ldt = jnp.broadcast_to(log_dt.astype(f32)[:, None], (g, n))

    def b_blockdiag(b):
        b4 = b.astype(f32).reshape(nlb, gb, n, p)
        return jnp.einsum("lhnq,gh->lgqhn", b4, eye).reshape(nlb, gb * p, nst)

    def c_blockdiag(c):
        c4 = c.astype(f32).reshape(nlb, gb, p, n)
        return jnp.einsum("lhpn,gh->lgphn", c4, eye).reshape(nlb, gb * p, nst)

    tc = S5_CHUNK
    vec_r = pl.BlockSpec((None, 1, nst), lambda l: (l, 0, 0))
    mat = pl.BlockSpec((None, LANES, nst), lambda l: (l, 0, 0))
    return pl.pallas_call(
        _s5_param_kernel,
        grid=(nlb,),
        in_specs=[vec_r, vec_r, vec_r, mat, mat, mat, mat],
        out_specs=[pl.BlockSpec((None, (tc + 1) * LANES, LANES), lambda l: (l, 0, 0)),
                   pl.BlockSpec((None, tc * LANES, 2 * nst), lambda l: (l, 0, 0)),
                   pl.BlockSpec((None, tc, LANES, 2 * nst), lambda l: (l, 0, 0, 0)),
                   pl.BlockSpec((None, 1, 2 * nst), lambda l: (l, 0, 0))],
        out_shape=[jax.ShapeDtypeStruct((nlb, (tc + 1) * LANES, LANES), jnp.bfloat16),
                   jax.ShapeDtypeStruct((nlb, tc * LANES, 2 * nst), jnp.bfloat16),
                   jax.ShapeDtypeStruct((nlb, tc, LANES, 2 * nst), jnp.bfloat16),
                   jax.ShapeDtypeStruct((nlb, 1, 2 * nst), f32)],
        compiler_params=_params(("arbitrary",)),
        name="s5_params",
    )(row(lam_re), row(lam_im), row(ldt),
      b_blockdiag(b_re), b_blockdiag(b_im), c_blockdiag(c_re), c_blockdiag(c_im))


def _load_ucat(u_ref, uf_scr, ucat_scr, nc):
    uf_scr[...] = u_ref[...].astype(jnp.float32)
    for s in range(S5_CHUNK):
        ucat_scr[:, s * LANES:(s + 1) * LANES] = (
            uf_scr[pl.ds(s, nc, stride=S5_CHUNK), :].astype(ucat_scr.dtype))


def _s5_state_in_kernel(u_ref, pblk_ref, b_ref, uf_scr, ucat_scr, *, nc):
    _load_ucat(u_ref, uf_scr, ucat_scr, nc)
    b_ref[...] = jnp.dot(ucat_scr[...], pblk_ref[...], preferred_element_type=jnp.float32)


def _s5_scan_kernel(b_ref, at_ref, h_ref, *, nc, nlb):
    nst = STATE_PER_BLOCK
    a_re = at_ref[:, 0:nst]
    a_im = at_ref[:, nst:2 * nst]

    def body(c, carry):
        hr, hi = carry
        h_ref[c, :, 0:nst] = hr
        h_ref[c, :, nst:2 * nst] = hi
        bc = b_ref[c]
        return (a_re * hr - a_im * hi + bc[:, 0:nst], a_re * hi + a_im * hr + bc[:, nst:2 * nst])

    zero = jnp.zeros((nlb, nst), jnp.float32)
    lax.fori_loop(0, nc, body, (zero, zero))


def _s5_out_kernel(u_ref, h_ref, krev_ref, qblk_ref, dsk_ref, y_ref, uf_scr, ucat_scr, y_scr, *, nc):
    _load_ucat(u_ref, uf_scr, ucat_scr, nc)
    hb = h_ref[...].astype(jnp.bfloat16)
    dsk = dsk_ref[...]
    tc = S5_CHUNK
    nt = (((1,), (1,)), ((), ()))
    for t in range(0, tc, 2):
        kd = (t + 2) * LANES
        r_t = (tc - 1 - t) * LANES
        taps = jnp.concatenate([krev_ref[r_t:r_t + kd, :], krev_ref[r_t - LANES:r_t - LANES + kd, :]], axis=1)
        y2 = jnp.dot(ucat_scr[:, 0:kd], taps, preferred_element_type=jnp.float32)
        carry_in = qblk_ref[t:t + 2].reshape(2 * LANES, qblk_ref.shape[2])
        y2 = y2 + lax.dot_general(hb, carry_in, nt, preferred_element_type=jnp.float32)
        for s in range(2):
            y = y2[:, s * LANES:(s + 1) * LANES] + dsk * uf_scr[pl.ds(t + s, nc, stride=tc), :]
            y = 0.5 * y * (1.0 + lax.erf(y * (2.0 ** -0.5)))
            y_scr[pl.ds(t + s, nc, stride=tc), :] = y
    y_ref[...] = y_scr[...].astype(y_ref.dtype)


def _s5(z, lam_re, lam_im, log_dt, b_re, b_im, c_re, c_im, d_skip):
    s = z.shape[0]
    nlb = lam_re.shape[0] // GROUPS_PER_BLOCK
    nc = s // S5_CHUNK
    nst2 = 2 * STATE_PER_BLOCK
    u_col0 = (3 * ATTN_WIDTH) // LANES
    krev, pblk, qblk, at = _s5_params(lam_re, lam_im, log_dt, b_re, b_im, c_re, c_im)
    u_spec = pl.BlockSpec((s, LANES), lambda l: (0, u_col0 + l))
    b2 = pl.pallas_call(
        functools.partial(_s5_state_in_kernel, nc=nc),
        grid=(nlb,),
        in_specs=[u_spec, pl.BlockSpec((None, S5_CHUNK * LANES, nst2), lambda l: (l, 0, 0))],
        out_specs=pl.BlockSpec((nc, nst2), lambda l: (0, l)),
        out_shape=jax.ShapeDtypeStruct((nc, nlb * nst2), jnp.float32),
        scratch_shapes=[pltpu.VMEM((s, LANES), jnp.float32),
                        pltpu.VMEM((nc, S5_CHUNK * LANES), jnp.bfloat16)],
        compiler_params=_params(("arbitrary",)),
        name="s5_state_in",
    )(z, pblk)
    h3 = pl.pallas_call(
        functools.partial(_s5_scan_kernel, nc=nc, nlb=nlb),
        out_shape=jax.ShapeDtypeStruct((nc, nlb, nst2), jnp.float32),
        compiler_params=pltpu.CompilerParams(vmem_limit_bytes=VMEM_LIMIT_BYTES),
        name="s5_scan",
    )(b2.reshape(nc, nlb, nst2), at.reshape(nlb, nst2))
    return pl.pallas_call(
        functools.partial(_s5_out_kernel, nc=nc),
        grid=(nlb,),
        in_specs=[u_spec,
                  pl.BlockSpec((nc, nst2), lambda l: (0, l)),
                  pl.BlockSpec((None, (S5_CHUNK + 1) * LANES, LANES), lambda l: (l, 0, 0)),
                  pl.BlockSpec((None, S5_CHUNK, LANES, nst2), lambda l: (l, 0, 0, 0)),
                  pl.BlockSpec((1, LANES), lambda l: (0, l))],
        out_specs=pl.BlockSpec((s, LANES), lambda l: (0, l)),
        out_shape=jax.ShapeDtypeStruct((s, nlb * LANES), jnp.bfloat16),
        scratch_shapes=[pltpu.VMEM((s, LANES), jnp.float32),
                        pltpu.VMEM((nc, S5_CHUNK * LANES), jnp.bfloat16),
                        pltpu.VMEM((s, LANES), jnp.float32)],
        compiler_params=_params(("arbitrary",)),
        name="s5_out",
    )(z, h3.reshape(nc, nlb * nst2), krev, qblk, d_skip.astype(jnp.float32).reshape(1, -1))


def _mixout_kernel(*refs, alpha, n_pat):
    o_refs = refs[:n_pat]
    lse_refs = refs[n_pat:2 * n_pat]
    (yg_ref, x_ref, wglu_ref, bglu_ref, ga_ref, gs_ref, wout_ref,
     gt_ref, lg_ref, lb_ref, sc_ref, sh_ref, x1_ref, h2_ref, o_scr, lse_scr) = refs[2 * n_pat:]
    planes, tm, _ = o_scr.shape
    aw = planes * LANES

    def token_major(ref, scr):
        dil = ref.shape[0]
        if dil == 1:
            return ref[0].astype(jnp.float32)
        n_planes = ref.shape[2] // LANES
        for r in range(dil):
            for cc in range(n_planes):
                scr[cc, pl.ds(r, tm // dil, stride=dil), :] = (
                    ref[r, :, cc * LANES:(cc + 1) * LANES].astype(jnp.float32))
        return jnp.concatenate([scr[cc] for cc in range(n_planes)], axis=1)

    lses = [token_major(ref, lse_scr) for ref in lse_refs]
    top = functools.reduce(jnp.maximum, lses)
    ws = [jnp.exp(v - top) for v in lses]
    inv = 1.0 / functools.reduce(lambda u, v: u + v, ws)
    head_of_lane = lax.broadcasted_iota(jnp.int32, (LANES, aw), 1) // HEAD_DIM
    spread = (lax.broadcasted_iota(jnp.int32, (LANES, aw), 0) == head_of_lane).astype(jnp.bfloat16)
    spread2 = jnp.concatenate([spread, spread], axis=0)
    a = jnp.zeros((tm, aw), jnp.float32)
    for w, o_ref in zip(ws, o_refs):
        w = w * inv
        w_hi = w.astype(jnp.bfloat16)
        w_lo = (w - w_hi.astype(jnp.float32)).astype(jnp.bfloat16)
        wide = jnp.dot(jnp.concatenate([w_hi, w_lo], axis=1), spread2, preferred_element_type=jnp.float32)
        a = a + wide * token_major(o_ref, o_scr)
    ra = a * lax.rsqrt(jnp.mean(a * a, axis=-1, keepdims=True) + LN_EPS) * ga_ref[...]
    yg = yg_ref[...]
    y = yg.astype(jnp.float32)
    gate = jax.nn.sigmoid(jnp.dot(yg, wglu_ref[...], preferred_element_type=jnp.float32) + bglu_ref[...])
    y = y * gate
    ry = y * lax.rsqrt(jnp.mean(y * y, axis=-1, keepdims=True) + LN_EPS) * gs_ref[...]
    mix = (jnp.dot(ra.astype(jnp.bfloat16), wout_ref[0:aw, :], preferred_element_type=jnp.float32)
           + jnp.dot(ry.astype(jnp.bfloat16), wout_ref[aw:, :], preferred_element_type=jnp.float32))
    t = alpha * x_ref[...] + (1.0 + gt_ref[...]) * mix
    x1 = _ln_rows(t) * lg_ref[...] + lb_ref[...]
    x1_ref[...] = x1
    h2_ref[...] = (_ln_rows(x1) * (1.0 + sc_ref[...]) + sh_ref[...]).astype(h2_ref.dtype)


def _mixout(attn_parts, yg, x2, w_glu_bf, b_glu, g_attn, g_ssm, w_out_bf, gt1, ln_g, ln_b, sc2, sh2, alpha):
    s, d = x2.shape
    aw = attn_parts[0][0].shape[-1]
    sw = yg.shape[1]
    n_pat = len(attn_parts)
    tm = min(512, s)
    row = lambda i: (i, 0)
    fix = lambda i: (0, 0)
    vec = lambda n: pl.BlockSpec((1, n), fix)
    res = lambda a: pl.BlockSpec((a.shape[0], tm // a.shape[0], a.shape[2]), lambda i: (0, i, 0))
    return pl.pallas_call(
        functools.partial(_mixout_kernel, alpha=alpha, n_pat=n_pat),
        grid=(s // tm,),
        in_specs=([res(o) for o, _ in attn_parts] + [res(v) for _, v in attn_parts]
                  + [pl.BlockSpec((tm, sw), row), pl.BlockSpec((tm, d), row),
                     pl.BlockSpec((sw, sw), fix), vec(sw), vec(aw), vec(sw),
                     pl.BlockSpec((aw + sw, d), fix), vec(d), vec(d), vec(d), vec(d), vec(d)]),
        out_specs=[pl.BlockSpec((tm, d), row), pl.BlockSpec((tm, d), row)],
        out_shape=[jax.ShapeDtypeStruct((s, d), jnp.float32), jax.ShapeDtypeStruct((s, d), jnp.bfloat16)],
        scratch_shapes=[pltpu.VMEM((aw // LANES, tm, LANES), jnp.float32),
                        pltpu.VMEM((1, tm, LANES), jnp.float32)],
        compiler_params=_params(("arbitrary",), [k in (2 * n_pat + 2, 2 * n_pat + 6) for k in range(2 * n_pat + 12)]),
        name="mixout",
    )(*[o for o, _ in attn_parts], *[l for _, l in attn_parts], yg, x2, w_glu_bf, b_glu.reshape(1, -1),
      g_attn.reshape(1, -1), g_ssm.reshape(1, -1), w_out_bf, gt1, ln_g.reshape(1, -1), ln_b.reshape(1, -1),
      sc2, sh2)


def _top_values(sc, k, want_rank=False):
    vals = []
    for j in range(k):
        m = jnp.max(sc, axis=0, keepdims=True)
        vals.append(m)
        sc = jnp.where(sc == m, BIG_NEG + j * RANK_STEP, sc)
    rank = None
    if want_rank:
        rank = jnp.where(sc < BIG_NEG + k * RANK_STEP, jnp.round((sc - BIG_NEG) * (1.0 / RANK_STEP)), float(k))
    return vals, rank


def _peerq_kernel(h2_ref, wqp_ref, keys_ref, c0_ref, r1_ref, e0_ref, e1_ref, sc_scr):
    k = PEER_TOPK
    tm = h2_ref.shape[0]
    qp = jnp.dot(h2_ref[...], wqp_ref[...], preferred_element_type=jnp.float32).astype(jnp.bfloat16)
    nt = (((1,), (1,)), ((), ()))
    for hi in range(2 * PEER_HEADS):
        col = hi * PEER_SUB_DIM
        sc_scr[hi] = lax.dot_general(keys_ref[hi % 2], qp[:, col:col + PEER_SUB_DIM], nt,
                                     preferred_element_type=jnp.float32)
    def body(h, carry):
        s0 = sc_scr[2 * h]
        s1 = sc_scr[2 * h + 1]
        top0, _ = _top_values(s0, k)
        top1, rank1 = _top_values(s1, k, want_rank=True)
        cands = [top0[i] + top1[j] for i in range(k) for j in range(k) if (i + 1) * (j + 1) <= k]
        pad = (-len(cands)) % SUBLANES
        cmat = jnp.concatenate(cands + [jnp.full_like(cands[0], BIG_NEG)] * pad, axis=0)
        best, _ = _top_values(cmat, k)
        thr = best[k - 1]
        m0, m1 = top0[0], top1[0]
        zsum = jnp.sum(jnp.where(cmat >= thr, jnp.exp(cmat - (m0 + m1)), 0.0), axis=0, keepdims=True)
        count0 = jnp.zeros_like(s0)
        for j in range(k):
            count0 = jnp.where(s0 + top1[j] >= thr, float(j + 1), count0)
        c0_ref[h] = count0
        r1_ref[h] = rank1.astype(r1_ref.dtype)
        e0_ref[h] = jnp.exp(s0 - m0) / zsum
        e1_ref[h] = jnp.exp(s1 - m1).astype(e1_ref.dtype)
        return carry

    lax.fori_loop(0, PEER_HEADS, body, 0)


def _peerq(h2, w_qp_bf, keys_bf):
    s, d = h2.shape
    tm = min(512, s)
    nk = keys_bf.shape[1]
    stat = pl.BlockSpec((PEER_HEADS, nk, tm), lambda i: (0, 0, i))
    words = jax.ShapeDtypeStruct((PEER_HEADS, nk, s), jnp.float32)
    halfs = jax.ShapeDtypeStruct((PEER_HEADS, nk, s), jnp.bfloat16)
    return pl.pallas_call(
        _peerq_kernel,
        grid=(s // tm,),
        in_specs=[pl.BlockSpec((tm, d), lambda i: (i, 0)),
                  pl.BlockSpec(w_qp_bf.shape, lambda i: (0, 0)),
                  pl.BlockSpec(keys_bf.shape, lambda i: (0, 0, 0))],
        out_specs=[stat, stat, stat, stat],
        out_shape=[words, halfs, words, halfs],
        scratch_shapes=[pltpu.VMEM((2 * PEER_HEADS, nk, tm), jnp.float32)],
        compiler_params=_params(("arbitrary",), [False, True, True]),
        name="peerq",
    )(h2, w_qp_bf, keys_bf)


def _peer_kernel(ua_ref, ub_ref, vt_ref, h2t_ref, c0_ref, e0_ref, r1_ref, e1_ref, o_ref,
                 a_s0, a_s1, w_s0, w_s1, *, n_pairs, lane_chunk):
    m = pl.program_id(1)
    nk = PEER_N_KEYS
    te, tm = a_s0.shape
    rows_per_tile = te // nk
    packed = 2 * SUBLANES
    bf = jnp.bfloat16

    def stage_a(u_ref, dst):
        dst[...] = jnp.dot(u_ref[...], h2t_ref[...], preferred_element_type=jnp.float32)

    def row_tile(ref, h, i0, ls):
        return jnp.broadcast_to(ref[h, i0:i0 + 1, ls], (packed, lane_chunk)).astype(bf)

    def stage_b(src, dst, row_off, ls):
        for rr in range(rows_per_tile):
            rows = slice(rr * nk, (rr + 1) * nk)
            i0 = row_off + rr
            a = src[rows, ls]
            act = 0.5 * a * (1.0 + lax.erf(a * (2.0 ** -0.5)))
            gate = jnp.zeros((nk // packed, packed, lane_chunk), bf)
            for h in range(PEER_HEADS):
                c0 = row_tile(c0_ref, h, i0, ls)
                e0 = row_tile(e0_ref, h, i0, ls)
                r1 = r1_ref[h, :, ls].reshape(nk // packed, packed, lane_chunk)
                e1 = e1_ref[h, :, ls].reshape(nk // packed, packed, lane_chunk)
                gate = gate + jnp.where(r1 < c0, e0 * e1, jnp.zeros_like(e1))
            w = gate * act.astype(bf).reshape(nk // packed, packed, lane_chunk)
            dst[rows, ls] = w.reshape(nk, lane_chunk)

    def tile(src, dst, half, row_off):
        for c in range(tm // lane_chunk):
            ls = slice(c * lane_chunk, (c + 1) * lane_chunk)
            stage_b(src, dst, row_off, ls)
            o_ref[:, ls] += jnp.dot(vt_ref[:, half * te:(half + 1) * te], dst[:, ls],
                                    preferred_element_type=jnp.float32)

    @pl.when(m == 0)
    def _():
        o_ref[...] = jnp.zeros_like(o_ref)
        stage_a(ub_ref, a_s0)

    @pl.when(jnp.logical_and(m > 0, m < n_pairs))
    def _():
        stage_a(ua_ref, a_s1)
        tile(a_s0, w_s0, 0, 0)
        stage_a(ub_ref, a_s0)
        tile(a_s1, w_s1, 1, rows_per_tile)

    @pl.when(m == n_pairs)
    def _():
        stage_a(ua_ref, a_s1)
        tile(a_s0, w_s0, 0, 0)
        tile(a_s1, w_s1, 1, rows_per_tile)


def _peer(h2t, u_bf, vt_bf, a0, n1, e0, e1):
    d, s = h2t.shape
    ne = u_bf.shape[0]
    nk = PEER_N_KEYS
    tm = min(512, s)
    te = (SUBLANES // 2) * nk
    n_tiles = ne // te
    n_pairs = n_tiles // 2
    rows = pl.BlockSpec((PEER_HEADS, SUBLANES, tm), lambda i, m: (0, jnp.maximum(m - 1, 0), i))
    full_stat = pl.BlockSpec((PEER_HEADS, nk, tm), lambda i, m: (0, 0, i))
    return pl.pallas_call(
        functools.partial(_peer_kernel, n_pairs=n_pairs, lane_chunk=min(256, tm)),
        grid=(s // tm, n_pairs + 1),
        in_specs=[pl.BlockSpec((te, d), lambda i, m: (jnp.maximum(2 * m - 1, 0), 0)),
                  pl.BlockSpec((te, d), lambda i, m: (jnp.minimum(2 * m, n_tiles - 1), 0)),
                  pl.BlockSpec((d, 2 * te), lambda i, m: (0, jnp.maximum(m - 1, 0))),
                  pl.BlockSpec((d, tm), lambda i, m: (0, i)),
                  rows, rows, full_stat, full_stat],
        out_specs=pl.BlockSpec((d, tm), lambda i, m: (0, i)),
        out_shape=jax.ShapeDtypeStruct((d, s), jnp.float32),
        scratch_shapes=[pltpu.VMEM((te, tm), jnp.float32), pltpu.VMEM((te, tm), jnp.float32),
                        pltpu.VMEM((te, tm), jnp.bfloat16), pltpu.VMEM((te, tm), jnp.bfloat16)],
        compiler_params=_params(("arbitrary", "arbitrary")),
        name="peer",
    )(u_bf, u_bf, vt_bf, h2t, a0, e0, n1, e1)


def _transpose_cast_kernel(v_ref, o_ref):
    o_ref[...] = v_ref[...].T.astype(o_ref.dtype)


def _transpose_cast(v, dtype):
    ne, d = v.shape
    te = min(1024, ne)
    return pl.pallas_call(
        _transpose_cast_kernel,
        grid=(ne // te,),
        in_specs=[pl.BlockSpec((te, d), lambda j: (j, 0))],
        out_specs=pl.BlockSpec((d, te), lambda j: (0, j)),
        out_shape=jax.ShapeDtypeStruct((d, ne), dtype),
        compiler_params=_params(("arbitrary",)),
        name="expert_v_t",
    )(v)


def _final_kernel(ft_ref, x1_ref, gt_ref, lg_ref, lb_ref, o_ref, *, alpha):
    ffn = ft_ref[...].T
    t = alpha * x1_ref[...] + (1.0 + gt_ref[...]) * ffn
    o_ref[...] = _ln_rows(t) * lg_ref[...] + lb_ref[...]


def _final(ffn_t, x1, gt2, ln_g, ln_b, alpha):
    s, d = x1.shape
    tm = min(512, s)
    vec = pl.BlockSpec((1, d), lambda i: (0, 0))
    return pl.pallas_call(
        functools.partial(_final_kernel, alpha=alpha),
        grid=(s // tm,),
        in_specs=[pl.BlockSpec((d, tm), lambda i: (0, i)), pl.BlockSpec((tm, d), lambda i: (i, 0)),
                  vec, vec, vec],
        out_specs=pl.BlockSpec((tm, d), lambda i: (i, 0)),
        out_shape=jax.ShapeDtypeStruct((s, d), jnp.float32),
        compiler_params=_params(("arbitrary",)),
        name="final",
    )(ffn_t, x1, gt2, ln_g.reshape(1, -1), ln_b.reshape(1, -1))


def kernel(x, c, positions, w_ada, b_ada, w_in, lam_re, lam_im, log_dt, ssm_b_re, ssm_b_im, ssm_c_re,
           ssm_c_im, ssm_d, w_glu, b_glu, g_attn, g_ssm, w_out, ln1_g, ln1_b, w_qp, sub_keys, expert_u,
           expert_v, ln2_g, ln2_b):
    b, s, d = x.shape
    assert b == 1, "one sequence per call"
    depth = w_ada.shape[0]
    alpha = (2.0 * depth) ** 0.25
    bf = jnp.bfloat16
    x2 = x.reshape(s, d)
    pos = positions.reshape(s, 1)
    dilations = [dil for _, dil in DILATED_PATTERNS if dil > 1]
    for l in range(depth):
        mod = _ada(c, w_ada[l], b_ada[l])
        sh1, sc1, gt1, sh2, sc2, gt2 = [mod[:, k * d:(k + 1) * d] for k in range(ADA_CHUNKS)]
        z, *z_res = _inproj(x2, pos, sc1, sh1, w_in[l].astype(bf), dilations)
        z_dil = [z.reshape(1, s, -1) if dil == 1 else z_res[dilations.index(dil)]
                 for _, dil in DILATED_PATTERNS]
        attn = _attention(z, z_dil)
        yg = _s5(z, lam_re[l], lam_im[l], log_dt[l], ssm_b_re[l], ssm_b_im[l],
                 ssm_c_re[l], ssm_c_im[l], ssm_d[l])
        x1, h2 = _mixout(attn, yg, x2, w_glu[l].astype(bf), b_glu[l], g_attn[l], g_ssm[l],
                         w_out[l].astype(bf), gt1, ln1_g[l], ln1_b[l], sc2, sh2, alpha)
        a0, n1, e0, e1 = _peerq(h2, w_qp[l].astype(bf), sub_keys[l].astype(bf))
        ffn_t = _peer(h2.T, expert_u[l].astype(bf), _transpose_cast(expert_v[l], bf), a0, n1, e0, e1)
        x2 = _final(ffn_t, x1, gt2, ln2_g[l], ln2_b[l], alpha)
    return x2.reshape(b, s, d)
```
